```python
import jax
import jax.numpy as jnp
from jax import lax
import numpy as np

D_MODEL = 2048
BATCH = 2
SEQ = 4096
DEPTH = 2
DEC_BATCH = 32
DEC_SEQ = 4
PAST_LEN = 8192
PAGE_SIZE = 128

POOL_WINDOWS = (2, 4, 8, 16)
POOL_WIDTH = D_MODEL // 2
POOL_GROUP = POOL_WIDTH // len(POOL_WINDOWS)
POOL_HIST = max(POOL_WINDOWS) - 1
N_HEADS = 16
KV_HEADS = 4
HEAD_DIM = 64
Q_PER_KV = N_HEADS // KV_HEADS
CMP_LEN = 32
CMP_STRIDE = 16
CMP_HID = 256
SEL_BLOCK = 64
SEL_TOPK = 16
WINDOW = 512
Q_BLOCK = 128
N_KV_PLANES = 6
N_CACHED_PLANES = 4
FORCE_BONUS = 1.0e4
N_GROUPS = 4
EXPERTS_PER_GROUP = 4
N_EXPERTS = N_GROUPS * EXPERTS_PER_GROUP
EXPERT_TOP_K = 2
D_EXPERT = D_MODEL // 4
DEEPNORM_ALPHA = (2 * DEPTH) ** 0.25
DEEPNORM_BETA = (8 * DEPTH) ** -0.25
LN_EPS = 1e-5
NEG_INF = -1e30
Q_WIDTH = N_HEADS * HEAD_DIM
KV_WIDTH = N_KV_PLANES * KV_HEADS * HEAD_DIM
N_IN = POOL_WIDTH + Q_WIDTH + KV_WIDTH + 3 * N_HEADS + 2 * D_MODEL

kernel_name = 'hybrid_pool_nsa_hmoe_step'


def alibi_slopes():
    return jnp.exp2(-8.0 * jnp.arange(1, N_HEADS + 1, dtype=jnp.float32) / N_HEADS)


def layer_norm(x, g, b):
    xf = x.astype(jnp.float32)
    mu = xf.mean(-1, keepdims=True)
    var = jnp.square(xf - mu).mean(-1, keepdims=True)
    return ((xf - mu) * lax.rsqrt(var + LN_EPS) * g + b).astype(x.dtype)


def masked_softmax(s, mask):
    s = jnp.where(mask, s.astype(jnp.float32), NEG_INF)
    return jnp.where(mask, jax.nn.softmax(s, axis=-1), 0.0)


def pool_mixer(u_pool, hist, pos, w_grp, scale):
    B, T, _ = u_pool.shape
    ext = jnp.concatenate([hist.astype(u_pool.dtype), u_pool], axis=1).astype(jnp.float32)
    cs = jnp.pad(jnp.cumsum(ext, axis=1), ((0, 0), (1, 0), (0, 0)))
    end = cs[:, POOL_HIST + 1:]
    diffs = []
    for g, w in enumerate(POOL_WINDOWS):
        ch = slice(g * POOL_GROUP, (g + 1) * POOL_GROUP)
        start = cs[:, POOL_HIST + 1 - w: POOL_HIST + 1 - w + T, ch]
        count = jnp.minimum(pos + 1, w).astype(jnp.float32)[None, :, None]
        diffs.append((end[..., ch] - start) / count - ext[:, POOL_HIST:, ch])
    d = jnp.stack(diffs, axis=2)
    y = jnp.einsum('btgc,gce->btge', d, w_grp.astype(jnp.float32)).reshape(B, T, POOL_WIDTH)
    return (y * scale).astype(u_pool.dtype)


def compress(raw, w1, pe, w2):
    B, tk = raw.shape[:2]
    r = CMP_LEN // CMP_STRIDE
    n_cmp = (tk - CMP_LEN) // CMP_STRIDE + 1
    n_chunk = n_cmp + r - 1
    chunks = raw[:, :n_chunk * CMP_STRIDE].reshape(B, n_chunk, CMP_STRIDE, KV_HEADS, HEAD_DIM)
    h = None
    for m in range(r):
        seg = slice(m * CMP_STRIDE, (m + 1) * CMP_STRIDE)
        part = chunks[:, m:m + n_cmp] + pe[seg][:, None, :]
        t = jnp.einsum('bnsgd,sdh->bngh', part, w1[seg])
        h = t if h is None else h + t
    return jnp.einsum('bngh,hd->bngd', jax.nn.gelu(h), w2)


def nsa_attention(q, gates, kc, vc, ks, vs, kw, vw, win_pos0, q_pos0):
    B, T = q.shape[:2]
    qb = min(Q_BLOCK, T)
    nqb = T // qb
    tk = ks.shape[1]
    n_cmp = kc.shape[1]
    n_sel = -(-tk // SEL_BLOCK)
    n_top = min(SEL_TOPK, n_sel)
    pad = n_sel * SEL_BLOCK - tk

    def to_blocks(a):
        a = jnp.pad(a, ((0, 0), (0, pad), (0, 0), (0, 0)))
        return a.reshape(B, n_sel, SEL_BLOCK, KV_HEADS, HEAD_DIM).transpose(0, 3, 1, 2, 4)

    ks_b, vs_b = to_blocks(ks), to_blocks(vs)
    pad_w = ((0, 0), (WINDOW, 0), (0, 0), (0, 0))
    kw_p, vw_p = jnp.pad(kw, pad_w), jnp.pad(vw, pad_w)
    cmp_start = jnp.arange(n_cmp) * CMP_STRIDE
    cmp_end = cmp_start + CMP_LEN - 1
    sel_start = jnp.arange(n_sel) * SEL_BLOCK
    cover = ((cmp_start[:, None] < sel_start[None, :] + SEL_BLOCK)
             & (cmp_end[:, None] >= sel_start[None, :])).astype(jnp.float32)
    slopes = alibi_slopes().reshape(KV_HEADS, Q_PER_KV)
    scale = HEAD_DIM ** -0.5
    q_blocks = q.reshape(B, nqb, qb, KV_HEADS, Q_PER_KV, HEAD_DIM).transpose(1, 0, 2, 3, 4, 5)
    g_blocks = gates.reshape(B, nqb, qb, KV_HEADS, Q_PER_KV, 3).transpose(1, 0, 2, 3, 4, 5)
    b_idx = jnp.arange(B)[:, None, None, None]
    g_idx = jnp.arange(KV_HEADS)[None, None, :, None]
    sel_ids = jnp.arange(n_sel)
    in_block = jnp.arange(SEL_BLOCK)
    win_off = jnp.arange(WINDOW + qb)

    def one_block(args):
        qblk, gblk, blk = args
        p0 = q_pos0 + blk * qb
        qpos = p0 + jnp.arange(qb)
        qs = qblk * scale
        s_c = jnp.einsum('bqghd,bngd->bqghn', qs, kc).astype(jnp.float32)
        d_c = (qpos[:, None] - cmp_end[None, :]).astype(jnp.float32)
        s_c = s_c - slopes[None, None, :, :, None] * d_c[None, :, None, None, :]
        p_c = masked_softmax(s_c, (d_c >= 0)[None, :, None, None, :])
        o_c = jnp.einsum('bqghn,bngd->bqghd', p_c.astype(vc.dtype), vc)
        imp = jnp.einsum('bqgn,nj->bqgj', p_c.sum(axis=3), cover)
        cur = qpos // SEL_BLOCK
        forced = ((sel_ids[None, :] == 0) | (sel_ids[None, :] == cur[:, None])
                  | (sel_ids[None, :] == cur[:, None] - 1))
        visible = sel_start[None, :] <= qpos[:, None]
        score = jnp.where(visible[None, :, None, :],
                          imp + jnp.where(forced, FORCE_BONUS, 0.0)[None, :, None, :], -1.0)
        _, top = lax.top_k(score, n_top)
        k_g = ks_b[b_idx, g_idx, top]
        v_g = vs_b[b_idx, g_idx, top]
        kpos = top[..., None] * SEL_BLOCK + in_block
        d_s = (qpos[None, :, None, None, None] - kpos).astype(jnp.float32)
        s_s = jnp.einsum('bqghd,bqgnkd->bqghnk', qs, k_g).astype(jnp.float32)
        s_s = s_s - slopes[None, None, :, :, None, None] * d_s[:, :, :, None]
        s_s = s_s.reshape(B, qb, KV_HEADS, Q_PER_KV, n_top * SEL_BLOCK)
        m_s = (d_s >= 0).reshape(B, qb, KV_HEADS, 1, n_top * SEL_BLOCK)
        p_s = masked_softmax(s_s, m_s)
        o_s = jnp.einsum('bqghk,bqgkd->bqghd', p_s.astype(v_g.dtype),
                         v_g.reshape(B, qb, KV_HEADS, n_top * SEL_BLOCK, HEAD_DIM))
        k_w = lax.dynamic_slice_in_dim(kw_p, p0 - win_pos0, WINDOW + qb, axis=1)
        v_w = lax.dynamic_slice_in_dim(vw_p, p0 - win_pos0, WINDOW + qb, axis=1)
        kpos_w = p0 - WINDOW + win_off
        d_w = qpos[:, None] - kpos_w[None, :]
        m_w = (d_w >= 0) & (d_w < WINDOW) & (kpos_w >= win_pos0)[None, :]
        s_w = jnp.einsum('bqghd,bkgd->bqghk', qs, k_w).astype(jnp.float32)
        s_w = s_w - slopes[None, None, :, :, None] * d_w.astype(jnp.float32)[None, :, None, None, :]
        p_w = masked_softmax(s_w, m_w[None, :, None, None, :])
        o_w = jnp.einsum('bqghk,bkgd->bqghd', p_w.astype(v_w.dtype), v_w)
        return gblk[..., 0:1] * o_c + gblk[..., 1:2] * o_s + gblk[..., 2:3] * o_w

    out = lax.map(one_block, (q_blocks, g_blocks, jnp.arange(nqb)))
    return out.transpose(1, 0, 2, 3, 4, 5).reshape(B, T, Q_WIDTH)


def mixing_sublayer(u, pos0, past_kv, past_win, pool_hist, win_keep, p):
    B, T, _ = u.shape
    proj = jnp.einsum('btd,dn->btn', u, p['w_in'])
    o1 = POOL_WIDTH
    o2 = o1 + Q_WIDTH
    o3 = o2 + KV_WIDTH
    o4 = o3 + 3 * N_HEADS
    u_pool = proj[..., :o1]
    q = proj[..., o1:o2].reshape(B, T, N_HEADS, HEAD_DIM)
    kv = proj[..., o2:o3].reshape(B, T, N_KV_PLANES, KV_HEADS, HEAD_DIM)
    g_nsa = jax.nn.sigmoid(proj[..., o3:o4]).reshape(B, T, N_HEADS, 3)
    g_pool, g_attn = jnp.split(jax.nn.sigmoid(proj[..., o4:]), 2, axis=-1)
    pos = pos0 + jnp.arange(T)
    y_pool = pool_mixer(u_pool, pool_hist, pos, p['pool_w'], p['pool_scale'])
    new_rows = kv[:, :, :N_CACHED_PLANES]
    rows = jnp.concatenate([past_kv.astype(new_rows.dtype), new_rows], axis=1)
    kc = compress(rows[:, :, 0], p['cmp_w1'][0], p['cmp_pe'][0], p['cmp_w2'][0])
    vc = compress(rows[:, :, 1], p['cmp_w1'][1], p['cmp_pe'][1], p['cmp_w2'][1])
    win = jnp.concatenate([past_win.astype(kv.dtype), kv[:, :, N_CACHED_PLANES:]], axis=1)
    win_pos0 = pos0 - past_win.shape[1]
    y_attn = nsa_attention(q, g_nsa, kc, vc, rows[:, :, 2], rows[:, :, 3],
                           win[:, :, 0], win[:, :, 1], win_pos0, pos0)
    merged = (g_pool * jnp.einsum('btc,cd->btd', y_pool, p['w_up_pool'])
              + g_attn * jnp.einsum('btc,cd->btd', y_attn, p['w_up_nsa']))
    out = jnp.einsum('btd,de->bte', merged, p['w_out'])
    new_win = win[:, win.shape[1] - win_keep:]
    new_hist = jnp.concatenate([pool_hist.astype(u_pool.dtype), u_pool], axis=1)[:, -POOL_HIST:]
    return out, new_rows, new_win, new_hist


def hier_moe(u, w_rg, b_rg, w_re, b_re, w_gate, w_up, w_down):
    B, T, D = u.shape
    xt = u.reshape(B * T, D)
    p_grp = jax.nn.softmax((jnp.einsum('nd,dg->ng', xt, w_rg) + b_rg).astype(jnp.float32), axis=-1)
    p_top, grp = lax.top_k(p_grp, 1)
    le = (jnp.einsum('nd,de->ne', xt, w_re) + b_re).astype(jnp.float32)
    le = le.reshape(B * T, N_GROUPS, EXPERTS_PER_GROUP)
    le = jnp.take_along_axis(le, grp[:, :, None], axis=1)[:, 0]
    v, e_idx = lax.top_k(jax.nn.softmax(le, axis=-1), EXPERT_TOP_K)
    wts = p_top * (v / v.sum(-1, keepdims=True))
    eid = grp * EXPERTS_PER_GROUP + e_idx
    comb = jnp.einsum('nk,nke->ne', wts, jax.nn.one_hot(eid, N_EXPERTS, dtype=jnp.float32))
    h = jax.nn.silu(jnp.einsum('nd,edf->nef', xt, w_gate)) * jnp.einsum('nd,edf->nef', xt, w_up)
    y = jnp.einsum('nef,efd->nd', h * comb[:, :, None].astype(h.dtype), w_down)
    return y.reshape(B, T, D)


def trunk_layer(x, c, pos0, past_kv, past_win, pool_hist, win_keep, p):
    mod = jnp.einsum('bd,de->be', jax.nn.silu(c), p['w_ada']) + p['b_ada']
    shift1, scale1, gate1, shift2, scale2, gate2 = [m[:, None, :] for m in jnp.split(mod, 6, axis=-1)]
    u = x * (1.0 + scale1) + shift1
    mix, new_rows, new_win, new_hist = mixing_sublayer(u, pos0, past_kv, past_win, pool_hist, win_keep, p)
    x = layer_norm(DEEPNORM_ALPHA * x + gate1 * mix, p['ln1_g'], p['ln1_b'])
    u = x * (1.0 + scale2) + shift2
    ffn = hier_moe(u, p['w_rg'], p['b_rg'], p['w_re'], p['b_re'], p['w_gate'], p['w_up'], p['w_down'])
    x = layer_norm(DEEPNORM_ALPHA * x + gate2 * ffn, p['ln2_g'], p['ln2_b'])
    return x, new_rows, new_win, new_hist


def setup_inputs(seed: int = 0) -> dict:
    key = jax.random.key(seed)
    ks = jax.random.split(key, 32)

    def normal(k, shape, std):
        return jax.random.normal(k, shape, jnp.float32) * std

    n_pages = PAST_LEN // PAGE_SIZE
    n_used = DEC_BATCH * n_pages
    n_pool = n_used + n_used // 4
    win_buf = min(WINDOW, PAST_LEN)
    D = D_MODEL
    page_table = jax.random.permutation(ks[7], n_pool)[:n_used].reshape(DEC_BATCH, n_pages).astype(jnp.int32)
    inputs = {}
    inputs['x_prompt'] = normal(ks[0], (BATCH, SEQ, D), 1.0)
    inputs['x_sample'] = normal(ks[1], (DEC_BATCH, DEC_SEQ, D), 1.0)
    inputs['c_prompt'] = normal(ks[2], (BATCH, D), 1.0)
    inputs['c_sample'] = normal(ks[3], (DEC_BATCH, D), 1.0)
    inputs['cache_kv'] = normal(ks[4], (DEPTH, n_pool, PAGE_SIZE, N_CACHED_PLANES, KV_HEADS, HEAD_DIM), 1.0)
    inputs['cache_win'] = normal(ks[5], (DEPTH, DEC_BATCH, win_buf, N_KV_PLANES - N_CACHED_PLANES, KV_HEADS, HEAD_DIM), 1.0)
    inputs['state_pool'] = normal(ks[6], (DEPTH, DEC_BATCH, POOL_HIST, POOL_WIDTH), 1.0)
    inputs['page_table'] = page_table
    inputs['w_ada'] = normal(ks[8], (DEPTH, D, 6 * D), 0.5 * D ** -0.5)
    inputs['b_ada'] = normal(ks[9], (DEPTH, 6 * D), 0.02)
    inputs['w_in'] = normal(ks[10], (DEPTH, D, N_IN), D ** -0.5)
    inputs['cmp_w1'] = normal(ks[11], (DEPTH, 2, CMP_LEN, HEAD_DIM, CMP_HID), (CMP_LEN * HEAD_DIM) ** -0.5)
    inputs['cmp_pe'] = normal(ks[12], (DEPTH, 2, CMP_LEN, HEAD_DIM), 0.1)
    inputs['cmp_w2'] = normal(ks[13], (DEPTH, 2, CMP_HID, HEAD_DIM), CMP_HID ** -0.5)
    inputs['pool_w'] = normal(ks[14], (DEPTH, len(POOL_WINDOWS), POOL_GROUP, POOL_GROUP), POOL_GROUP ** -0.5)
    inputs['pool_scale'] = 1.0 + normal(ks[15], (DEPTH, POOL_WIDTH), 0.02)
    inputs['w_up_pool'] = normal(ks[16], (DEPTH, POOL_WIDTH, D), POOL_WIDTH ** -0.5)
    inputs['w_up_nsa'] = normal(ks[17], (DEPTH, Q_WIDTH, D), Q_WIDTH ** -0.5)
    inputs['w_out'] = normal(ks[18], (DEPTH, D, D), DEEPNORM_BETA * D ** -0.5)
    inputs['ln1_g'] = 1.0 + normal(ks[19], (DEPTH, D), 0.02)
    inputs['ln1_b'] = normal(ks[20], (DEPTH, D), 0.02)
    inputs['w_rg'] = normal(ks[21], (DEPTH, D, N_GROUPS), D ** -0.5)
    inputs['b_rg'] = normal(ks[22], (DEPTH, N_GROUPS), 0.01)
    inputs['w_re'] = normal(ks[23], (DEPTH, D, N_EXPERTS), D ** -0.5)
    inputs['b_re'] = normal(ks[24], (DEPTH, N_EXPERTS), 0.01)
    inputs['w_gate'] = normal(ks[25], (DEPTH, N_EXPERTS, D, D_EXPERT), D ** -0.5)
    inputs['w_up'] = normal(ks[26], (DEPTH, N_EXPERTS, D, D_EXPERT), D ** -0.5)
    inputs['w_down'] = normal(ks[27], (DEPTH, N_EXPERTS, D_EXPERT, D), DEEPNORM_BETA * D_EXPERT ** -0.5)
    inputs['ln2_g'] = 1.0 + normal(ks[28], (DEPTH, D), 0.02)
    inputs['ln2_b'] = normal(ks[29], (DEPTH, D), 0.02)
    return inputs


def reference(x_prompt, x_sample, c_prompt, c_sample, cache_kv, cache_win, state_pool, page_table,
              w_ada, b_ada, w_in, cmp_w1, cmp_pe, cmp_w2, pool_w, pool_scale, w_up_pool, w_up_nsa, w_out,
              ln1_g, ln1_b, w_rg, b_rg, w_re, b_re, w_gate, w_up, w_down, ln2_g, ln2_b):
    B, T = x_prompt.shape[:2]
    NB = x_sample.shape[0]
    past_len = page_table.shape[1] * PAGE_SIZE
    win_buf = cache_win.shape[2]
    xp, xs = x_prompt, x_sample
    kv_p, win_p, pool_p, kv_s, win_s, pool_s = [], [], [], [], [], []
    empty_kv = jnp.zeros((B, 0, N_CACHED_PLANES, KV_HEADS, HEAD_DIM), x_prompt.dtype)
    empty_win = jnp.zeros((B, 0, N_KV_PLANES - N_CACHED_PLANES, KV_HEADS, HEAD_DIM), x_prompt.dtype)
    empty_hist = jnp.zeros((B, POOL_HIST, POOL_WIDTH), x_prompt.dtype)
    for l in range(DEPTH):
        p = {'w_ada': w_ada[l], 'b_ada': b_ada[l], 'w_in': w_in[l], 'cmp_w1': cmp_w1[l],
             'cmp_pe': cmp_pe[l], 'cmp_w2': cmp_w2[l], 'pool_w': pool_w[l], 'pool_scale': pool_scale[l],
             'w_up_pool': w_up_pool[l], 'w_up_nsa': w_up_nsa[l], 'w_out': w_out[l],
             'ln1_g': ln1_g[l], 'ln1_b': ln1_b[l], 'w_rg': w_rg[l], 'b_rg': b_rg[l],
             'w_re': w_re[l], 'b_re': b_re[l], 'w_gate': w_gate[l], 'w_up': w_up[l],
             'w_down': w_down[l], 'ln2_g': ln2_g[l], 'ln2_b': ln2_b[l]}
        xp, r_p, w_p, h_p = trunk_layer(xp, c_prompt, 0, empty_kv, empty_win, empty_hist, min(WINDOW, T), p)
        past = cache_kv[l, page_table].reshape(NB, past_len, N_CACHED_PLANES, KV_HEADS, HEAD_DIM)
        xs, r_s, w_s, h_s = trunk_layer(xs, c_sample, past_len, past, cache_win[l], state_pool[l], win_buf, p)
        kv_p.append(r_p)
        win_p.append(w_p)
        pool_p.append(h_p)
        kv_s.append(r_s)
        win_s.append(w_s)
        pool_s.append(h_s)
    new_kv_prompt = jnp.stack(kv_p)
    new_win_prompt = jnp.stack(win_p)
    new_pool_prompt = jnp.stack(pool_p)
    new_kv_sample = jnp.stack(kv_s)
    new_win_sample = jnp.stack(win_s)
    new_pool_sample = jnp.stack(pool_s)
    return (xp, xs, new_kv_prompt, new_win_prompt, new_pool_prompt, new_kv_sample, new_win_sample, new_pool_sample)
```

```python
import functools

import jax
import jax.numpy as jnp
from jax import lax
from jax.experimental import pallas as pl
from jax.experimental.pallas import tpu as pltpu

F32 = jnp.float32
BF16 = jnp.bfloat16
I32 = jnp.int32

POOL_WINDOWS = (2, 4, 8, 16)
POOL_HIST = max(POOL_WINDOWS) - 1
N_HEADS = 16
KV_HEADS = 4
HEAD_DIM = 64
Q_PER_KV = N_HEADS // KV_HEADS
CMP_LEN = 32
CMP_STRIDE = 16
SEL_BLOCK = 64
SEL_TOPK = 16
WINDOW = 512
Q_BLOCK = 128
N_KV_PLANES = 6
N_CACHED_PLANES = 4
FORCE_BONUS = 1.0e4
N_GROUPS = 4
EXPERTS_PER_GROUP = 4
N_EXPERTS = N_GROUPS * EXPERTS_PER_GROUP
LN_EPS = 1e-5
NEG_INF = -1e30

LANES = 128
SUBLANES = 8
VMEM_LIMIT = 52 * 1024 * 1024

ROW_TILE = 256
MM_TILE_M = 512
MM_TILE_N = 512
SEL_KV_TILE = 512
EXPERT_TILE = 128


def _cparams(*sem):
    return pltpu.CompilerParams(dimension_semantics=sem, vmem_limit_bytes=VMEM_LIMIT)


def _dot(a, b):
    return jnp.dot(a, b, preferred_element_type=F32)


def _dot_nt(a, b):
    return lax.dot_general(a, b, (((1,), (1,)), ((), ())), preferred_element_type=F32)


def _split_dot(a, b_bf16):
    hi = a.astype(BF16)
    lo = (a - hi.astype(F32)).astype(BF16)
    return _dot(hi, b_bf16) + _dot(lo, b_bf16)


def _masked_softmax(s, valid):
    s = jnp.where(valid, s, NEG_INF)
    m = jnp.max(s, axis=-1, keepdims=True)
    e = jnp.where(valid, jnp.exp(s - m), 0.0)
    l = jnp.sum(e, axis=-1, keepdims=True)
    return e / jnp.where(l > 0.0, l, 1.0)


def _layer_norm(x, g, b):
    mu = jnp.mean(x, axis=-1, keepdims=True)
    xc = x - mu
    var = jnp.mean(xc * xc, axis=-1, keepdims=True)
    return xc * lax.rsqrt(var + LN_EPS) * g + b


def _top_k_mask(score, n_top):
    col = lax.broadcasted_iota(I32, score.shape, 1)
    big = jnp.int32(score.shape[1])

    def body(_, carry):
        sc, sel = carry
        m = jnp.max(sc, axis=-1, keepdims=True)
        idx = jnp.min(jnp.where(sc == m, col, big), axis=-1, keepdims=True)
        hit = col == idx
        return jnp.where(hit, -jnp.inf, sc), jnp.where(hit, 1.0, sel)

    _, sel = lax.fori_loop(0, n_top, body, (score, jnp.zeros(score.shape, F32)))
    return sel


def _adaln_kernel(c_ref, w_ref, b_ref, o_ref):
    c = c_ref[...]
    a = (c * jax.nn.sigmoid(c)).astype(BF16)
    o_ref[0] = _dot(a, w_ref[0].astype(BF16)) + b_ref[0]


def _adaln(c_all, w_ada, b_ada):
    n_layers, d, n6 = w_ada.shape
    mp = c_all.shape[0]
    tn = 1024
    assert n6 % tn == 0
    return pl.pallas_call(
        _adaln_kernel,
        grid=(n_layers, n6 // tn),
        in_specs=[pl.BlockSpec((mp, d), lambda l, n: (0, 0)),
                  pl.BlockSpec((1, d, tn), lambda l, n: (l, 0, n)),
                  pl.BlockSpec((1, 1, tn), lambda l, n: (l, 0, n))],
        out_specs=pl.BlockSpec((1, mp, tn), lambda l, n: (l, 0, n)),
        out_shape=jax.ShapeDtypeStruct((n_layers, mp, n6), F32),
        compiler_params=_cparams("arbitrary", "arbitrary"),
        name="adaln",
    )(c_all, w_ada, b_ada.reshape(n_layers, 1, n6))


class _Rows:
    def __init__(self, batch, seq, n_sample):
        self.batch = batch
        self.seq = seq
        self.n_prompt = batch * seq
        self.n_sample = n_sample
        self.sample_pad = -(-n_sample // MM_TILE_M) * MM_TILE_M
        self.n_rows = self.n_prompt + self.sample_pad
        assert seq % MM_TILE_M == 0 and MM_TILE_M % ROW_TILE == 0
        self.n_ptiles = self.n_prompt // ROW_TILE

    def mod_specs(self, d, group):
        tiles_per_seq = self.seq // ROW_TILE
        last = self.batch - 1
        n_pt = self.n_ptiles
        return [pl.BlockSpec((1, 1, SUBLANES, d),
                             lambda i, *_: (jnp.minimum(i // tiles_per_seq, last), group, 0, 0)),
                pl.BlockSpec((1, 3, ROW_TILE, d),
                             lambda i, *_: (group, 0, jnp.maximum(i - n_pt, 0), 0))]


def _mod_row(is_sample, mp_ref, ms_ref, k):
    return jnp.where(is_sample, ms_ref[0, k], mp_ref[0, 0, k:k + 1, :])


def _modulate_kernel(x_ref, mp_ref, ms_ref, u_ref, *, n_ptiles):
    is_s = pl.program_id(0) >= n_ptiles
    shift = _mod_row(is_s, mp_ref, ms_ref, 0)
    scale = _mod_row(is_s, mp_ref, ms_ref, 1)
    u_ref[...] = (x_ref[...] * (1.0 + scale) + shift).astype(BF16)


def _modulate(rows, x, modp, mods):
    d = x.shape[1]
    return pl.pallas_call(
        functools.partial(_modulate_kernel, n_ptiles=rows.n_ptiles),
        grid=(rows.n_rows // ROW_TILE,),
        in_specs=[pl.BlockSpec((ROW_TILE, d), lambda i: (i, 0))] + rows.mod_specs(d, 0),
        out_specs=pl.BlockSpec((ROW_TILE, d), lambda i: (i, 0)),
        out_shape=jax.ShapeDtypeStruct((rows.n_rows, d), BF16),
        compiler_params=_cparams("arbitrary"),
        name="modulate",
    )(x, modp, mods)


def _mm_kernel(x_ref, w_ref, o_ref, *, act, scale):
    y = _dot(x_ref[...], w_ref[...])
    if act == "sigmoid":
        y = jax.nn.sigmoid(y)
    if scale != 1.0:
        y = y * scale
    o_ref[...] = y.astype(o_ref.dtype)


def _mm(x, w, col0, ncols, tn, out_dtype, act=None, scale=1.0):
    m, k = x.shape
    assert m % MM_TILE_M == 0 and ncols % tn == 0 and col0 % tn == 0
    cb = col0 // tn
    return pl.pallas_call(
        functools.partial(_mm_kernel, act=act, scale=scale),
        grid=(ncols // tn, m // MM_TILE_M),
        in_specs=[pl.BlockSpec((MM_TILE_M, k), lambda n, i: (i, 0)),
                  pl.BlockSpec((k, tn), lambda n, i: (0, cb + n))],
        out_specs=pl.BlockSpec((MM_TILE_M, tn), lambda n, i: (i, n)),
        out_shape=jax.ShapeDtypeStruct((m, ncols), out_dtype),
        compiler_params=_cparams("arbitrary", "arbitrary"),
        name="proj_mm",
    )(x, w)


def _pool_prompt_kernel(cur_ref, prev_ref, pw_ref, sc_ref, o_ref, ext_ref, *, tt, pg):
    i = pl.program_id(1)
    halo = 2 * SUBLANES
    ext_ref[0:halo, :] = jnp.where(i == 0, 0.0, prev_ref[...])
    ext_ref[halo:, :] = cur_ref[...]
    pos = i * tt + lax.broadcasted_iota(I32, (tt, 1), 0)
    for g, w in enumerate(POOL_WINDOWS):
        cols = slice(g * pg, (g + 1) * pg)
        cur = ext_ref[halo:halo + tt, cols]
        acc = cur
        for j in range(1, w):
            acc = acc + ext_ref[halo - j:halo - j + tt, cols]
        count = jnp.minimum(pos + 1, w).astype(F32)
        dlt = acc / count - cur
        y = _dot(dlt.astype(BF16), pw_ref[g].astype(BF16)) * sc_ref[:, cols]
        o_ref[:, cols] = y.astype(o_ref.dtype)


def _pool_prompt(up, batch, seq, pool_w, pool_scale):
    pw = up.shape[1]
    pg = pw // len(POOL_WINDOWS)
    tt = MM_TILE_M
    halo = 2 * SUBLANES
    assert POOL_HIST <= halo and seq % tt == 0
    nt = seq // tt
    return pl.pallas_call(
        functools.partial(_pool_prompt_kernel, tt=tt, pg=pg),
        grid=(batch, nt),
        in_specs=[pl.BlockSpec((tt, pw), lambda b, i: (b * nt + i, 0)),
                  pl.BlockSpec((halo, pw), lambda b, i: (jnp.maximum((b * nt + i) * (tt // halo) - 1, 0), 0)),
                  pl.BlockSpec((len(POOL_WINDOWS), pg, pg), lambda b, i: (0, 0, 0)),
                  pl.BlockSpec((1, pw), lambda b, i: (0, 0))],
        out_specs=pl.BlockSpec((tt, pw), lambda b, i: (b * nt + i, 0)),
        out_shape=jax.ShapeDtypeStruct((batch * seq, pw), BF16),
        scratch_shapes=[pltpu.VMEM((halo + tt, pw), F32)],
        compiler_params=_cparams("arbitrary", "arbitrary"),
        name="pool_prompt",
    )(up, up, pool_w, pool_scale.reshape(1, pw))


def _pool_sample_kernel(ext_ref, pw_ref, sc_ref, o_ref, *, ts, pg, pos0):
    for t in range(ts):
        for g, w in enumerate(POOL_WINDOWS):
            cols = slice(g * pg, (g + 1) * pg)
            cur = ext_ref[POOL_HIST + t, :, cols]
            acc = cur
            for j in range(1, w):
                acc = acc + ext_ref[POOL_HIST + t - j, :, cols]
            count = float(min(pos0 + t + 1, w))
            dlt = acc / count - cur
            y = _dot(dlt.astype(BF16), pw_ref[g].astype(BF16)) * sc_ref[:, cols]
            o_ref[t, :, cols] = y.astype(o_ref.dtype)


def _pool_sample(ext, ts, pos0, pool_w, pool_scale):
    n_ext, nb, pw = ext.shape
    pg = pw // len(POOL_WINDOWS)
    return pl.pallas_call(
        functools.partial(_pool_sample_kernel, ts=ts, pg=pg, pos0=pos0),
        grid=(1,),
        in_specs=[pl.BlockSpec((n_ext, nb, pw), lambda i: (0, 0, 0)),
                  pl.BlockSpec((len(POOL_WINDOWS), pg, pg), lambda i: (0, 0, 0)),
                  pl.BlockSpec((1, pw), lambda i: (0, 0))],
        out_specs=pl.BlockSpec((ts, nb, pw), lambda i: (0, 0, 0)),
        out_shape=jax.ShapeDtypeStruct((ts, nb, pw), BF16),
        compiler_params=_cparams("arbitrary"),
        name="pool_sample",
    )(ext, pool_w, pool_scale.reshape(1, pw))


def _compress_chunks(c, w1_ref, pe_ref, w2_ref):
    n_chunk = c.shape[0]
    ca = (c + pe_ref[0, 0]).astype(BF16)
    cb = (c + pe_ref[0, 1]).astype(BF16)
    ha = _dot(ca, w1_ref[0, 0].astype(BF16))
    hb = _dot(cb, w1_ref[0, 1].astype(BF16))
    h = ha + pltpu.roll(hb, n_chunk - 1, 0)
    return _dot(jax.nn.gelu(h).astype(BF16), w2_ref[0].astype(BF16))


def _compress_prompt_kernel(c_ref, w1_ref, pe_ref, w2_ref, o_ref):
    o_ref[0, 0, 0] = _compress_chunks(c_ref[0, 0, 0], w1_ref, pe_ref, w2_ref)


def _compress_specs(hid, kdim):
    return [pl.BlockSpec((1, 2, kdim, hid), lambda p, *_: (p, 0, 0, 0)),
            pl.BlockSpec((1, 2, 1, kdim), lambda p, *_: (p, 0, 0, 0)),
            pl.BlockSpec((1, hid, HEAD_DIM), lambda p, *_: (p, 0, 0))]


def _compress_prompt(chunks, w1, pe, w2):
    _, b, g, n_chunk, kdim = chunks.shape
    hid = w1.shape[-1]
    return pl.pallas_call(
        _compress_prompt_kernel,
        grid=(2, b, g),
        in_specs=[pl.BlockSpec((1, 1, 1, n_chunk, kdim), lambda p, bi, gi: (p, bi, gi, 0, 0))]
        + _compress_specs(hid, kdim),
        out_specs=pl.BlockSpec((1, 1, 1, n_chunk, HEAD_DIM), lambda p, bi, gi: (p, bi, gi, 0, 0)),
        out_shape=jax.ShapeDtypeStruct((2, b, g, n_chunk, HEAD_DIM), F32),
        compiler_params=_cparams("arbitrary", "arbitrary", "arbitrary"),
        name="compress_prompt",
    )(chunks, w1, pe, w2)


def _compress_sample_kernel(pt_ref, cache_ref, w1_ref, pe_ref, w2_ref, o_ref, buf_ref, sem, *, n_pages, cpp):
    p = pl.program_id(0)
    b = pl.program_id(1)

    def page_copy(j):
        pid = pt_ref[b * n_pages + j]
        return pltpu.make_async_copy(cache_ref.at[p, :, pid], buf_ref.at[:, pl.ds(j * cpp, cpp), :], sem)

    def issue(j, c):
        page_copy(j).start()
        return c

    def drain(j, c):
        page_copy(j).wait()
        return c

    lax.fori_loop(0, n_pages, issue, 0)
    lax.fori_loop(0, n_pages, drain, 0)
    for g in range(KV_HEADS):
        o_ref[0, 0, g] = _compress_chunks(buf_ref[g], w1_ref, pe_ref, w2_ref)


def _compress_sample(page_table, cache_c, w1, pe, w2):
    nb, n_pages = page_table.shape
    _, g, _, cpp, kdim = cache_c.shape
    n_chunk = n_pages * cpp
    hid = w1.shape[-1]
    grid_spec = pltpu.PrefetchScalarGridSpec(
        num_scalar_prefetch=1,
        grid=(2, nb),
        in_specs=[pl.BlockSpec(memory_space=pl.ANY)] + _compress_specs(hid, kdim),
        out_specs=pl.BlockSpec((1, 1, g, n_chunk, HEAD_DIM), lambda p, bi, pt: (p, bi, 0, 0, 0)),
        scratch_shapes=[pltpu.VMEM((g, n_chunk, kdim), F32), pltpu.SemaphoreType.DMA(())],
    )
    return pl.pallas_call(
        functools.partial(_compress_sample_kernel, n_pages=n_pages, cpp=cpp),
        grid_spec=grid_spec,
        out_shape=jax.ShapeDtypeStruct((2, nb, g, n_chunk, HEAD_DIM), F32),
        compiler_params=_cparams("arbitrary", "arbitrary"),
        name="compress_sample",
    )(page_table.reshape(-1), cache_c, w1, pe, w2)


def _attn_prompt_kernel(q_ref, kc_ref, vc_ref, ks_ref, vs_ref, kw_ref, vw_ref, gate_ref, slope_ref,
                        cover_ref, o_ref, m_ref, l_ref, acc_ref, *, seq, n_cmp, n_sel, n_top, wlen):
    qb = Q_BLOCK
    nrow = Q_PER_KV * qb
    i = pl.program_id(2)
    p0 = i * qb
    q = q_ref[0].reshape(nrow, HEAD_DIM)
    slope = slope_ref[0]
    tok = lax.broadcasted_iota(I32, (nrow, 1), 0) % qb
    qpos = p0 + tok
    qpos_q = p0 + lax.broadcasted_iota(I32, (qb, 1), 0)

    n_chunk = kc_ref.shape[3]
    ccol = lax.broadcasted_iota(I32, (1, n_chunk), 1)
    d_c = qpos - (ccol * CMP_STRIDE + (CMP_LEN - 1))
    s_c = _dot_nt(q, kc_ref[0, 0, 0].astype(BF16)) - slope * d_c.astype(F32)
    p_c = _masked_softmax(s_c, (d_c >= 0) & (ccol < n_cmp))
    o_c = _dot(p_c.astype(BF16), vc_ref[0, 0, 0].astype(BF16))

    p_sum = p_c[0:qb]
    for h in range(1, Q_PER_KV):
        p_sum = p_sum + p_c[h * qb:(h + 1) * qb]
    imp = _split_dot(p_sum, cover_ref[...])
    scol = lax.broadcasted_iota(I32, (1, imp.shape[1]), 1)
    cur = qpos_q // SEL_BLOCK
    forced = (scol == 0) | (scol == cur) | (scol == cur - 1)
    visible = scol * SEL_BLOCK <= qpos_q
    score = jnp.where(visible, imp + jnp.where(forced, FORCE_BONUS, 0.0), -1.0)
    score = jnp.where(scol < n_sel, score, -2.0)
    selmask = _top_k_mask(score, n_top).astype(BF16)

    tk = SEL_KV_TILE
    m_ref[...] = jnp.full(m_ref.shape, NEG_INF, F32)
    l_ref[...] = jnp.zeros(l_ref.shape, F32)
    acc_ref[...] = jnp.zeros(acc_ref.shape, F32)
    n_tiles = (p0 + qb + tk - 1) // tk
    bcol = lax.broadcasted_iota(I32, (selmask.shape[1], tk), 0)
    kcol = lax.broadcasted_iota(I32, (selmask.shape[1], tk), 1)

    def sweep(jt, carry):
        k0 = pl.multiple_of(jt * tk, tk)
        expand = jnp.where(bcol == (k0 + kcol) // SEL_BLOCK, 1.0, 0.0).astype(BF16)
        selk = _dot(selmask, expand)
        selk = jnp.concatenate([selk] * Q_PER_KV, axis=0)
        d_s = qpos - (k0 + lax.broadcasted_iota(I32, (1, tk), 1))
        s = _dot_nt(q, ks_ref[0, 0, 0, pl.ds(k0, tk), :]) - slope * d_s.astype(F32)
        valid = (selk > 0.5) & (d_s >= 0)
        s = jnp.where(valid, s, NEG_INF)
        m_old = m_ref[...]
        m_new = jnp.maximum(m_old, jnp.max(s, axis=-1, keepdims=True))
        e = jnp.where(valid, jnp.exp(s - m_new), 0.0)
        alpha = jnp.exp(m_old - m_new)
        l_ref[...] = alpha * l_ref[...] + jnp.sum(e, axis=-1, keepdims=True)
        acc_ref[...] = alpha * acc_ref[...] + _dot(e.astype(BF16), vs_ref[0, 0, 0, pl.ds(k0, tk), :])
        m_ref[...] = m_new
        return carry

    lax.fori_loop(0, n_tiles, sweep, 0)
    l_s = l_ref[...]
    o_s = acc_ref[...] / jnp.where(l_s > 0.0, l_s, 1.0)

    w0 = pl.multiple_of(jnp.minimum(jnp.maximum(p0 - WINDOW, 0), seq - wlen), qb)
    d_w = qpos - (w0 + lax.broadcasted_iota(I32, (1, wlen), 1))
    s_w = _dot_nt(q, kw_ref[0, 0, 0, pl.ds(w0, wlen), :]) - slope * d_w.astype(F32)
    p_w = _masked_softmax(s_w, (d_w >= 0) & (d_w < WINDOW))
    o_w = _dot(p_w.astype(BF16), vw_ref[0, 0, 0, pl.ds(w0, wlen), :])

    for h in range(Q_PER_KV):
        r = slice(h * qb, (h + 1) * qb)
        g0 = gate_ref[0, 0, :, 3 * h + 0:3 * h + 1]
        g1 = gate_ref[0, 0, :, 3 * h + 1:3 * h + 2]
        g2 = gate_ref[0, 0, :, 3 * h + 2:3 * h + 3]
        o_ref[0, h] = (g0 * o_c[r] + g1 * o_s[r] + g2 * o_w[r]).astype(o_ref.dtype)


def _attn_prompt(q_hm, cmp_kv, kvh, gates, slope_rows, cover):
    b, _, seq, _ = q_hm.shape
    n_chunk = cmp_kv.shape[3]
    n_cmp = (seq - CMP_LEN) // CMP_STRIDE + 1
    n_sel = -(-seq // SEL_BLOCK)
    n_top = min(SEL_TOPK, n_sel)
    wlen = min(WINDOW + Q_BLOCK, seq)
    nrow = Q_PER_KV * Q_BLOCK
    assert seq % SEL_KV_TILE == 0 and seq % Q_BLOCK == 0 and n_chunk >= n_cmp

    def plane(pidx):
        return pl.BlockSpec((1, 1, 1, seq, HEAD_DIM), lambda bi, gi, i: (pidx, bi, gi, 0, 0))

    def cmp_plane(pidx):
        return pl.BlockSpec((1, 1, 1, n_chunk, HEAD_DIM), lambda bi, gi, i: (pidx, bi, gi, 0, 0))

    return pl.pallas_call(
        functools.partial(_attn_prompt_kernel, seq=seq, n_cmp=n_cmp, n_sel=n_sel, n_top=n_top, wlen=wlen),
        grid=(b, KV_HEADS, seq // Q_BLOCK),
        in_specs=[pl.BlockSpec((1, Q_PER_KV, Q_BLOCK, HEAD_DIM), lambda bi, gi, i: (bi, gi, i, 0)),
                  cmp_plane(0), cmp_plane(1), plane(0), plane(1), plane(2), plane(3),
                  pl.BlockSpec((1, 1, Q_BLOCK, 3 * Q_PER_KV), lambda bi, gi, i: (bi, gi, i, 0)),
                  pl.BlockSpec((1, nrow, 1), lambda bi, gi, i: (gi, 0, 0)),
                  pl.BlockSpec(cover.shape, lambda bi, gi, i: (0, 0))],
        out_specs=pl.BlockSpec((1, Q_PER_KV, Q_BLOCK, HEAD_DIM), lambda bi, gi, i: (bi, gi, i, 0)),
        out_shape=jax.ShapeDtypeStruct(q_hm.shape, BF16),
        scratch_shapes=[pltpu.VMEM((nrow, 1), F32), pltpu.VMEM((nrow, 1), F32),
                        pltpu.VMEM((nrow, HEAD_DIM), F32)],
        compiler_params=_cparams("arbitrary", "arbitrary", "arbitrary"),
        name="attn_prompt",
    )(q_hm, cmp_kv, cmp_kv, kvh, kvh, kvh, kvh, gates, slope_rows, cover)


def _attn_sample_kernel(pt_ref, qbd_ref, kc_ref, vc_ref, cache_ref, new_ref, kw_ref, gate_ref, slope_ref,
                        qpos_ref, cover_ref, expand_ref, o_ref, kv_ref, sem,
                        *, n_pages, page, past, n_cmp, n_sel, n_top, n_win, win_pos0, gt):
    b = pl.program_id(0)
    width = KV_HEADS * HEAD_DIM
    nk = kv_ref.shape[0]
    n_new = new_ref.shape[1]

    def page_copy(j):
        pid = pt_ref[b * n_pages + j]
        return pltpu.make_async_copy(cache_ref.at[pid], kv_ref.at[pl.ds(j * page, page), :], sem)

    def issue(j, c):
        page_copy(j).start()
        return c

    def drain(j, c):
        page_copy(j).wait()
        return c

    lax.fori_loop(0, n_pages, issue, 0)
    kv_ref[past:past + n_new, :] = new_ref[0]
    kv_ref[past + n_new:, :] = jnp.zeros((nk - past - n_new, kv_ref.shape[1]), F32)

    q = qbd_ref[0]
    nrow = q.shape[0]
    slope = slope_ref[...]
    qpos = qpos_ref[...]
    lane_grp = lax.broadcasted_iota(I32, (nrow, width), 1) // HEAD_DIM
    row_grp = (lax.broadcasted_iota(I32, (nrow, width), 0) % gt) // (gt // KV_HEADS)
    diag = lane_grp == row_grp

    def own_group(full):
        kept = jnp.where(diag, full, 0.0)
        out = kept[:, 0:HEAD_DIM]
        for g in range(1, KV_HEADS):
            out = out + kept[:, g * HEAD_DIM:(g + 1) * HEAD_DIM]
        return out

    n_chunk = kc_ref.shape[1]
    ccol = lax.broadcasted_iota(I32, (1, n_chunk), 1)
    d_c = qpos - (ccol * CMP_STRIDE + (CMP_LEN - 1))
    s_c = _dot_nt(q, kc_ref[0].astype(BF16)) - slope * d_c.astype(F32)
    p_c = _masked_softmax(s_c, (d_c >= 0) & (ccol < n_cmp))
    o_c = own_group(_dot(p_c.astype(BF16), vc_ref[0].astype(BF16)))

    p_sum = p_c[0:gt]
    for h in range(1, Q_PER_KV):
        p_sum = p_sum + p_c[h * gt:(h + 1) * gt]
    imp = _split_dot(p_sum, cover_ref[...])
    qpos_g = qpos[0:gt]
    scol = lax.broadcasted_iota(I32, (1, imp.shape[1]), 1)
    cur = qpos_g // SEL_BLOCK
    forced = (scol == 0) | (scol == cur) | (scol == cur - 1)
    visible = scol * SEL_BLOCK <= qpos_g
    score = jnp.where(visible, imp + jnp.where(forced, FORCE_BONUS, 0.0), -1.0)
    score = jnp.where(scol < n_sel, score, -2.0)
    selmask = _top_k_mask(score, n_top).astype(BF16)

    wk = kw_ref.shape[1]
    wcol = lax.broadcasted_iota(I32, (1, wk), 1)
    d_w = qpos - (win_pos0 + wcol)
    kw = kw_ref[0]
    s_w = _dot_nt(q, kw[:, 0:width].astype(BF16)) - slope * d_w.astype(F32)
    p_w = _masked_softmax(s_w, (d_w >= 0) & (d_w < WINDOW) & (wcol < n_win))
    o_w = own_group(_dot(p_w.astype(BF16), kw[:, width:2 * width].astype(BF16)))

    lax.fori_loop(0, n_pages, drain, 0)
    selk = _dot(selmask, expand_ref[...])
    selk = jnp.concatenate([selk] * Q_PER_KV, axis=0)
    d_s = qpos - lax.broadcasted_iota(I32, (1, nk), 1)
    s_s = _dot_nt(q, kv_ref[:, 0:width].astype(BF16)) - slope * d_s.astype(F32)
    p_s = _masked_softmax(s_s, (selk > 0.5) & (d_s >= 0))
    o_s = own_group(_dot(p_s.astype(BF16), kv_ref[:, width:2 * width].astype(BF16)))

    gate = gate_ref[0]
    o_ref[0] = gate[:, 0:1] * o_c + gate[:, 1:2] * o_s + gate[:, 2:3] * o_w


def _attn_sample(page_table, qbd, kc_all, vc_all, cache_s, new_sel, kw_all, gates, slope_rows, qpos_rows,
                 cover, expand, *, past, n_cmp, n_sel, n_win, win_pos0, gt):
    nb, n_pages = page_table.shape
    page = cache_s.shape[1]
    nrow = qbd.shape[1]
    nk = expand.shape[1]
    n_top = min(SEL_TOPK, n_sel)

    def per_seq(shape):
        return pl.BlockSpec((1,) + tuple(shape[1:]), lambda bi, pt: (bi,) + (0,) * (len(shape) - 1))

    def whole(shape):
        return pl.BlockSpec(tuple(shape), lambda bi, pt: (0,) * len(shape))

    grid_spec = pltpu.PrefetchScalarGridSpec(
        num_scalar_prefetch=1,
        grid=(nb,),
        in_specs=[per_seq(qbd.shape), per_seq(kc_all.shape), per_seq(vc_all.shape),
                  pl.BlockSpec(memory_space=pl.ANY), per_seq(new_sel.shape), per_seq(kw_all.shape),
                  per_seq(gates.shape), whole(slope_rows.shape), whole(qpos_rows.shape),
                  whole(cover.shape), whole(expand.shape)],
        out_specs=pl.BlockSpec((1, nrow, HEAD_DIM), lambda bi, pt: (bi, 0, 0)),
        scratch_shapes=[pltpu.VMEM((nk, cache_s.shape[2]), F32), pltpu.SemaphoreType.DMA(())],
    )
    return pl.pallas_call(
        functools.partial(_attn_sample_kernel, n_pages=n_pages, page=page, past=past, n_cmp=n_cmp,
                          n_sel=n_sel, n_top=n_top, n_win=n_win, win_pos0=win_pos0, gt=gt),
        grid_spec=grid_spec,
        out_shape=jax.ShapeDtypeStruct((nb, nrow, HEAD_DIM), F32),
        compiler_params=_cparams("arbitrary"),
        name="attn_sample",
    )(page_table.reshape(-1), qbd, kc_all, vc_all, cache_s, new_sel, kw_all, gates, slope_rows, qpos_rows,
      cover, expand)


def _merge_kernel(yp_ref, ya_ref, wp_ref, wa_ref, gp_ref, ga_ref, o_ref, wpb_ref, wab_ref):
    @pl.when(pl.program_id(1) == 0)
    def _():
        wpb_ref[...] = wp_ref[...].astype(BF16)
        wab_ref[...] = wa_ref[...].astype(BF16)

    merged = gp_ref[...] * _dot(yp_ref[...], wpb_ref[...]) + ga_ref[...] * _dot(ya_ref[...], wab_ref[...])
    o_ref[...] = merged.astype(o_ref.dtype)


def _merge(y_pool, y_attn, w_up_pool, w_up_nsa, gm):
    m, kp = y_pool.shape
    ka = y_attn.shape[1]
    d = w_up_pool.shape[1]
    tn = MM_TILE_N
    nn = d // tn
    return pl.pallas_call(
        _merge_kernel,
        grid=(nn, m // MM_TILE_M),
        in_specs=[pl.BlockSpec((MM_TILE_M, kp), lambda n, i: (i, 0)),
                  pl.BlockSpec((MM_TILE_M, ka), lambda n, i: (i, 0)),
                  pl.BlockSpec((kp, tn), lambda n, i: (0, n)),
                  pl.BlockSpec((ka, tn), lambda n, i: (0, n)),
                  pl.BlockSpec((MM_TILE_M, tn), lambda n, i: (i, n)),
                  pl.BlockSpec((MM_TILE_M, tn), lambda n, i: (i, nn + n))],
        out_specs=pl.BlockSpec((MM_TILE_M, tn), lambda n, i: (i, n)),
        out_shape=jax.ShapeDtypeStruct((m, d), BF16),
        scratch_shapes=[pltpu.VMEM((kp, tn), BF16), pltpu.VMEM((ka, tn), BF16)],
        compiler_params=_cparams("arbitrary", "arbitrary"),
        name="merge",
    )(y_pool, y_attn, w_up_pool, w_up_nsa, gm, gm)


def _route(logits):
    lane = lax.broadcasted_iota(I32, logits.shape, 1)
    is_grp = lane < N_GROUPS
    p_grp = _masked_softmax(logits, is_grp)
    p_top = jnp.max(p_grp, axis=-1, keepdims=True)
    grp = jnp.min(jnp.where(is_grp & (p_grp == p_top), lane, LANES), axis=-1, keepdims=True)
    first = N_GROUPS + grp * EXPERTS_PER_GROUP
    in_grp = (lane >= first) & (lane < first + EXPERTS_PER_GROUP)
    v = _masked_softmax(logits, in_grp)
    v0 = jnp.max(jnp.where(in_grp, v, -1.0), axis=-1, keepdims=True)
    i0 = jnp.min(jnp.where(in_grp & (v == v0), lane, LANES), axis=-1, keepdims=True)
    rest = in_grp & (lane != i0)
    v1 = jnp.max(jnp.where(rest, v, -1.0), axis=-1, keepdims=True)
    i1 = jnp.min(jnp.where(rest & (v == v1), lane, LANES), axis=-1, keepdims=True)
    tot = v0 + v1
    out = jnp.where(lane == 0, (i0 - N_GROUPS).astype(F32), 0.0)
    out = jnp.where(lane == 1, (i1 - N_GROUPS).astype(F32), out)
    out = jnp.where(lane == 2, p_top * (v0 / tot), out)
    out = jnp.where(lane == 3, p_top * (v1 / tot), out)
    return out


def _outproj_kernel(mg_ref, x_ref, w_ref, mp_ref, ms_ref, g_ref, b_ref, wr_ref, br_ref,
                    x1_ref, u_ref, rt_ref, *, n_ptiles, alpha):
    is_s = pl.program_id(0) >= n_ptiles
    gate1 = _mod_row(is_s, mp_ref, ms_ref, 0)
    shift2 = _mod_row(is_s, mp_ref, ms_ref, 1)
    scale2 = _mod_row(is_s, mp_ref, ms_ref, 2)
    mix = _dot(mg_ref[...], w_ref[...])
    x1 = _layer_norm(alpha * x_ref[...] + gate1 * mix, g_ref[...], b_ref[...])
    x1_ref[...] = x1
    u = x1 * (1.0 + scale2) + shift2
    u_ref[...] = u
    wr = wr_ref[...]
    wr_hi = wr.astype(BF16)
    wr_lo = (wr - wr_hi.astype(F32)).astype(BF16)
    u_hi = u.astype(BF16)
    u_lo = (u - u_hi.astype(F32)).astype(BF16)
    logits = _dot(u_hi, wr_hi) + _dot(u_lo, wr_hi) + _dot(u_hi, wr_lo) + br_ref[...]
    rt_ref[...] = _route(logits)


def _outproj(rows, merged, x, w_out_bf, modp, mods, ln_g, ln_b, w_router, b_router, alpha):
    d = x.shape[1]
    row = pl.BlockSpec((ROW_TILE, d), lambda i: (i, 0))
    vec = pl.BlockSpec((1, d), lambda i: (0, 0))
    return pl.pallas_call(
        functools.partial(_outproj_kernel, n_ptiles=rows.n_ptiles, alpha=alpha),
        grid=(rows.n_rows // ROW_TILE,),
        in_specs=[row, row, pl.BlockSpec((d, d), lambda i: (0, 0))] + rows.mod_specs(d, 1)
        + [vec, vec, pl.BlockSpec((d, LANES), lambda i: (0, 0)), pl.BlockSpec((1, LANES), lambda i: (0, 0))],
        out_specs=[row, row, pl.BlockSpec((ROW_TILE, LANES), lambda i: (i, 0))],
        out_shape=[jax.ShapeDtypeStruct((rows.n_rows, d), F32),
                   jax.ShapeDtypeStruct((rows.n_rows, d), F32),
                   jax.ShapeDtypeStruct((rows.n_rows, LANES), F32)],
        compiler_params=_cparams("arbitrary"),
        name="outproj_ln_route",
    )(merged, x, w_out_bf, modp, mods, ln_g.reshape(1, d), ln_b.reshape(1, d), w_router, b_router)


def _moe_gather_kernel(tok_ref, used_ref, u_ref, o_ref, buf_ref, sem):
    t = pl.program_id(0)
    tm = buf_ref.shape[0]

    def row_copy(r, tok):
        return pltpu.make_async_copy(u_ref.at[pl.ds(tok, 1)], buf_ref.at[pl.ds(r, 1)], sem)

    @pl.when(t < used_ref[0])
    def _():
        def issue(r, c):
            row_copy(r, tok_ref[t * tm + r]).start()
            return c

        def drain(r, c):
            row_copy(r, 0).wait()
            return c

        lax.fori_loop(0, tm, issue, 0)
        lax.fori_loop(0, tm, drain, 0)
        o_ref[...] = buf_ref[...].astype(o_ref.dtype)

    @pl.when(t >= used_ref[0])
    def _():
        o_ref[...] = jnp.zeros(o_ref.shape, o_ref.dtype)


def _moe_gather(row_tok, n_used, u):
    r_pad = row_tok.shape[0]
    d = u.shape[1]
    tm = EXPERT_TILE
    grid_spec = pltpu.PrefetchScalarGridSpec(
        num_scalar_prefetch=2,
        grid=(r_pad // tm,),
        in_specs=[pl.BlockSpec(memory_space=pl.ANY)],
        out_specs=pl.BlockSpec((tm, d), lambda t, *_: (t, 0)),
        scratch_shapes=[pltpu.VMEM((tm, d), F32), pltpu.SemaphoreType.DMA(())],
    )
    return pl.pallas_call(
        _moe_gather_kernel,
        grid_spec=grid_spec,
        out_shape=jax.ShapeDtypeStruct((r_pad, d), BF16),
        compiler_params=_cparams("arbitrary"),
        name="moe_gather",
    )(row_tok, n_used, u)


def _moe_expert_kernel(te_ref, used_ref, x_ref, wg_ref, wu_ref, wd_ref, rw_ref, o_ref,
                       wgb_ref, wub_ref, wdb_ref):
    t = pl.program_id(0)
    fresh = jnp.logical_or(t == 0, te_ref[t] != te_ref[jnp.maximum(t - 1, 0)])

    @pl.when(fresh)
    def _():
        wgb_ref[...] = wg_ref[0].astype(BF16)
        wub_ref[...] = wu_ref[0].astype(BF16)
        wdb_ref[...] = wd_ref[0].astype(BF16)

    @pl.when(t < used_ref[0])
    def _():
        x = x_ref[...]
        h = jax.nn.silu(_dot(x, wgb_ref[...])) * _dot(x, wub_ref[...])
        o_ref[...] = _dot((h * rw_ref[...]).astype(BF16), wdb_ref[...])

    @pl.when(t >= used_ref[0])
    def _():
        o_ref[...] = jnp.zeros(o_ref.shape, o_ref.dtype)


def _moe_experts(tile_expert, n_used, xs, row_w, w_gate, w_up, w_down):
    r_pad, d = xs.shape
    f = w_gate.shape[2]
    tm = EXPERT_TILE
    grid_spec = pltpu.PrefetchScalarGridSpec(
        num_scalar_prefetch=2,
        grid=(r_pad // tm,),
        in_specs=[pl.BlockSpec((tm, d), lambda t, te, nu: (t, 0)),
                  pl.BlockSpec((1, d, f), lambda t, te, nu: (te[t], 0, 0)),
                  pl.BlockSpec((1, d, f), lambda t, te, nu: (te[t], 0, 0)),
                  pl.BlockSpec((1, f, d), lambda t, te, nu: (te[t], 0, 0)),
                  pl.BlockSpec((tm, 1), lambda t, te, nu: (t, 0))],
        out_specs=pl.BlockSpec((tm, d), lambda t, te, nu: (t, 0)),
        scratch_shapes=[pltpu.VMEM((d, f), BF16), pltpu.VMEM((d, f), BF16), pltpu.VMEM((f, d), BF16)],
    )
    return pl.pallas_call(
        _moe_expert_kernel,
        grid_spec=grid_spec,
        out_shape=jax.ShapeDtypeStruct((r_pad, d), F32),
        compiler_params=_cparams("arbitrary"),
        name="moe_experts",
    )(tile_expert, n_used, xs, w_gate, w_up, w_down, row_w)


def _moe_combine_kernel(pos_ref, y_ref, x1_ref, mp_ref, ms_ref, g_ref, b_ref, x2_ref, *rest,
                        n_ptiles, alpha, emit_next):
    if emit_next:
        u_ref, buf_ref, sem = rest
    else:
        buf_ref, sem = rest
    i = pl.program_id(0)
    tm = x1_ref.shape[0]

    def row_copy(r, k, src):
        return pltpu.make_async_copy(y_ref.at[pl.ds(src, 1)], buf_ref.at[k, pl.ds(r, 1)], sem)

    def issue(r, c):
        base = 2 * (i * tm + r)
        row_copy(r, 0, pos_ref[base]).start()
        row_copy(r, 1, pos_ref[base + 1]).start()
        return c

    def drain(r, c):
        row_copy(r, 0, 0).wait()
        row_copy(r, 1, 0).wait()
        return c

    lax.fori_loop(0, tm, issue, 0)
    lax.fori_loop(0, tm, drain, 0)
    is_s = i >= n_ptiles
    gate2 = _mod_row(is_s, mp_ref, ms_ref, 0)
    ffn = buf_ref[0] + buf_ref[1]
    x2 = _layer_norm(alpha * x1_ref[...] + gate2 * ffn, g_ref[...], b_ref[...])
    x2_ref[...] = x2
    if emit_next:
        shift = _mod_row(is_s, mp_ref, ms_ref, 1)
        scale = _mod_row(is_s, mp_ref, ms_ref, 2)
        u_ref[...] = (x2 * (1.0 + scale) + shift).astype(u_ref.dtype)


def _moe_combine(rows, pos, y_rows, x1, modp, mods, ln_g, ln_b, alpha, emit_next):
    d = x1.shape[1]
    row = pl.BlockSpec((ROW_TILE, d), lambda i, *_: (i, 0))
    vec = pl.BlockSpec((1, d), lambda i, *_: (0, 0))
    out_specs = [row]
    out_shape = [jax.ShapeDtypeStruct((rows.n_rows, d), F32)]
    if emit_next:
        out_specs.append(row)
        out_shape.append(jax.ShapeDtypeStruct((rows.n_rows, d), BF16))
    grid_spec = pltpu.PrefetchScalarGridSpec(
        num_scalar_prefetch=1,
        grid=(rows.n_rows // ROW_TILE,),
        in_specs=[pl.BlockSpec(memory_space=pl.ANY), row] + rows.mod_specs(d, 2) + [vec, vec],
        out_specs=out_specs,
        scratch_shapes=[pltpu.VMEM((2, ROW_TILE, d), F32), pltpu.SemaphoreType.DMA(())],
    )
    out = pl.pallas_call(
        functools.partial(_moe_combine_kernel, n_ptiles=rows.n_ptiles, alpha=alpha, emit_next=emit_next),
        grid_spec=grid_spec,
        out_shape=out_shape,
        compiler_params=_cparams("arbitrary"),
        name="moe_combine_ln",
    )(pos, y_rows, x1, modp, mods, ln_g.reshape(1, d), ln_b.reshape(1, d))
    return out if emit_next else (out[0], None)


def _moe_dispatch(route, n_real, n_rows):
    tm = EXPERT_TILE
    n_pairs = 2 * n_real
    n_tiles = -(-(n_pairs + N_EXPERTS * (tm - 1)) // tm)
    r_pad = n_tiles * tm
    eid = route[:n_real, 0:2].astype(I32).reshape(-1)
    wts = route[:n_real, 2:4].reshape(-1)
    tok = jnp.arange(n_pairs, dtype=I32) // 2
    order = jnp.argsort(eid, stable=True)
    s_eid = eid[order]
    counts = jnp.zeros((N_EXPERTS,), I32).at[eid].add(1)
    padded = -(-counts // tm) * tm
    ends_pad = jnp.cumsum(padded)
    starts_pad = ends_pad - padded
    starts = jnp.cumsum(counts) - counts
    dest = starts_pad[s_eid] + (jnp.arange(n_pairs, dtype=I32) - starts[s_eid])
    row_tok = jnp.zeros((r_pad,), I32).at[dest].set(tok[order])
    row_w = jnp.zeros((r_pad,), F32).at[dest].set(wts[order]).reshape(r_pad, 1)
    pos = jnp.zeros((2 * n_rows,), I32).at[order].set(dest)
    n_used = (ends_pad[-1] // tm).astype(I32)
    tile_start = jnp.arange(n_tiles, dtype=I32) * tm
    tile_e = jnp.minimum(jnp.searchsorted(ends_pad, tile_start, side="right"), N_EXPERTS - 1).astype(I32)
    last_e = tile_e[jnp.maximum(n_used - 1, 0)]
    tile_e = jnp.where(jnp.arange(n_tiles) < n_used, tile_e, last_e)
    return row_tok, row_w, pos, tile_e, n_used.reshape(1)


def _cover_matrix(n_chunk, n_cmp, n_sel, n_sel_pad):
    ci = jnp.arange(n_chunk)[:, None]
    sj = jnp.arange(n_sel_pad)[None, :]
    hit = ((ci * CMP_STRIDE < sj * SEL_BLOCK + SEL_BLOCK)
           & (ci * CMP_STRIDE + CMP_LEN - 1 >= sj * SEL_BLOCK)
           & (ci < n_cmp) & (sj < n_sel))
    return hit.astype(BF16)


def kernel(x_prompt, x_sample, c_prompt, c_sample, cache_kv, cache_win, state_pool, page_table, w_ada, b_ada,
           w_in, cmp_w1, cmp_pe, cmp_w2, pool_w, pool_scale, w_up_pool, w_up_nsa, w_out, ln1_g, ln1_b, w_rg,
           b_rg, w_re, b_re, w_gate, w_up, w_down, ln2_g, ln2_b):
    batch, seq, d = x_prompt.shape
    nb, ts, _ = x_sample.shape
    n_layers = w_ada.shape[0]
    n_pool, page = cache_kv.shape[1], cache_kv.shape[2]
    n_pages = page_table.shape[1]
    past = n_pages * page
    win_buf = cache_win.shape[2]
    pool_width = pool_w.shape[1] * pool_w.shape[2]
    q_width = N_HEADS * HEAD_DIM
    grp_width = KV_HEADS * HEAD_DIM
    kv_width = N_KV_PLANES * grp_width
    gn_width = 3 * N_HEADS
    alpha = (2 * n_layers) ** 0.25
    cmp_hid = cmp_w1.shape[-1]
    chunk_w = CMP_STRIDE * HEAD_DIM

    rows = _Rows(batch, seq, nb * ts)
    n_p, n_s, n_rows = rows.n_prompt, rows.n_sample, rows.n_rows
    n_real = n_p + n_s

    x = jnp.concatenate([x_prompt.reshape(n_p, d), x_sample.reshape(n_s, d),
                         jnp.zeros((rows.sample_pad - n_s, d), F32)], axis=0)

    n_seq = batch + nb
    c_all = jnp.concatenate([c_prompt, c_sample, jnp.zeros((-n_seq % SUBLANES, d), F32)], axis=0)
    mod = _adaln(c_all, w_ada, b_ada)[:, :n_seq].reshape(n_layers, n_seq, 6, d)
    nxt = jnp.concatenate([mod[1:], jnp.zeros_like(mod[:1])], axis=0)
    mod9 = jnp.stack([mod[:, :, 0], mod[:, :, 1], jnp.zeros_like(mod[:, :, 0]),
                      mod[:, :, 2], mod[:, :, 3], mod[:, :, 4],
                      mod[:, :, 5], nxt[:, :, 0], nxt[:, :, 1]], axis=2).reshape(n_layers, n_seq, 3, 3, d)
    modp_all = jnp.pad(mod9[:, :batch], ((0, 0), (0, 0), (0, 0), (0, SUBLANES - 3), (0, 0)))
    mods_all = jnp.repeat(mod9[:, batch:], ts, axis=1).transpose(0, 2, 3, 1, 4)
    mods_all = jnp.pad(mods_all, ((0, 0), (0, 0), (0, 0), (0, rows.sample_pad - n_s), (0, 0)))

    o1 = pool_width
    o2 = o1 + q_width
    o3 = o2 + kv_width
    o4 = o3 + gn_width
    gm_width = w_in.shape[2] - o4
    gn_pad = -gn_width % LANES
    w_in_r = jnp.concatenate([w_in[:, :, :o3], w_in[:, :, o4:], w_in[:, :, o3:o4],
                              jnp.zeros((n_layers, d, gn_pad), F32)], axis=2).astype(BF16)
    col_gm = o3
    col_gn = o3 + gm_width
    w_out_bf = w_out.astype(BF16)
    w_router = jnp.concatenate([w_rg, w_re, jnp.zeros((n_layers, d, LANES - N_GROUPS - N_EXPERTS), F32)], axis=2)
    b_router = jnp.concatenate([b_rg, b_re, jnp.zeros((n_layers, LANES - N_GROUPS - N_EXPERTS), F32)], axis=1)

    slopes = jnp.exp2(-8.0 * jnp.arange(1, N_HEADS + 1, dtype=F32) / N_HEADS).reshape(KV_HEADS, Q_PER_KV)
    slope_p = jnp.repeat(slopes, Q_BLOCK, axis=1).reshape(KV_HEADS, Q_PER_KV * Q_BLOCK, 1)
    nch_p = seq // CMP_STRIDE
    ncmp_p = (seq - CMP_LEN) // CMP_STRIDE + 1
    nsel_p = -(-seq // SEL_BLOCK)
    cover_p = _cover_matrix(nch_p, ncmp_p, nsel_p, -(-nsel_p // LANES) * LANES)

    tk_s = past + ts
    ncmp_s = (tk_s - CMP_LEN) // CMP_STRIDE + 1
    nch_s = past // CMP_STRIDE
    assert (ncmp_s + CMP_LEN // CMP_STRIDE - 1) * CMP_STRIDE <= past and nch_s >= ncmp_s
    assert page % CMP_STRIDE == 0 and page % SEL_BLOCK == 0
    nsel_s = -(-tk_s // SEL_BLOCK)
    nsel_s_pad = -(-nsel_s // LANES) * LANES
    cover_s = _cover_matrix(nch_s, ncmp_s, nsel_s, nsel_s_pad)
    n_new = -(-ts // SUBLANES) * SUBLANES
    nk_s = -(-(past + n_new) // LANES) * LANES
    expand_s = (jnp.arange(nsel_s_pad)[:, None] == jnp.arange(nk_s)[None, :] // SEL_BLOCK).astype(BF16)
    gt = KV_HEADS * ts
    n_win = win_buf + ts
    n_win_pad = -(-n_win // SUBLANES) * SUBLANES
    slope_s = jnp.broadcast_to(slopes.T[:, :, None], (Q_PER_KV, KV_HEADS, ts)).reshape(Q_PER_KV * gt, 1)
    qpos_s = jnp.broadcast_to(past + jnp.arange(ts, dtype=I32), (Q_PER_KV, KV_HEADS, ts)).reshape(Q_PER_KV * gt, 1)
    eye_g = jnp.eye(KV_HEADS, dtype=BF16)

    u = _modulate(rows, x, modp_all[0], mods_all[0])
    outs = {k: [] for k in ("kv_p", "win_p", "pool_p", "kv_s", "win_s", "pool_s")}
    for l in range(n_layers):
        w_l = w_in_r[l]
        up = _mm(u, w_l, 0, pool_width, MM_TILE_N, F32)
        q = _mm(u, w_l, o1, q_width, MM_TILE_N, BF16, scale=HEAD_DIM ** -0.5)
        kv = _mm(u, w_l, o2, kv_width, MM_TILE_N, F32)
        gm = _mm(u, w_l, col_gm, gm_width, MM_TILE_N, F32, act="sigmoid")
        gn = _mm(u, w_l, col_gn, LANES, LANES, F32, act="sigmoid")

        kv_p = kv[:n_p].reshape(batch, seq, N_KV_PLANES, KV_HEADS, HEAD_DIM)
        kv_s = kv[n_p:n_real].reshape(nb, ts, N_KV_PLANES, KV_HEADS, HEAD_DIM)
        up_p = up[:n_p].reshape(batch, seq, pool_width)
        up_s = up[n_p:n_real].reshape(nb, ts, pool_width)
        outs["kv_p"].append(kv_p[:, :, :N_CACHED_PLANES])
        outs["win_p"].append(kv_p[:, seq - min(WINDOW, seq):, N_CACHED_PLANES:])
        outs["pool_p"].append(jnp.concatenate([jnp.zeros((batch, POOL_HIST, pool_width), F32), up_p],
                                              axis=1)[:, -POOL_HIST:])
        win_all = jnp.concatenate([cache_win[l], kv_s[:, :, N_CACHED_PLANES:]], axis=1)
        outs["kv_s"].append(kv_s[:, :, :N_CACHED_PLANES])
        outs["win_s"].append(win_all[:, n_win - win_buf:])
        pool_ext = jnp.concatenate([state_pool[l], up_s], axis=1)
        outs["pool_s"].append(pool_ext[:, -POOL_HIST:])

        yp_p = _pool_prompt(up, batch, seq, pool_w[l], pool_scale[l])
        yp_s = _pool_sample(pool_ext.transpose(1, 0, 2), ts, past, pool_w[l], pool_scale[l])
        y_pool = jnp.concatenate([yp_p, yp_s.transpose(1, 0, 2).reshape(n_s, pool_width),
                                  jnp.zeros((rows.sample_pad - n_s, pool_width), BF16)], axis=0)

        w1 = cmp_w1[l].reshape(2, CMP_LEN // CMP_STRIDE, chunk_w, cmp_hid)
        pe = cmp_pe[l].reshape(2, CMP_LEN // CMP_STRIDE, 1, chunk_w)
        w2 = cmp_w2[l]
        chunks_p = kv_p[:, :, 0:2].reshape(batch, nch_p, CMP_STRIDE, 2, KV_HEADS, HEAD_DIM)
        chunks_p = chunks_p.transpose(3, 0, 4, 1, 2, 5).reshape(2, batch, KV_HEADS, nch_p, chunk_w)
        cmp_p = _compress_prompt(chunks_p, w1, pe, w2)
        cache_c = cache_kv[l][:, :, 0:2].transpose(2, 3, 0, 1, 4)
        cache_c = cache_c.reshape(2, KV_HEADS, n_pool, page // CMP_STRIDE, chunk_w)
        cmp_s = _compress_sample(page_table, cache_c, w1, pe, w2)

        q_hm = q[:n_p].reshape(batch, seq, N_HEADS, HEAD_DIM).transpose(0, 2, 1, 3)
        kvh = kv_p[:, :, 2:].transpose(2, 0, 3, 1, 4).astype(BF16)
        gates_p = gn[:n_p, :gn_width].reshape(batch, seq, KV_HEADS, 3 * Q_PER_KV).transpose(0, 2, 1, 3)
        oa_p = _attn_prompt(q_hm, cmp_p, kvh, gates_p, slope_p, cover_p)
        ya_p = oa_p.transpose(0, 2, 1, 3).reshape(n_p, q_width)

        q_s = q[n_p:n_real].reshape(nb, ts, KV_HEADS, Q_PER_KV, HEAD_DIM).transpose(0, 3, 2, 1, 4)
        qbd = (q_s[:, :, :, :, None, :] * eye_g[None, None, :, None, :, None]).reshape(nb, Q_PER_KV * gt, grp_width)
        kvc_all = cmp_s.transpose(0, 1, 3, 2, 4).reshape(2, nb, nch_s, grp_width)
        cache_s = cache_kv[l][:, :, 2:4].reshape(n_pool, page, 2 * grp_width)
        new_sel = jnp.pad(kv_s[:, :, 2:4].reshape(nb, ts, 2 * grp_width), ((0, 0), (0, n_new - ts), (0, 0)))
        kw_all = jnp.pad(win_all.reshape(nb, n_win, 2 * grp_width), ((0, 0), (0, n_win_pad - n_win), (0, 0)))
        gates_s = gn[n_p:n_real, :gn_width].reshape(nb, ts, KV_HEADS, Q_PER_KV, 3).transpose(0, 3, 2, 1, 4)
        gates_s = gates_s.reshape(nb, Q_PER_KV * gt, 3)
        oa_s = _attn_sample(page_table, qbd, kvc_all[0], kvc_all[1], cache_s, new_sel, kw_all, gates_s,
                            slope_s, qpos_s, cover_s, expand_s, past=past, n_cmp=ncmp_s, n_sel=nsel_s,
                            n_win=n_win, win_pos0=past - win_buf, gt=gt)
        ya_s = oa_s.reshape(nb, Q_PER_KV, KV_HEADS, ts, HEAD_DIM).transpose(0, 3, 2, 1, 4).reshape(n_s, q_width)
        y_attn = jnp.concatenate([ya_p, ya_s.astype(BF16),
                                  jnp.zeros((rows.sample_pad - n_s, q_width), BF16)], axis=0)

        merged = _merge(y_pool, y_attn, w_up_pool[l], w_up_nsa[l], gm)
        x1, u2, route = _outproj(rows, merged, x, w_out_bf[l], modp_all[l], mods_all[l], ln1_g[l], ln1_b[l],
                                 w_router[l], b_router[l].reshape(1, LANES), alpha)

        row_tok, row_w, pos, tile_e, n_used = _moe_dispatch(route, n_real, n_rows)
        xs = _moe_gather(row_tok, n_used, u2)
        y_rows = _moe_experts(tile_e, n_used, xs, row_w, w_gate[l], w_up[l], w_down[l])
        x, u = _moe_combine(rows, pos, y_rows, x1, modp_all[l], mods_all[l], ln2_g[l], ln2_b[l], alpha,
                            emit_next=l + 1 < n_layers)

    y_prompt = x[:n_p].reshape(batch, seq, d)
    y_sample = x[n_p:n_real].reshape(nb, ts, d)
    return (y_prompt, y_sample, jnp.stack(outs["kv_p"]), jnp.stack(outs["win_p"]), jnp.stack(outs["pool_p"]),
            jnp.stack(outs["kv_s"]), jnp.stack(outs["win_s"]), jnp.stack(outs["pool_s"]))
```

```python
import functools

import jax
import jax.numpy as jnp
from jax import lax
from jax.experimental import pallas as pl
from jax.experimental.pallas import tpu as pltpu

F32 = jnp.float32
BF16 = jnp.bfloat16
I32 = jnp.int32

POOL_WINDOWS = (2, 4, 8, 16)
POOL_HIST = max(POOL_WINDOWS) - 1
N_HEADS = 16
KV_HEADS = 4
HEAD_DIM = 64
Q_PER_KV = N_HEADS // KV_HEADS
GRP_WIDTH = KV_HEADS * HEAD_DIM
CMP_LEN = 32
CMP_STRIDE = 16
CMP_SEGS = CMP_LEN // CMP_STRIDE
SEL_BLOCK = 64
SEL_TOPK = 16
WINDOW = 512
Q_BLOCK = 128
N_KV_PLANES = 6
N_CACHED_PLANES = 4
FORCE_BONUS = 1.0e4
N_GROUPS = 4
EXPERTS_PER_GROUP = 4
N_EXPERTS = N_GROUPS * EXPERTS_PER_GROUP
LN_EPS = 1e-5
NEG_INF = -1e30
NEG_CLAMP = -1e29

LANES = 128
SUBLANES = 8
VMEM_LIMIT = 52 * 1024 * 1024

ROW_TILE = 256
MM_TILE_M = 512
MM_TILE_N = 512
SEL_KV_TILE = 512
EXPERT_TILE = 128
CMP_K_GROUP = 4


def _cparams(*sem):
    return pltpu.CompilerParams(dimension_semantics=sem, vmem_limit_bytes=VMEM_LIMIT)


def _dot(a, b):
    return jnp.dot(a, b, preferred_element_type=F32)


def _dot_nt(a, b):
    return lax.dot_general(a, b, (((1,), (1,)), ((), ())), preferred_element_type=F32)


def _pv_t(v_t, p):
    return _dot_nt(v_t, p).T


def _split_bf16(a):
    hi = a.astype(BF16)
    return hi, (a - hi.astype(F32)).astype(BF16)


def _masked_softmax(s, valid):
    s = jnp.where(valid, s, NEG_INF)
    m = jnp.max(s, axis=-1, keepdims=True)
    e = jnp.where(valid, jnp.exp(s - m), 0.0)
    l = jnp.sum(e, axis=-1, keepdims=True)
    return e / jnp.where(l > 0.0, l, 1.0)


def _softmax_addmask(z):
    m = jnp.maximum(jnp.max(z, axis=-1, keepdims=True), NEG_CLAMP)
    e = jnp.exp(z - m)
    l = jnp.sum(e, axis=-1, keepdims=True)
    return e, jnp.where(l > 0.0, l, 1.0)


def _layer_norm(x, g, b):
    mu = jnp.mean(x, axis=-1, keepdims=True)
    xc = x - mu
    var = jnp.mean(xc * xc, axis=-1, keepdims=True)
    return xc * lax.rsqrt(var + LN_EPS) * g + b


def _top_k_mask(score, n_top):
    col = lax.broadcasted_iota(I32, score.shape, 1)
    big = jnp.int32(score.shape[1])

    def body(_, carry):
        sc, sel = carry
        m = jnp.max(sc, axis=-1, keepdims=True)
        idx = jnp.min(jnp.where(sc == m, col, big), axis=-1, keepdims=True)
        hit = col == idx
        return jnp.where(hit, -jnp.inf, sc), jnp.where(hit, 1.0, sel)

    _, sel = lax.fori_loop(0, n_top, body, (score, jnp.zeros(score.shape, F32)))
    return sel


def _block_scores(imp, blk, qpos, n_sel):
    cur = qpos // SEL_BLOCK
    forced = (blk == 0) | (blk == cur) | (blk == cur - 1)
    visible = blk * SEL_BLOCK <= qpos
    score = jnp.where(visible, imp + jnp.where(forced, FORCE_BONUS, 0.0), -1.0)
    return jnp.where(blk < n_sel, score, -2.0)


def _adaln_kernel(c_ref, w_ref, b_ref, o_ref):
    c = c_ref[...]
    a = (c * jax.nn.sigmoid(c)).astype(BF16)
    o_ref[0] = _dot(a, w_ref[0].astype(BF16)) + b_ref[0]


def _adaln(c_all, w_ada, b_ada):
    n_layers, d, n6 = w_ada.shape
    mp = c_all.shape[0]
    tn = 1024
    assert n6 % tn == 0
    return pl.pallas_call(
        _adaln_kernel,
        grid=(n_layers, n6 // tn),
        in_specs=[pl.BlockSpec((mp, d), lambda l, n: (0, 0)),
                  pl.BlockSpec((1, d, tn), lambda l, n: (l, 0, n)),
                  pl.BlockSpec((1, 1, tn), lambda l, n: (l, 0, n))],
        out_specs=pl.BlockSpec((1, mp, tn), lambda l, n: (l, 0, n)),
        out_shape=jax.ShapeDtypeStruct((n_layers, mp, n6), F32),
        compiler_params=_cparams("arbitrary", "arbitrary"),
        name="adaln",
    )(c_all, w_ada, b_ada.reshape(n_layers, 1, n6))


class _Rows:
    def __init__(self, batch, seq, n_sample):
        self.batch = batch
        self.seq = seq
        self.n_prompt = batch * seq
        self.n_sample = n_sample
        self.sample_pad = -(-n_sample // MM_TILE_M) * MM_TILE_M
        self.n_rows = self.n_prompt + self.sample_pad
        assert seq % MM_TILE_M == 0 and MM_TILE_M % ROW_TILE == 0
        self.n_ptiles = self.n_prompt // ROW_TILE

    def mod_specs(self, d, group):
        tiles_per_seq = self.seq // ROW_TILE
        last = self.batch - 1
        n_pt = self.n_ptiles
        return [pl.BlockSpec((1, 1, SUBLANES, d),
                             lambda i, *_: (jnp.minimum(i // tiles_per_seq, last), group, 0, 0)),
                pl.BlockSpec((1, 3, ROW_TILE, d),
                             lambda i, *_: (group, 0, jnp.maximum(i - n_pt, 0), 0))]


def _mod_row(is_sample, mp_ref, ms_ref, k):
    return jnp.where(is_sample, ms_ref[0, k], mp_ref[0, 0, k:k + 1, :])


def _modulate_kernel(x_ref, mp_ref, ms_ref, u_ref, *, n_ptiles):
    is_s = pl.program_id(0) >= n_ptiles
    shift = _mod_row(is_s, mp_ref, ms_ref, 0)
    scale = _mod_row(is_s, mp_ref, ms_ref, 1)
    u_ref[...] = (x_ref[...] * (1.0 + scale) + shift).astype(BF16)


def _modulate(rows, x, modp, mods):
    d = x.shape[1]
    return pl.pallas_call(
        functools.partial(_modulate_kernel, n_ptiles=rows.n_ptiles),
        grid=(rows.n_rows // ROW_TILE,),
        in_specs=[pl.BlockSpec((ROW_TILE, d), lambda i: (i, 0))] + rows.mod_specs(d, 0),
        out_specs=pl.BlockSpec((ROW_TILE, d), lambda i: (i, 0)),
        out_shape=jax.ShapeDtypeStruct((rows.n_rows, d), BF16),
        compiler_params=_cparams("arbitrary"),
        name="modulate",
    )(x, modp, mods)


def _proj_rows_kernel(x_ref, w_ref, o_ref, wb_ref, *, act, scale):
    @pl.when(pl.program_id(1) == 0)
    def _():
        wb_ref[...] = w_ref[0].astype(BF16)

    y = _dot_nt(x_ref[...], wb_ref[...])
    if act == "sigmoid":
        y = jax.nn.sigmoid(y)
    if scale != 1.0:
        y = y * scale
    o_ref[...] = y.astype(o_ref.dtype)


def _proj_rows(x, w_t, layer, row0, ncols, tn, out_dtype, act=None, scale=1.0, m0=0, m_rows=None):
    k = x.shape[1]
    m_rows = x.shape[0] - m0 if m_rows is None else m_rows
    assert m0 % MM_TILE_M == 0 and m_rows % MM_TILE_M == 0 and row0 % tn == 0 and ncols % tn == 0
    mb, rb = m0 // MM_TILE_M, row0 // tn
    return pl.pallas_call(
        functools.partial(_proj_rows_kernel, act=act, scale=scale),
        grid=(ncols // tn, m_rows // MM_TILE_M),
        in_specs=[pl.BlockSpec((MM_TILE_M, k), lambda n, i: (mb + i, 0)),
                  pl.BlockSpec((1, tn, k), lambda n, i: (layer, rb + n, 0))],
        out_specs=pl.BlockSpec((MM_TILE_M, tn), lambda n, i: (i, n)),
        out_shape=jax.ShapeDtypeStruct((m_rows, ncols), out_dtype),
        scratch_shapes=[pltpu.VMEM((tn, k), BF16)],
        compiler_params=_cparams("arbitrary", "arbitrary"),
        name="proj_rows",
    )(x, w_t)


def _proj_t_kernel(w_ref, x_ref, o_ref, wb_ref):
    @pl.when((pl.program_id(1) == 0) & (pl.program_id(2) == 0))
    def _():
        wb_ref[...] = w_ref[0].astype(BF16)

    o_ref[0, 0] = _dot_nt(wb_ref[...], x_ref[...])


def _proj_t(x, w_t, layer, row0, n_planes, batch, seq):
    k = x.shape[1]
    tt = MM_TILE_M
    assert row0 % GRP_WIDTH == 0 and seq % tt == 0
    rb = row0 // GRP_WIDTH
    nt = seq // tt
    return pl.pallas_call(
        _proj_t_kernel,
        grid=(n_planes, batch, nt),
        in_specs=[pl.BlockSpec((1, GRP_WIDTH, k), lambda p, b, i: (layer, rb + p, 0)),
                  pl.BlockSpec((tt, k), lambda p, b, i: (b * nt + i, 0))],
        out_specs=pl.BlockSpec((1, 1, GRP_WIDTH, tt), lambda p, b, i: (b, p, 0, i)),
        out_shape=jax.ShapeDtypeStruct((batch, n_planes, GRP_WIDTH, seq), F32),
        scratch_shapes=[pltpu.VMEM((GRP_WIDTH, k), BF16)],
        compiler_params=_cparams("arbitrary", "arbitrary", "arbitrary"),
        name="proj_t",
    )(w_t, x)


def _pool_prompt_kernel(cur_ref, prev_ref, pw_ref, sc_ref, o_ref, ext_ref, *, tt, pg):
    i = pl.program_id(1)
    halo = 2 * SUBLANES
    ext_ref[0:halo, :] = jnp.where(i == 0, 0.0, prev_ref[...])
    ext_ref[halo:, :] = cur_ref[...]
    pos = i * tt + lax.broadcasted_iota(I32, (tt, 1), 0)
    for g, w in enumerate(POOL_WINDOWS):
        cols = slice(g * pg, (g + 1) * pg)
        cur = ext_ref[halo:halo + tt, cols]
        acc = cur
        for j in range(1, w):
            acc = acc + ext_ref[halo - j:halo - j + tt, cols]
        count = jnp.minimum(pos + 1, w).astype(F32)
        dlt = acc / count - cur
        y = _dot(dlt.astype(BF16), pw_ref[g].astype(BF16)) * sc_ref[:, cols]
        o_ref[:, cols] = y.astype(o_ref.dtype)


def _pool_prompt(up, batch, seq, pool_w, pool_scale):
    pw = up.shape[1]
    pg = pw // len(POOL_WINDOWS)
    tt = MM_TILE_M
    halo = 2 * SUBLANES
    assert POOL_HIST <= halo and seq % tt == 0
    nt = seq // tt
    return pl.pallas_call(
        functools.partial(_pool_prompt_kernel, tt=tt, pg=pg),
        grid=(batch, nt),
        in_specs=[pl.BlockSpec((tt, pw), lambda b, i: (b * nt + i, 0)),
                  pl.BlockSpec((halo, pw), lambda b, i: (jnp.maximum((b * nt + i) * (tt // halo) - 1, 0), 0)),
                  pl.BlockSpec((len(POOL_WINDOWS), pg, pg), lambda b, i: (0, 0, 0)),
                  pl.BlockSpec((1, pw), lambda b, i: (0, 0))],
        out_specs=pl.BlockSpec((tt, pw), lambda b, i: (b * nt + i, 0)),
        out_shape=jax.ShapeDtypeStruct((batch * seq, pw), BF16),
        scratch_shapes=[pltpu.VMEM((halo + tt, pw), F32)],
        compiler_params=_cparams("arbitrary", "arbitrary"),
        name="pool_prompt",
    )(up, up, pool_w, pool_scale.reshape(1, pw))


def _pool_sample_kernel(ext_ref, pw_ref, sc_ref, o_ref, *, ts, pg, pos0):
    for t in range(ts):
        for g, w in enumerate(POOL_WINDOWS):
            cols = slice(g * pg, (g + 1) * pg)
            cur = ext_ref[POOL_HIST + t, :, cols]
            acc = cur
            for j in range(1, w):
                acc = acc + ext_ref[POOL_HIST + t - j, :, cols]
            count = float(min(pos0 + t + 1, w))
            dlt = acc / count - cur
            y = _dot(dlt.astype(BF16), pw_ref[g].astype(BF16)) * sc_ref[:, cols]
            o_ref[t, :, cols] = y.astype(o_ref.dtype)


def _pool_sample(ext, ts, pos0, pool_w, pool_scale):
    n_ext, nb, pw = ext.shape
    pg = pw // len(POOL_WINDOWS)
    return pl.pallas_call(
        functools.partial(_pool_sample_kernel, ts=ts, pg=pg, pos0=pos0),
        grid=(1,),
        in_specs=[pl.BlockSpec((n_ext, nb, pw), lambda i: (0, 0, 0)),
                  pl.BlockSpec((len(POOL_WINDOWS), pg, pg), lambda i: (0, 0, 0)),
                  pl.BlockSpec((1, pw), lambda i: (0, 0))],
        out_specs=pl.BlockSpec((ts, nb, pw), lambda i: (0, 0, 0)),
        out_shape=jax.ShapeDtypeStruct((ts, nb, pw), BF16),
        compiler_params=_cparams("arbitrary"),
        name="pool_sample",
    )(ext, pool_w, pool_scale.reshape(1, pw))


def _compress_tail(ha, hb, w2t_ref):
    n_chunk = ha.shape[0]
    h = ha + pltpu.roll(hb, n_chunk - 1, 0)
    act = jax.nn.gelu(h).astype(BF16)
    w2t = w2t_ref[0].astype(BF16)
    return _dot_nt(act, w2t), _dot_nt(w2t, act)


def _compress_prompt_kernel(c_ref, w1_ref, pe_ref, w2t_ref, o_ref, ot_ref):
    c = c_ref[0, 0, 0]
    ha = _dot((c + pe_ref[0, 0]).astype(BF16), w1_ref[0, 0].astype(BF16))
    hb = _dot((c + pe_ref[0, 1]).astype(BF16), w1_ref[0, 1].astype(BF16))
    o_ref[0, 0, 0], ot_ref[0, 0, 0] = _compress_tail(ha, hb, w2t_ref)


def _compress_prompt(chunks, w1, pe, w2t):
    _, b, g, n_chunk, kdim = chunks.shape
    hid = w1.shape[-1]
    return pl.pallas_call(
        _compress_prompt_kernel,
        grid=(2, b, g),
        in_specs=[pl.BlockSpec((1, 1, 1, n_chunk, kdim), lambda p, bi, gi: (p, bi, gi, 0, 0)),
                  pl.BlockSpec((1, CMP_SEGS, kdim, hid), lambda p, bi, gi: (p, 0, 0, 0)),
                  pl.BlockSpec((1, CMP_SEGS, 1, kdim), lambda p, bi, gi: (p, 0, 0, 0)),
                  pl.BlockSpec((1, HEAD_DIM, hid), lambda p, bi, gi: (p, 0, 0))],
        out_specs=[pl.BlockSpec((1, 1, 1, n_chunk, HEAD_DIM), lambda p, bi, gi: (p, bi, gi, 0, 0)),
                   pl.BlockSpec((1, 1, 1, HEAD_DIM, n_chunk), lambda p, bi, gi: (p, bi, gi, 0, 0))],
        out_shape=[jax.ShapeDtypeStruct((2, b, g, n_chunk, HEAD_DIM), F32),
                   jax.ShapeDtypeStruct((2, b, g, HEAD_DIM, n_chunk), F32)],
        compiler_params=_cparams("arbitrary", "arbitrary", "arbitrary"),
        name="compress_prompt",
    )(chunks, w1, pe, w2t)


def _compress_sample_kernel(pt_ref, cache_ref, w1_ref, pe_ref, w2t_ref, o_ref, buf_ref, rows_ref, w1b_ref, sem,
                            *, layer, n_pages, page):
    p = pl.program_id(0)
    b = pl.program_id(1)
    n_chunk = n_pages * page // CMP_STRIDE
    kg = CMP_K_GROUP

    def page_copy(j):
        pid = pt_ref[b * n_pages + j]
        return pltpu.make_async_copy(cache_ref.at[layer, pid, p], buf_ref.at[j], sem)

    def issue(j, c):
        page_copy(j).start()
        return c

    def drain(j, c):
        page_copy(j).wait()
        return c

    lax.fori_loop(0, n_pages, issue, 0)

    @pl.when(b == 0)
    def _():
        w1b_ref[...] = w1_ref[0].astype(BF16)

    lax.fori_loop(0, n_pages, drain, 0)
    for g in range(KV_HEADS):
        def to_rows(j, c):
            r0 = pl.multiple_of(j * page, page)
            rows_ref[pl.ds(r0, page), :] = buf_ref[j, g].T
            return c

        lax.fori_loop(0, n_pages, to_rows, 0, unroll=8)
        acc = []
        for m in range(CMP_SEGS):
            h = None
            for s0 in range(0, CMP_STRIDE, kg):
                piece = jnp.concatenate(
                    [rows_ref[pl.ds(s0 + k, n_chunk, stride=CMP_STRIDE), :] for k in range(kg)], axis=1)
                piece = (piece + pe_ref[0, m, s0 // kg]).astype(BF16)
                t = _dot(piece, w1b_ref[m, s0 // kg])
                h = t if h is None else h + t
            acc.append(h)
        o_ref[0, 0, g], _ = _compress_tail(acc[0], acc[1], w2t_ref)


def _compress_sample(page_table, cache_t, layer, w1, pe, w2t):
    nb, n_pages = page_table.shape
    page = cache_t.shape[-1]
    n_chunk = n_pages * page // CMP_STRIDE
    hid = w1.shape[-1]
    kg = CMP_K_GROUP
    w1g = w1.reshape(2, CMP_SEGS, CMP_STRIDE // kg, kg * HEAD_DIM, hid)
    peg = pe.reshape(2, CMP_SEGS, CMP_STRIDE // kg, 1, kg * HEAD_DIM)
    grid_spec = pltpu.PrefetchScalarGridSpec(
        num_scalar_prefetch=1,
        grid=(2, nb),
        in_specs=[pl.BlockSpec(memory_space=pl.ANY),
                  pl.BlockSpec((1,) + w1g.shape[1:], lambda p, bi, pt: (p, 0, 0, 0, 0)),
                  pl.BlockSpec((1,) + peg.shape[1:], lambda p, bi, pt: (p, 0, 0, 0, 0)),
                  pl.BlockSpec((1, HEAD_DIM, hid), lambda p, bi, pt: (p, 0, 0))],
        out_specs=pl.BlockSpec((1, 1, KV_HEADS, n_chunk, HEAD_DIM), lambda p, bi, pt: (p, bi, 0, 0, 0)),
        scratch_shapes=[pltpu.VMEM((n_pages, KV_HEADS, HEAD_DIM, page), F32),
                        pltpu.VMEM((n_pages * page, HEAD_DIM), F32),
                        pltpu.VMEM(w1g.shape[1:], BF16),
                        pltpu.SemaphoreType.DMA(())],
    )
    return pl.pallas_call(
        functools.partial(_compress_sample_kernel, layer=layer, n_pages=n_pages, page=page),
        grid_spec=grid_spec,
        out_shape=jax.ShapeDtypeStruct((2, nb, KV_HEADS, n_chunk, HEAD_DIM), F32),
        compiler_params=_cparams("arbitrary", "arbitrary"),
        name="compress_sample",
    )(page_table.reshape(-1), cache_t, w1g, peg, w2t)


def _attn_prompt_kernel(slope_ref, q_ref, kc_ref, vct_ref, ks_ref, vs_ref, kw_ref, vw_ref, gate_ref, covert_ref,
                        o_ref, m_ref, l_ref, acc_ref, st_ref, bias_ref, *, seq, n_cmp, n_sel, n_top, wlen):
    qb = Q_BLOCK
    tk = SEL_KV_TILE
    g = pl.program_id(1)
    p0 = pl.program_id(2) * qb
    tok = lax.broadcasted_iota(I32, (qb, 1), 0)
    qpos = p0 + tok
    slopes = [slope_ref[g * Q_PER_KV + h] for h in range(Q_PER_KV)]

    n_chunk = kc_ref.shape[3]
    ccol = lax.broadcasted_iota(I32, (1, n_chunk), 1)
    d_c = qpos - (ccol * CMP_STRIDE + (CMP_LEN - 1))
    mask_c = jnp.where((d_c >= 0) & (ccol < n_cmp), 0.0, NEG_INF)
    d_cf = d_c.astype(F32)
    kc = kc_ref[0, 0, 0].astype(BF16)
    vct = vct_ref[0, 0, 0].astype(BF16)
    p_sum = jnp.zeros((qb, n_chunk), F32)
    o_c = []
    for h in range(Q_PER_KV):
        e, l = _softmax_addmask(_dot_nt(q_ref[0, h], kc) - slopes[h] * d_cf + mask_c)
        p = e / l
        p_sum = p_sum + p
        o_c.append(_pv_t(vct, p.astype(BF16)))

    p_hi, p_lo = _split_bf16(p_sum)
    cover_t = covert_ref[...]
    imp_t = _dot_nt(cover_t, p_hi) + _dot_nt(cover_t, p_lo)
    blk = lax.broadcasted_iota(I32, (imp_t.shape[0], 1), 0)
    score = _block_scores(imp_t, blk, p0 + lax.broadcasted_iota(I32, (1, qb), 1), n_sel)
    st_ref[...] = score

    def count_beaten(k, cnt):
        row = st_ref[pl.ds(k, 1), :]
        return cnt + jnp.where((row > score) | ((row == score) & (k < blk)), 1.0, 0.0)

    rank = lax.fori_loop(0, n_sel, count_beaten, jnp.zeros(score.shape, F32))
    sel = jnp.where(rank < n_top, 1.0, 0.0).T.astype(BF16)

    kcol = lax.broadcasted_iota(I32, (1, tk), 1)
    rel = (kcol - tok).astype(F32)
    for h in range(Q_PER_KV):
        bias_ref[h] = slopes[h] * rel
    blk_diff = (lax.broadcasted_iota(I32, (sel.shape[1], tk), 0)
                - lax.broadcasted_iota(I32, (sel.shape[1], tk), 1) // SEL_BLOCK)
    m_ref[...] = jnp.full(m_ref.shape, NEG_INF, F32)
    l_ref[...] = jnp.zeros(l_ref.shape, F32)
    acc_ref[...] = jnp.zeros(acc_ref.shape, F32)
    n_tiles = (p0 + qb + tk - 1) // tk

    def sweep_tile(jt, causal):
        k0 = pl.multiple_of(jt * tk, tk)
        expand = jnp.where(blk_diff == k0 // SEL_BLOCK, 1.0, 0.0).astype(BF16)
        madd = (_dot(sel, expand) - 1.0) * (-NEG_INF)
        if causal:
            madd = jnp.where(k0 + kcol <= qpos, madd, NEG_INF)
        k_t = ks_ref[0, 0, :, pl.ds(k0, tk)].astype(BF16)
        v_t = vs_ref[0, 0, :, pl.ds(k0, tk)].astype(BF16)
        off = (k0 - p0).astype(F32)
        for h in range(Q_PER_KV):
            z = _dot(q_ref[0, h], k_t) + (bias_ref[h] + madd)
            shift = slopes[h] * off
            m_old = m_ref[h]
            m_new = jnp.maximum(m_old, jnp.max(z, axis=-1, keepdims=True) + shift)
            m_use = jnp.maximum(m_new, NEG_CLAMP)
            e = jnp.exp(z - (m_use - shift))
            alpha = jnp.exp(m_old - m_use)
            l_ref[h] = alpha * l_ref[h] + jnp.sum(e, axis=-1, keepdims=True)
            acc_ref[h] = alpha * acc_ref[h] + _pv_t(v_t, e.astype(BF16))
            m_ref[h] = m_new

    def sweep_body(jt, carry):
        sweep_tile(jt, False)
        return carry

    lax.fori_loop(0, n_tiles - 1, sweep_body, 0)
    sweep_tile(n_tiles - 1, True)

    w0 = pl.multiple_of(jnp.minimum(jnp.maximum(p0 - WINDOW, 0), seq - wlen), qb)
    relw = (w0 - p0) + lax.broadcasted_iota(I32, (1, wlen), 1) - tok
    mask_w = jnp.where((relw <= 0) & (relw > -WINDOW), 0.0, NEG_INF)
    relw_f = relw.astype(F32)
    kw_t = kw_ref[0, 0, :, pl.ds(w0, wlen)].astype(BF16)
    vw_t = vw_ref[0, 0, :, pl.ds(w0, wlen)].astype(BF16)

    for h in range(Q_PER_KV):
        e, l = _softmax_addmask(_dot(q_ref[0, h], kw_t) + slopes[h] * relw_f + mask_w)
        o_w = _pv_t(vw_t, e.astype(BF16)) / l
        l_s = l_ref[h]
        o_s = acc_ref[h] / jnp.where(l_s > 0.0, l_s, 1.0)
        g0 = gate_ref[0, 0, :, 3 * h + 0:3 * h + 1]
        g1 = gate_ref[0, 0, :, 3 * h + 1:3 * h + 2]
        g2 = gate_ref[0, 0, :, 3 * h + 2:3 * h + 3]
        o_ref[0, h] = (g0 * o_c[h] + g1 * o_s + g2 * o_w).astype(o_ref.dtype)


def _attn_prompt(slopes, q_hm, cmp_rows, cmp_t, kv_t, win_t, gates, cover_t):
    b, _, seq, _ = q_hm.shape
    n_chunk = cmp_rows.shape[3]
    n_cmp = (seq - CMP_LEN) // CMP_STRIDE + 1
    n_sel = -(-seq // SEL_BLOCK)
    n_top = min(SEL_TOPK, n_sel)
    wlen = min(WINDOW + Q_BLOCK, seq)
    nsp = cover_t.shape[0]
    assert seq % SEL_KV_TILE == 0 and seq % Q_BLOCK == 0 and n_chunk >= n_cmp and wlen % LANES == 0

    def plane(pidx):
        return pl.BlockSpec((1, 1, HEAD_DIM, seq), lambda bi, gi, i: (bi, pidx, gi, 0))

    return pl.pallas_call(
        functools.partial(_attn_prompt_kernel, seq=seq, n_cmp=n_cmp, n_sel=n_sel, n_top=n_top, wlen=wlen),
        grid=(b, KV_HEADS, seq // Q_BLOCK),
        in_specs=[pl.BlockSpec(memory_space=pltpu.SMEM),
                  pl.BlockSpec((1, Q_PER_KV, Q_BLOCK, HEAD_DIM), lambda bi, gi, i: (bi, gi, i, 0)),
                  pl.BlockSpec((1, 1, 1, n_chunk, HEAD_DIM), lambda bi, gi, i: (0, bi, gi, 0, 0)),
                  pl.BlockSpec((1, 1, 1, HEAD_DIM, n_chunk), lambda bi, gi, i: (1, bi, gi, 0, 0)),
                  plane(2), plane(3), plane(0), plane(1),
                  pl.BlockSpec((1, 1, Q_BLOCK, 3 * Q_PER_KV), lambda bi, gi, i: (bi, gi, i, 0)),
                  pl.BlockSpec(cover_t.shape, lambda bi, gi, i: (0, 0))],
        out_specs=pl.BlockSpec((1, Q_PER_KV, Q_BLOCK, HEAD_DIM), lambda bi, gi, i: (bi, gi, i, 0)),
        out_shape=jax.ShapeDtypeStruct(q_hm.shape, BF16),
        scratch_shapes=[pltpu.VMEM((Q_PER_KV, Q_BLOCK, 1), F32), pltpu.VMEM((Q_PER_KV, Q_BLOCK, 1), F32),
                        pltpu.VMEM((Q_PER_KV, Q_BLOCK, HEAD_DIM), F32), pltpu.VMEM((nsp, Q_BLOCK), F32),
                        pltpu.VMEM((Q_PER_KV, Q_BLOCK, SEL_KV_TILE), F32)],
        compiler_params=_cparams("arbitrary", "arbitrary", "arbitrary"),
        name="attn_prompt",
    )(slopes, q_hm, cmp_rows, cmp_t, kv_t, kv_t, win_t, win_t, gates, cover_t)


def _attn_sample_kernel(pt_ref, qbd_ref, kc_ref, vc_ref, cache_ref, new_ref, win_ref, gate_ref, slope_ref,
                        qpos_ref, cover_ref, expand_ref, o_ref, kt_ref, vt_ref, sem,
                        *, layer, n_pages, page, past, ts, n_cmp, n_sel, n_top, win_buf, gt):
    b = pl.program_id(0)
    nk = kt_ref.shape[2]

    def page_copies(j):
        pid = pt_ref[b * n_pages + j]
        dst = pl.ds(pl.multiple_of(j * page, page), page)
        return (pltpu.make_async_copy(cache_ref.at[layer, pid, 2], kt_ref.at[:, :, dst], sem),
                pltpu.make_async_copy(cache_ref.at[layer, pid, 3], vt_ref.at[:, :, dst], sem))

    def issue(j, c):
        for cp in page_copies(j):
            cp.start()
        return c

    def drain(j, c):
        for cp in page_copies(j):
            cp.wait()
        return c

    lax.fori_loop(0, n_pages, issue, 0)
    kt_ref[:, :, past:] = new_ref[0, 0].reshape(KV_HEADS, HEAD_DIM, nk - past)
    vt_ref[:, :, past:] = new_ref[0, 1].reshape(KV_HEADS, HEAD_DIM, nk - past)

    q = qbd_ref[0]
    nrow = q.shape[0]
    slope = slope_ref[...]
    qpos = qpos_ref[...]
    lane_grp = lax.broadcasted_iota(I32, (nrow, GRP_WIDTH), 1) // HEAD_DIM
    row_grp = (lax.broadcasted_iota(I32, (nrow, GRP_WIDTH), 0) % gt) // (gt // KV_HEADS)
    diag = lane_grp == row_grp

    def own_group(full):
        kept = jnp.where(diag, full, 0.0)
        out = kept[:, 0:HEAD_DIM]
        for g in range(1, KV_HEADS):
            out = out + kept[:, g * HEAD_DIM:(g + 1) * HEAD_DIM]
        return out

    n_chunk = kc_ref.shape[1]
    ccol = lax.broadcasted_iota(I32, (1, n_chunk), 1)
    d_c = qpos - (ccol * CMP_STRIDE + (CMP_LEN - 1))
    s_c = _dot_nt(q, kc_ref[0].astype(BF16)) - slope * d_c.astype(F32)
    p_c = _masked_softmax(s_c, (d_c >= 0) & (ccol < n_cmp))
    o_c = own_group(_dot(p_c.astype(BF16), vc_ref[0].astype(BF16)))

    p_sum = p_c[0:gt]
    for h in range(1, Q_PER_KV):
        p_sum = p_sum + p_c[h * gt:(h + 1) * gt]
    p_hi, p_lo = _split_bf16(p_sum)
    imp = _dot(p_hi, cover_ref[...]) + _dot(p_lo, cover_ref[...])
    scol = lax.broadcasted_iota(I32, (1, imp.shape[1]), 1)
    selmask = _top_k_mask(_block_scores(imp, scol, qpos[0:gt], n_sel), n_top).astype(BF16)

    n_newl = new_ref.shape[3]
    wcol = lax.broadcasted_iota(I32, (1, win_buf + n_newl), 1)
    kpos_w = jnp.where(wcol < win_buf, past - win_buf + wcol, past + wcol - win_buf)
    d_w = qpos - kpos_w
    valid_w = (d_w >= 0) & (d_w < WINDOW) & (wcol < win_buf + ts)
    kw_t = win_ref[0, 0, 0].reshape(GRP_WIDTH, win_buf).astype(BF16)
    vw_t = win_ref[0, 0, 1].reshape(GRP_WIDTH, win_buf).astype(BF16)
    s_w = jnp.concatenate([_dot(q, kw_t), _dot(q, new_ref[0, 2].astype(BF16))], axis=1)
    p_w = _masked_softmax(s_w - slope * d_w.astype(F32), valid_w).astype(BF16)
    o_w = own_group(_dot_nt(p_w[:, :win_buf], vw_t) + _dot_nt(p_w[:, win_buf:], new_ref[0, 3].astype(BF16)))

    lax.fori_loop(0, n_pages, drain, 0)
    selk = _dot(selmask, expand_ref[...])
    selk = jnp.concatenate([selk] * Q_PER_KV, axis=0)
    d_s = qpos - lax.broadcasted_iota(I32, (1, nk), 1)
    k_t = kt_ref[...].reshape(GRP_WIDTH, nk).astype(BF16)
    s_s = _dot(q, k_t) - slope * d_s.astype(F32)
    p_s = _masked_softmax(s_s, (selk > 0.5) & (d_s >= 0))
    v_t = vt_ref[...].reshape(GRP_WIDTH, nk).astype(BF16)
    o_s = own_group(_dot_nt(p_s.astype(BF16), v_t))

    gate = gate_ref[0]
    o_ref[0] = gate[:, 0:1] * o_c + gate[:, 1:2] * o_s + gate[:, 2:3] * o_w


def _attn_sample(page_table, qbd, kc_all, vc_all, cache_t, new_t, win_t, gates, slope_rows, qpos_rows,
                 cover, expand, *, layer, past, ts, n_cmp, n_sel, gt):
    nb, n_pages = page_table.shape
    page = cache_t.shape[-1]
    win_buf = win_t.shape[-1]
    nrow = qbd.shape[1]
    nk = expand.shape[1]
    n_top = min(SEL_TOPK, n_sel)
    assert nk == past + new_t.shape[3] and win_buf % LANES == 0

    def per_seq(shape):
        return pl.BlockSpec((1,) + tuple(shape[1:]), lambda bi, pt: (bi,) + (0,) * (len(shape) - 1))

    def whole(shape):
        return pl.BlockSpec(tuple(shape), lambda bi, pt: (0,) * len(shape))

    grid_spec = pltpu.PrefetchScalarGridSpec(
        num_scalar_prefetch=1,
        grid=(nb,),
        in_specs=[per_seq(qbd.shape), per_seq(kc_all.shape), per_seq(vc_all.shape),
                  pl.BlockSpec(memory_space=pl.ANY), per_seq(new_t.shape),
                  pl.BlockSpec((1, 1) + tuple(win_t.shape[2:]), lambda bi, pt: (layer, bi, 0, 0, 0, 0)),
                  per_seq(gates.shape), whole(slope_rows.shape), whole(qpos_rows.shape),
                  whole(cover.shape), whole(expand.shape)],
        out_specs=pl.BlockSpec((1, nrow, HEAD_DIM), lambda bi, pt: (bi, 0, 0)),
        scratch_shapes=[pltpu.VMEM((KV_HEADS, HEAD_DIM, nk), F32), pltpu.VMEM((KV_HEADS, HEAD_DIM, nk), F32),
                        pltpu.SemaphoreType.DMA(())],
    )
    return pl.pallas_call(
        functools.partial(_attn_sample_kernel, layer=layer, n_pages=n_pages, page=page, past=past, ts=ts,
                          n_cmp=n_cmp, n_sel=n_sel, n_top=n_top, win_buf=win_buf, gt=gt),
        grid_spec=grid_spec,
        out_shape=jax.ShapeDtypeStruct((nb, nrow, HEAD_DIM), F32),
        compiler_params=_cparams("arbitrary"),
        name="attn_sample",
    )(page_table.reshape(-1), qbd, kc_all, vc_all, cache_t, new_t, win_t, gates, slope_rows, qpos_rows,
      cover, expand)


def _merge_kernel(u_ref, yp_ref, ya_ref, wgp_ref, wga_ref, wp_ref, wa_ref, o_ref, wgpb_ref, wgab_ref, wpb_ref,
                  wab_ref):
    @pl.when(pl.program_id(1) == 0)
    def _():
        wgpb_ref[...] = wgp_ref[0].astype(BF16)
        wgab_ref[...] = wga_ref[0].astype(BF16)
        wpb_ref[...] = wp_ref[0].astype(BF16)
        wab_ref[...] = wa_ref[0].astype(BF16)

    u = u_ref[...]
    g_pool = jax.nn.sigmoid(_dot_nt(u, wgpb_ref[...]))
    g_attn = jax.nn.sigmoid(_dot_nt(u, wgab_ref[...]))
    merged = g_pool * _dot(yp_ref[...], wpb_ref[...]) + g_attn * _dot(ya_ref[...], wab_ref[...])
    o_ref[...] = merged.astype(o_ref.dtype)


def _merge(u, y_pool, y_attn, w_gm_t, w_up_pool, w_up_nsa, layer):
    m, d = u.shape
    kp = y_pool.shape[1]
    ka = y_attn.shape[1]
    tn = MM_TILE_N
    nn = d // tn
    row = lambda w: pl.BlockSpec((MM_TILE_M, w), lambda n, i: (i, 0))
    return pl.pallas_call(
        _merge_kernel,
        grid=(nn, m // MM_TILE_M),
        in_specs=[row(d), row(kp), row(ka),
                  pl.BlockSpec((1, tn, d), lambda n, i: (layer, n, 0)),
                  pl.BlockSpec((1, tn, d), lambda n, i: (layer, nn + n, 0)),
                  pl.BlockSpec((1, kp, tn), lambda n, i: (layer, 0, n)),
                  pl.BlockSpec((1, ka, tn), lambda n, i: (layer, 0, n))],
        out_specs=pl.BlockSpec((MM_TILE_M, tn), lambda n, i: (i, n)),
        out_shape=jax.ShapeDtypeStruct((m, d), BF16),
        scratch_shapes=[pltpu.VMEM((tn, d), BF16), pltpu.VMEM((tn, d), BF16),
                        pltpu.VMEM((kp, tn), BF16), pltpu.VMEM((ka, tn), BF16)],
        compiler_params=_cparams("arbitrary", "arbitrary"),
        name="merge",
    )(u, y_pool, y_attn, w_gm_t, w_gm_t, w_up_pool, w_up_nsa)


def _route(logits):
    lane = lax.broadcasted_iota(I32, logits.shape, 1)
    is_grp = lane < N_GROUPS
    p_grp = _masked_softmax(logits, is_grp)
    p_top = jnp.max(p_grp, axis=-1, keepdims=True)
    grp = jnp.min(jnp.where(is_grp & (p_grp == p_top), lane, LANES), axis=-1, keepdims=True)
    first = N_GROUPS + grp * EXPERTS_PER_GROUP
    in_grp = (lane >= first) & (lane < first + EXPERTS_PER_GROUP)
    v = _masked_softmax(logits, in_grp)
    v0 = jnp.max(jnp.where(in_grp, v, -1.0), axis=-1, keepdims=True)
    i0 = jnp.min(jnp.where(in_grp & (v == v0), lane, LANES), axis=-1, keepdims=True)
    rest = in_grp & (lane != i0)
    v1 = jnp.max(jnp.where(rest, v, -1.0), axis=-1, keepdims=True)
    i1 = jnp.min(jnp.where(rest & (v == v1), lane, LANES), axis=-1, keepdims=True)
    tot = v0 + v1
    out = jnp.where(lane == 0, (i0 - N_GROUPS).astype(F32), 0.0)
    out = jnp.where(lane == 1, (i1 - N_GROUPS).astype(F32), out)
    out = jnp.where(lane == 2, p_top * (v0 / tot), out)
    out = jnp.where(lane == 3, p_top * (v1 / tot), out)
    return out


def _outproj_kernel(mg_ref, x_ref, w_ref, mp_ref, ms_ref, g_ref, b_ref, wr_ref, br_ref,
                    x1_ref, u_ref, rt_ref, *, n_ptiles, alpha):
    is_s = pl.program_id(0) >= n_ptiles
    gate1 = _mod_row(is_s, mp_ref, ms_ref, 0)
    shift2 = _mod_row(is_s, mp_ref, ms_ref, 1)
    scale2 = _mod_row(is_s, mp_ref, ms_ref, 2)
    mix = _dot(mg_ref[...], w_ref[...])
    x1 = _layer_norm(alpha * x_ref[...] + gate1 * mix, g_ref[...], b_ref[...])
    x1_ref[...] = x1
    u = x1 * (1.0 + scale2) + shift2
    u_ref[...] = u
    wr_hi, wr_lo = _split_bf16(wr_ref[...])
    u_hi, u_lo = _split_bf16(u)
    logits = _dot_nt(u_hi, wr_hi) + _dot_nt(u_lo, wr_hi) + _dot_nt(u_hi, wr_lo) + br_ref[...]
    rt_ref[...] = _route(logits)


def _outproj(rows, merged, x, w_out_bf, modp, mods, ln_g, ln_b, w_router_t, b_router, alpha):
    d = x.shape[1]
    row = pl.BlockSpec((ROW_TILE, d), lambda i: (i, 0))
    vec = pl.BlockSpec((1, d), lambda i: (0, 0))
    return pl.pallas_call(
        functools.partial(_outproj_kernel, n_ptiles=rows.n_ptiles, alpha=alpha),
        grid=(rows.n_rows // ROW_TILE,),
        in_specs=[row, row, pl.BlockSpec((d, d), lambda i: (0, 0))] + rows.mod_specs(d, 1)
        + [vec, vec, pl.BlockSpec((LANES, d), lambda i: (0, 0)), pl.BlockSpec((1, LANES), lambda i: (0, 0))],
        out_specs=[row, row, pl.BlockSpec((ROW_TILE, LANES), lambda i: (i, 0))],
        out_shape=[jax.ShapeDtypeStruct((rows.n_rows, d), F32),
                   jax.ShapeDtypeStruct((rows.n_rows, d), F32),
                   jax.ShapeDtypeStruct((rows.n_rows, LANES), F32)],
        compiler_params=_cparams("arbitrary"),
        name="outproj_ln_route",
    )(merged, x, w_out_bf, modp, mods, ln_g.reshape(1, d), ln_b.reshape(1, d), w_router_t, b_router)


def _moe_scatter_kernel(dest_ref, pad_ref, u_ref, xs_ref, zero_ref, sem, *, n_real):
    i = pl.program_id(0)
    tm = u_ref.shape[0]

    def pad_copy(r):
        return pltpu.make_async_copy(zero_ref, xs_ref.at[pl.ds(r, 1)], sem)

    @pl.when(i == 0)
    def _():
        zero_ref[...] = jnp.zeros(zero_ref.shape, zero_ref.dtype)
        for e in range(N_EXPERTS + 1):
            lo, hi = pad_ref[2 * e], pad_ref[2 * e + 1]

            def issue_pad(r, c):
                pad_copy(r).start()
                return c

            def drain_pad(r, c):
                pad_copy(r).wait()
                return c

            lax.fori_loop(lo, hi, issue_pad, 0)
            lax.fori_loop(lo, hi, drain_pad, 0)

    def row_copy(r, dst):
        return pltpu.make_async_copy(u_ref.at[pl.ds(r, 1)], xs_ref.at[pl.ds(dst, 1)], sem)

    def issue(r, c):
        base = 2 * (i * tm + r)
        row_copy(r, dest_ref[base]).start()
        row_copy(r, dest_ref[base + 1]).start()
        return c

    def drain(r, c):
        row_copy(r, 0).wait()
        row_copy(r, 0).wait()
        return c

    n_here = jnp.minimum(tm, n_real - i * tm)
    lax.fori_loop(0, n_here, issue, 0)
    lax.fori_loop(0, n_here, drain, 0)


def _moe_scatter(dest, pads, u, n_real, r_pad):
    d = u.shape[1]
    tm = EXPERT_TILE
    grid_spec = pltpu.PrefetchScalarGridSpec(
        num_scalar_prefetch=2,
        grid=(-(-n_real // tm),),
        in_specs=[pl.BlockSpec((tm, d), lambda i, *_: (i, 0))],
        out_specs=pl.BlockSpec(memory_space=pl.ANY),
        scratch_shapes=[pltpu.VMEM((1, d), F32), pltpu.SemaphoreType.DMA(())],
    )
    return pl.pallas_call(
        functools.partial(_moe_scatter_kernel, n_real=n_real),
        grid_spec=grid_spec,
        out_shape=jax.ShapeDtypeStruct((r_pad, d), F32),
        compiler_params=_cparams("arbitrary"),
        name="moe_scatter",
    )(dest, pads, u)


def _moe_expert_kernel(te_ref, used_ref, x_ref, wg_ref, wu_ref, wd_ref, o_ref, wgb_ref, wub_ref, wdb_ref):
    t = pl.program_id(0)
    fresh = jnp.logical_or(t == 0, te_ref[t] != te_ref[jnp.maximum(t - 1, 0)])

    @pl.when(fresh)
    def _():
        wgb_ref[...] = wg_ref[0].astype(BF16)
        wub_ref[...] = wu_ref[0].astype(BF16)
        wdb_ref[...] = wd_ref[0].astype(BF16)

    @pl.when(t < used_ref[0])
    def _():
        x = x_ref[...].astype(BF16)
        h = jax.nn.silu(_dot(x, wgb_ref[...])) * _dot(x, wub_ref[...])
        o_ref[...] = _dot(h.astype(BF16), wdb_ref[...])

    @pl.when(t >= used_ref[0])
    def _():
        o_ref[...] = jnp.zeros(o_ref.shape, o_ref.dtype)


def _moe_experts(tile_expert, n_used, xs, w_gate, w_up, w_down):
    r_pad, d = xs.shape
    f = w_gate.shape[2]
    tm = EXPERT_TILE
    n_tiles = r_pad // tm

    def x_map(t, te, nu):
        return (jnp.minimum(t, jnp.maximum(nu[0] - 1, 0)), 0)

    grid_spec = pltpu.PrefetchScalarGridSpec(
        num_scalar_prefetch=2,
        grid=(n_tiles,),
        in_specs=[pl.BlockSpec((tm, d), x_map),
                  pl.BlockSpec((1, d, f), lambda t, te, nu: (te[t], 0, 0)),
                  pl.BlockSpec((1, d, f), lambda t, te, nu: (te[t], 0, 0)),
                  pl.BlockSpec((1, f, d), lambda t, te, nu: (te[t], 0, 0))],
        out_specs=pl.BlockSpec((tm, d), lambda t, te, nu: (t, 0)),
        scratch_shapes=[pltpu.VMEM((d, f), BF16), pltpu.VMEM((d, f), BF16), pltpu.VMEM((f, d), BF16)],
    )
    return pl.pallas_call(
        _moe_expert_kernel,
        grid_spec=grid_spec,
        out_shape=jax.ShapeDtypeStruct((r_pad, d), F32),
        compiler_params=_cparams("arbitrary"),
        name="moe_experts",
    )(tile_expert, n_used, xs, w_gate, w_up, w_down)


def _moe_combine_kernel(pos_ref, y_ref, x1_ref, rt_ref, mp_ref, ms_ref, g_ref, b_ref, x2_ref, *rest,
                        n_ptiles, alpha, emit_next):
    if emit_next:
        u_ref, buf_ref, sem = rest
    else:
        buf_ref, sem = rest
    i = pl.program_id(0)
    tm = x1_ref.shape[0]

    def row_copy(r, k, src):
        return pltpu.make_async_copy(y_ref.at[pl.ds(src, 1)], buf_ref.at[k, pl.ds(r, 1)], sem)

    def issue(r, c):
        base = 2 * (i * tm + r)
        row_copy(r, 0, pos_ref[base]).start()
        row_copy(r, 1, pos_ref[base + 1]).start()
        return c

    def drain(r, c):
        row_copy(r, 0, 0).wait()
        row_copy(r, 1, 0).wait()
        return c

    lax.fori_loop(0, tm, issue, 0)
    lax.fori_loop(0, tm, drain, 0)
    is_s = i >= n_ptiles
    gate2 = _mod_row(is_s, mp_ref, ms_ref, 0)
    ffn = rt_ref[:, 2:3] * buf_ref[0] + rt_ref[:, 3:4] * buf_ref[1]
    x2 = _layer_norm(alpha * x1_ref[...] + gate2 * ffn, g_ref[...], b_ref[...])
    x2_ref[...] = x2
    if emit_next:
        shift = _mod_row(is_s, mp_ref, ms_ref, 1)
        scale = _mod_row(is_s, mp_ref, ms_ref, 2)
        u_ref[...] = (x2 * (1.0 + scale) + shift).astype(u_ref.dtype)


def _moe_combine(rows, pos, y_rows, x1, route, modp, mods, ln_g, ln_b, alpha, emit_next):
    d = x1.shape[1]
    row = pl.BlockSpec((ROW_TILE, d), lambda i, *_: (i, 0))
    vec = pl.BlockSpec((1, d), lambda i, *_: (0, 0))
    out_specs = [row]
    out_shape = [jax.ShapeDtypeStruct((rows.n_rows, d), F32)]
    if emit_next:
        out_specs.append(row)
        out_shape.append(jax.ShapeDtypeStruct((rows.n_rows, d), BF16))
    grid_spec = pltpu.PrefetchScalarGridSpec(
        num_scalar_prefetch=1,
        grid=(rows.n_rows // ROW_TILE,),
        in_specs=[pl.BlockSpec(memory_space=pl.ANY), row, pl.BlockSpec((ROW_TILE, LANES), lambda i, *_: (i, 0))]
        + rows.mod_specs(d, 2) + [vec, vec],
        out_specs=out_specs,
        scratch_shapes=[pltpu.VMEM((2, ROW_TILE, d), F32), pltpu.SemaphoreType.DMA(())],
    )
    out = pl.pallas_call(
        functools.partial(_moe_combine_kernel, n_ptiles=rows.n_ptiles, alpha=alpha, emit_next=emit_next),
        grid_spec=grid_spec,
        out_shape=out_shape,
        compiler_params=_cparams("arbitrary"),
        name="moe_combine_ln",
    )(pos, y_rows, x1, route, modp, mods, ln_g.reshape(1, d), ln_b.reshape(1, d))
    return out if emit_next else (out[0], None)


def _moe_dispatch(route, n_real, n_rows):
    tm = EXPERT_TILE
    n_pairs = 2 * n_real
    n_tiles = -(-(n_pairs + N_EXPERTS * (tm - 1)) // tm)
    eid = route[:n_real, 0:2].astype(I32).reshape(n_pairs, 1)
    onehot = (eid == jnp.arange(N_EXPERTS, dtype=I32)[None, :]).astype(I32)
    seen = jnp.cumsum(onehot, axis=0)
    counts = seen[-1]
    padded = -(-counts // tm) * tm
    ends_pad = jnp.cumsum(padded)
    starts_pad = ends_pad - padded
    dest = jnp.sum(onehot * (starts_pad[None, :] + seen - 1), axis=1).astype(I32)
    pos = jnp.concatenate([dest, jnp.zeros((2 * (n_rows - n_real),), I32)])
    pads = jnp.stack([starts_pad + counts, ends_pad], axis=1).reshape(-1)
    pads = jnp.concatenate([pads, ends_pad[-1:], jnp.full((1,), n_tiles * tm)]).astype(I32)
    n_used = (ends_pad[-1] // tm).astype(I32)
    tile_start = jnp.arange(n_tiles, dtype=I32) * tm
    tile_e = jnp.minimum(jnp.sum((tile_start[:, None] >= ends_pad[None, :]).astype(I32), axis=1), N_EXPERTS - 1)
    last_e = jnp.sum(jnp.where(jnp.arange(n_tiles) == jnp.maximum(n_used - 1, 0), tile_e, 0))
    tile_e = jnp.where(jnp.arange(n_tiles) < n_used, tile_e, last_e).astype(I32)
    return pos, pads, tile_e, n_used.reshape(1), n_tiles * tm


def _cover_matrix(n_chunk, n_cmp, n_sel, n_sel_pad):
    ci = jnp.arange(n_chunk)[:, None]
    sj = jnp.arange(n_sel_pad)[None, :]
    hit = ((ci * CMP_STRIDE < sj * SEL_BLOCK + SEL_BLOCK)
           & (ci * CMP_STRIDE + CMP_LEN - 1 >= sj * SEL_BLOCK)
           & (ci < n_cmp) & (sj < n_sel))
    return hit.astype(BF16)


def kernel(x_prompt, x_sample, c_prompt, c_sample, cache_kv, cache_win, state_pool, page_table, w_ada, b_ada,
           w_in, cmp_w1, cmp_pe, cmp_w2, pool_w, pool_scale, w_up_pool, w_up_nsa, w_out, ln1_g, ln1_b, w_rg,
           b_rg, w_re, b_re, w_gate, w_up, w_down, ln2_g, ln2_b):
    batch, seq, d = x_prompt.shape
    nb, ts, _ = x_sample.shape
    n_layers = w_ada.shape[0]
    page = cache_kv.shape[2]
    n_pages = page_table.shape[1]
    past = n_pages * page
    win_buf = cache_win.shape[2]
    pool_width = pool_w.shape[1] * pool_w.shape[2]
    q_width = N_HEADS * HEAD_DIM
    kv_width = N_KV_PLANES * GRP_WIDTH
    gn_width = 3 * N_HEADS
    alpha = (2 * n_layers) ** 0.25
    cmp_hid = cmp_w1.shape[-1]
    chunk_w = CMP_STRIDE * HEAD_DIM

    rows = _Rows(batch, seq, nb * ts)
    n_p, n_s, n_rows = rows.n_prompt, rows.n_sample, rows.n_rows
    n_real = n_p + n_s

    x = jnp.concatenate([x_prompt.reshape(n_p, d), x_sample.reshape(n_s, d),
                         jnp.zeros((rows.sample_pad - n_s, d), F32)], axis=0)

    n_seq = batch + nb
    c_all = jnp.concatenate([c_prompt, c_sample, jnp.zeros((-n_seq % SUBLANES, d), F32)], axis=0)
    mod = _adaln(c_all, w_ada, b_ada)[:, :n_seq].reshape(n_layers, n_seq, 6, d)
    nxt = jnp.concatenate([mod[1:], jnp.zeros_like(mod[:1])], axis=0)
    mod9 = jnp.stack([mod[:, :, 0], mod[:, :, 1], jnp.zeros_like(mod[:, :, 0]),
                      mod[:, :, 2], mod[:, :, 3], mod[:, :, 4],
                      mod[:, :, 5], nxt[:, :, 0], nxt[:, :, 1]], axis=2).reshape(n_layers, n_seq, 3, 3, d)
    modp_all = jnp.pad(mod9[:, :batch], ((0, 0), (0, 0), (0, 0), (0, SUBLANES - 3), (0, 0)))
    mods_all = jnp.repeat(mod9[:, batch:], ts, axis=1).transpose(0, 2, 3, 1, 4)
    mods_all = jnp.pad(mods_all, ((0, 0), (0, 0), (0, 0), (0, rows.sample_pad - n_s), (0, 0)))

    w_in_t = w_in.transpose(0, 2, 1)
    o1 = pool_width
    o2 = o1 + q_width
    o3 = o2 + kv_width
    o4 = o3 + gn_width
    w_gm_t = w_in_t[:, o4:]
    w_gn_t = jnp.pad(w_in_t[:, o3:o4], ((0, 0), (0, -gn_width % LANES), (0, 0)))
    w_out_bf = w_out.astype(BF16)
    n_rt = N_GROUPS + N_EXPERTS
    w_router_t = jnp.concatenate([w_rg.transpose(0, 2, 1), w_re.transpose(0, 2, 1),
                                  jnp.zeros((n_layers, LANES - n_rt, d), F32)], axis=1)
    b_router = jnp.concatenate([b_rg, b_re, jnp.zeros((n_layers, LANES - n_rt), F32)], axis=1)
    cache_t = cache_kv.transpose(0, 1, 3, 4, 5, 2)
    win_t = cache_win.transpose(0, 1, 3, 4, 5, 2)
    state_pm = state_pool.transpose(0, 2, 1, 3)
    w2_t = cmp_w2.transpose(0, 1, 3, 2)

    slopes = jnp.exp2(-8.0 * jnp.arange(1, N_HEADS + 1, dtype=F32) / N_HEADS)
    nch_p = seq // CMP_STRIDE
    ncmp_p = (seq - CMP_LEN) // CMP_STRIDE + 1
    nsel_p = -(-seq // SEL_BLOCK)
    cover_p_t = _cover_matrix(nch_p, ncmp_p, nsel_p, -(-nsel_p // LANES) * LANES).T

    tk_s = past + ts
    ncmp_s = (tk_s - CMP_LEN) // CMP_STRIDE + 1
    nch_s = past // CMP_STRIDE
    assert (ncmp_s + CMP_SEGS - 1) * CMP_STRIDE <= past and nch_s >= ncmp_s
    assert page % CMP_STRIDE == 0 and page % SEL_BLOCK == 0 and page % LANES == 0 and ts <= LANES
    nsel_s = -(-tk_s // SEL_BLOCK)
    nsel_s_pad = -(-nsel_s // LANES) * LANES
    cover_s = _cover_matrix(nch_s, ncmp_s, nsel_s, nsel_s_pad)
    nk_s = past + LANES
    expand_s = (jnp.arange(nsel_s_pad)[:, None] == jnp.arange(nk_s)[None, :] // SEL_BLOCK).astype(BF16)
    gt = KV_HEADS * ts
    slopes_gh = slopes.reshape(KV_HEADS, Q_PER_KV)
    slope_s = jnp.broadcast_to(slopes_gh.T[:, :, None], (Q_PER_KV, KV_HEADS, ts)).reshape(Q_PER_KV * gt, 1)
    qpos_s = jnp.broadcast_to(past + jnp.arange(ts, dtype=I32), (Q_PER_KV, KV_HEADS, ts)).reshape(Q_PER_KV * gt, 1)
    eye_g = jnp.eye(KV_HEADS, dtype=BF16)

    u = _modulate(rows, x, modp_all[0], mods_all[0])
    outs = {k: [] for k in ("kv_p", "win_p", "pool_p", "kv_s", "win_s", "pool_s")}
    for l in range(n_layers):
        up = _proj_rows(u, w_in_t, l, 0, pool_width, MM_TILE_N, F32)
        q = _proj_rows(u, w_in_t, l, o1, q_width, MM_TILE_N, BF16, scale=HEAD_DIM ** -0.5)
        kvc = _proj_rows(u, w_in_t, l, o2, 2 * GRP_WIDTH, MM_TILE_N, F32)
        kv_rest_s = _proj_rows(u, w_in_t, l, o2 + 2 * GRP_WIDTH, 4 * GRP_WIDTH, MM_TILE_N, F32,
                               m0=n_p, m_rows=rows.sample_pad)
        gn = _proj_rows(u, w_gn_t, l, 0, LANES, LANES, F32, act="sigmoid")
        kv_t = _proj_t(u, w_in_t, l, o2, N_CACHED_PLANES, batch, seq)
        kw_t = _proj_t(u, w_in_t, l, o2 + N_CACHED_PLANES * GRP_WIDTH, 2, batch, seq)

        kv_s = jnp.concatenate([kvc[n_p:n_real], kv_rest_s[:n_s]], axis=1)
        kv_s = kv_s.reshape(nb, ts, N_KV_PLANES, KV_HEADS, HEAD_DIM)
        up_p = up[:n_p].reshape(batch, seq, pool_width)
        up_s = up[n_p:n_real].reshape(nb, ts, pool_width)
        outs["kv_p"].append(kv_t.reshape(batch, N_CACHED_PLANES, KV_HEADS, HEAD_DIM, seq))
        n_keep = min(WINDOW, seq)
        outs["win_p"].append(kw_t[:, :, :, seq - n_keep:].reshape(batch, 2, KV_HEADS, HEAD_DIM, n_keep))
        outs["pool_p"].append(jnp.concatenate([jnp.zeros((POOL_HIST, batch, pool_width), F32),
                                               up_p.transpose(1, 0, 2)], axis=0)[-POOL_HIST:])
        new_t = kv_s.transpose(0, 2, 3, 4, 1)
        win_new = jnp.concatenate([win_t[l], new_t[:, N_CACHED_PLANES:]], axis=-1)[..., -win_buf:]
        outs["kv_s"].append(kv_s[:, :, :N_CACHED_PLANES])
        outs["win_s"].append(win_new)
        pool_ext = jnp.concatenate([state_pm[l], up_s.transpose(1, 0, 2)], axis=0)
        outs["pool_s"].append(pool_ext[-POOL_HIST:])

        yp_p = _pool_prompt(up, batch, seq, pool_w[l], pool_scale[l])
        yp_s = _pool_sample(pool_ext, ts, past, pool_w[l], pool_scale[l])
        y_pool = jnp.concatenate([yp_p, yp_s.transpose(1, 0, 2).reshape(n_s, pool_width),
                                  jnp.zeros((rows.sample_pad - n_s, pool_width), BF16)], axis=0)

        w1 = cmp_w1[l].reshape(2, CMP_SEGS, chunk_w, cmp_hid)
        pe = cmp_pe[l].reshape(2, CMP_SEGS, 1, chunk_w)
        chunks_p = kvc[:n_p].reshape(batch, nch_p, CMP_STRIDE, 2, KV_HEADS, HEAD_DIM)
        chunks_p = chunks_p.transpose(3, 0, 4, 1, 2, 5).reshape(2, batch, KV_HEADS, nch_p, chunk_w)
        cmp_rows_p, cmp_t_p = _compress_prompt(chunks_p, w1, pe, w2_t[l])
        cmp_s = _compress_sample(page_table, cache_t, l, w1, pe, w2_t[l])

        q_hm = q[:n_p].reshape(batch, seq, N_HEADS, HEAD_DIM).transpose(0, 2, 1, 3)
        gates_p = gn[:n_p, :gn_width].reshape(batch, seq, KV_HEADS, 3 * Q_PER_KV).transpose(0, 2, 1, 3)
        oa_p = _attn_prompt(slopes, q_hm, cmp_rows_p, cmp_t_p, kv_t, kw_t, gates_p, cover_p_t)
        ya_p = oa_p.transpose(0, 2, 1, 3).reshape(n_p, q_width)

        q_s = q[n_p:n_real].reshape(nb, ts, KV_HEADS, Q_PER_KV, HEAD_DIM).transpose(0, 3, 2, 1, 4)
        qbd = (q_s[:, :, :, :, None, :] * eye_g[None, None, :, None, :, None]).reshape(nb, Q_PER_KV * gt, GRP_WIDTH)
        kvc_all = cmp_s.transpose(0, 1, 3, 2, 4).reshape(2, nb, nch_s, GRP_WIDTH)
        new_blk = jnp.pad(new_t[:, 2:].reshape(nb, 4, GRP_WIDTH, ts), ((0, 0), (0, 0), (0, 0), (0, LANES - ts)))
        gates_s = gn[n_p:n_real, :gn_width].reshape(nb, ts, KV_HEADS, Q_PER_KV, 3).transpose(0, 3, 2, 1, 4)
        gates_s = gates_s.reshape(nb, Q_PER_KV * gt, 3)
        oa_s = _attn_sample(page_table, qbd, kvc_all[0], kvc_all[1], cache_t, new_blk, win_t, gates_s,
                            slope_s, qpos_s, cover_s, expand_s, layer=l, past=past, ts=ts, n_cmp=ncmp_s,
                            n_sel=nsel_s, gt=gt)
        ya_s = oa_s.reshape(nb, Q_PER_KV, KV_HEADS, ts, HEAD_DIM).transpose(0, 3, 2, 1, 4).reshape(n_s, q_width)
        y_attn = jnp.concatenate([ya_p, ya_s.astype(BF16),
                                  jnp.zeros((rows.sample_pad - n_s, q_width), BF16)], axis=0)

        merged = _merge(u, y_pool, y_attn, w_gm_t, w_up_pool, w_up_nsa, l)
        x1, u2, route = _outproj(rows, merged, x, w_out_bf[l], modp_all[l], mods_all[l], ln1_g[l], ln1_b[l],
                                 w_router_t[l], b_router[l].reshape(1, LANES), alpha)

        pos, pads, tile_e, n_used, r_pad = _moe_dispatch(route, n_real, n_rows)
        xs = _moe_scatter(pos, pads, u2, n_real, r_pad)
        y_rows = _moe_experts(tile_e, n_used, xs, w_gate[l], w_up[l], w_down[l])
        x, u = _moe_combine(rows, pos, y_rows, x1, route, modp_all[l], mods_all[l], ln2_g[l], ln2_b[l], alpha,
                            emit_next=l + 1 < n_layers)

    y_prompt = x[:n_p].reshape(batch, seq, d)
    y_sample = x[n_p:n_real].reshape(nb, ts, d)
    new_kv_p = jnp.stack(outs["kv_p"]).transpose(0, 1, 5, 2, 3, 4)
    new_win_p = jnp.stack(outs["win_p"]).transpose(0, 1, 5, 2, 3, 4)
    new_pool_p = jnp.stack(outs["pool_p"]).transpose(0, 2, 1, 3)
    new_win_s = jnp.stack(outs["win_s"]).transpose(0, 1, 5, 2, 3, 4)
    new_pool_s = jnp.stack(outs["pool_s"]).transpose(0, 2, 1, 3)
    return (y_prompt, y_sample, new_kv_p, new_win_p, new_pool_p, jnp.stack(outs["kv_s"]), new_win_s, new_pool_s)
```

```python
import functools

import jax
import jax.numpy as jnp
from jax import lax
from jax.experimental import pallas as pl
from jax.experimental.pallas import tpu as pltpu

F32 = jnp.float32
BF16 = jnp.bfloat16
I32 = jnp.int32

POOL_WINDOWS = (2, 4, 8, 16)
POOL_HIST = max(POOL_WINDOWS) - 1
N_HEADS = 16
KV_HEADS = 4
HEAD_DIM = 64
Q_PER_KV = N_HEADS // KV_HEADS
GRP_WIDTH = KV_HEADS * HEAD_DIM
CMP_LEN = 32
CMP_STRIDE = 16
CMP_SEGS = CMP_LEN // CMP_STRIDE
SEL_BLOCK = 64
SEL_TOPK = 16
WINDOW = 512
Q_BLOCK = 128
N_KV_PLANES = 6
N_CACHED_PLANES = 4
FORCE_BONUS = 1.0e4
N_GROUPS = 4
EXPERTS_PER_GROUP = 4
N_EXPERTS = N_GROUPS * EXPERTS_PER_GROUP
LN_EPS = 1e-5
NEG_INF = -1e30
NEG_CLAMP = -1e29

LANES = 128
SUBLANES = 8
VMEM_LIMIT = 52 * 1024 * 1024

ROW_TILE = 256
MM_TILE_M = 512
MM_TILE_N = 512
SEL_KV_TILE = 512
EXPERT_TILE = 256
CMP_K_GROUP = 4


def _cparams(*sem):
    return pltpu.CompilerParams(dimension_semantics=sem, vmem_limit_bytes=VMEM_LIMIT)


def _dot(a, b):
    return jnp.dot(a, b, preferred_element_type=F32)


def _dot_nt(a, b):
    return lax.dot_general(a, b, (((1,), (1,)), ((), ())), preferred_element_type=F32)


def _pv_t(v_t, p):
    return _dot_nt(v_t, p).T


def _split_bf16(a):
    hi = a.astype(BF16)
    return hi, (a - hi.astype(F32)).astype(BF16)


def _masked_softmax(s, valid):
    s = jnp.where(valid, s, NEG_INF)
    m = jnp.max(s, axis=-1, keepdims=True)
    e = jnp.where(valid, jnp.exp(s - m), 0.0)
    l = jnp.sum(e, axis=-1, keepdims=True)
    return e / jnp.where(l > 0.0, l, 1.0)


def _softmax_addmask(z):
    m = jnp.maximum(jnp.max(z, axis=-1, keepdims=True), NEG_CLAMP)
    e = jnp.exp(z - m)
    l = jnp.sum(e, axis=-1, keepdims=True)
    return e, jnp.where(l > 0.0, l, 1.0)


def _layer_norm(x, g, b):
    mu = jnp.mean(x, axis=-1, keepdims=True)
    xc = x - mu
    var = jnp.mean(xc * xc, axis=-1, keepdims=True)
    return xc * lax.rsqrt(var + LN_EPS) * g + b


def _top_k_mask(score, n_top):
    col = lax.broadcasted_iota(I32, score.shape, 1)
    big = jnp.int32(score.shape[1])

    def body(_, carry):
        sc, sel = carry
        m = jnp.max(sc, axis=-1, keepdims=True)
        idx = jnp.min(jnp.where(sc == m, col, big), axis=-1, keepdims=True)
        hit = col == idx
        return jnp.where(hit, -jnp.inf, sc), jnp.where(hit, 1.0, sel)

    _, sel = lax.fori_loop(0, n_top, body, (score, jnp.zeros(score.shape, F32)))
    return sel


def _block_scores(imp, blk, qpos, n_sel):
    cur = qpos // SEL_BLOCK
    forced = (blk == 0) | (blk == cur) | (blk == cur - 1)
    visible = blk * SEL_BLOCK <= qpos
    score = jnp.where(visible, imp + jnp.where(forced, FORCE_BONUS, 0.0), -1.0)
    return jnp.where(blk < n_sel, score, -2.0)


def _adaln_kernel(c_ref, w_ref, b_ref, o_ref):
    c = c_ref[...]
    a = (c * jax.nn.sigmoid(c)).astype(BF16)
    o_ref[0] = _dot(a, w_ref[0].astype(BF16)) + b_ref[0]


def _adaln(c_all, w_ada, b_ada):
    n_layers, d, n6 = w_ada.shape
    mp = c_all.shape[0]
    tn = 1024
    assert n6 % tn == 0
    return pl.pallas_call(
        _adaln_kernel,
        grid=(n_layers, n6 // tn),
        in_specs=[pl.BlockSpec((mp, d), lambda l, n: (0, 0)),
                  pl.BlockSpec((1, d, tn), lambda l, n: (l, 0, n)),
                  pl.BlockSpec((1, 1, tn), lambda l, n: (l, 0, n))],
        out_specs=pl.BlockSpec((1, mp, tn), lambda l, n: (l, 0, n)),
        out_shape=jax.ShapeDtypeStruct((n_layers, mp, n6), F32),
        compiler_params=_cparams("arbitrary", "arbitrary"),
        name="adaln",
    )(c_all, w_ada, b_ada.reshape(n_layers, 1, n6))


class _Rows:
    def __init__(self, batch, seq, n_sample):
        self.batch = batch
        self.seq = seq
        self.n_prompt = batch * seq
        self.n_sample = n_sample
        self.sample_pad = -(-n_sample // MM_TILE_M) * MM_TILE_M
        self.n_rows = self.n_prompt + self.sample_pad
        assert seq % MM_TILE_M == 0 and MM_TILE_M % ROW_TILE == 0
        self.n_ptiles = self.n_prompt // ROW_TILE

    def mod_specs(self, d, group):
        tiles_per_seq = self.seq // ROW_TILE
        last = self.batch - 1
        n_pt = self.n_ptiles
        return [pl.BlockSpec((1, 1, SUBLANES, d),
                             lambda i, *_: (jnp.minimum(i // tiles_per_seq, last), group, 0, 0)),
                pl.BlockSpec((1, 3, ROW_TILE, d),
                             lambda i, *_: (group, 0, jnp.maximum(i - n_pt, 0), 0))]


def _mod_row(is_sample, mp_ref, ms_ref, k):
    return jnp.where(is_sample, ms_ref[0, k], mp_ref[0, 0, k:k + 1, :])


def _modulate_kernel(x_ref, mp_ref, ms_ref, u_ref, *, n_ptiles):
    is_s = pl.program_id(0) >= n_ptiles
    shift = _mod_row(is_s, mp_ref, ms_ref, 0)
    scale = _mod_row(is_s, mp_ref, ms_ref, 1)
    u_ref[...] = (x_ref[...] * (1.0 + scale) + shift).astype(BF16)


def _modulate(rows, x, modp, mods):
    d = x.shape[1]
    return pl.pallas_call(
        functools.partial(_modulate_kernel, n_ptiles=rows.n_ptiles),
        grid=(rows.n_rows // ROW_TILE,),
        in_specs=[pl.BlockSpec((ROW_TILE, d), lambda i: (i, 0))] + rows.mod_specs(d, 0),
        out_specs=pl.BlockSpec((ROW_TILE, d), lambda i: (i, 0)),
        out_shape=jax.ShapeDtypeStruct((rows.n_rows, d), BF16),
        compiler_params=_cparams("arbitrary"),
        name="modulate",
    )(x, modp, mods)


def _proj_rows_kernel(x_ref, w_ref, o_ref, wb_ref, *, act, scale):
    @pl.when(pl.program_id(1) == 0)
    def _():
        wb_ref[...] = w_ref[0].astype(BF16)

    y = _dot_nt(x_ref[...], wb_ref[...])
    if act == "sigmoid":
        y = jax.nn.sigmoid(y)
    if scale != 1.0:
        y = y * scale
    o_ref[...] = y.astype(o_ref.dtype)


def _proj_rows(x, w_t, layer, row0, ncols, tn, out_dtype, act=None, scale=1.0, m0=0, m_rows=None):
    k = x.shape[1]
    m_rows = x.shape[0] - m0 if m_rows is None else m_rows
    assert m0 % MM_TILE_M == 0 and m_rows % MM_TILE_M == 0 and row0 % tn == 0 and ncols % tn == 0
    mb, rb = m0 // MM_TILE_M, row0 // tn
    return pl.pallas_call(
        functools.partial(_proj_rows_kernel, act=act, scale=scale),
        grid=(ncols // tn, m_rows // MM_TILE_M),
        in_specs=[pl.BlockSpec((MM_TILE_M, k), lambda n, i: (mb + i, 0)),
                  pl.BlockSpec((1, tn, k), lambda n, i: (layer, rb + n, 0))],
        out_specs=pl.BlockSpec((MM_TILE_M, tn), lambda n, i: (i, n)),
        out_shape=jax.ShapeDtypeStruct((m_rows, ncols), out_dtype),
        scratch_shapes=[pltpu.VMEM((tn, k), BF16)],
        compiler_params=_cparams("arbitrary", "arbitrary"),
        name="proj_rows",
    )(x, w_t)


def _proj_t_kernel(w_ref, x_ref, o_ref, wb_ref):
    @pl.when((pl.program_id(1) == 0) & (pl.program_id(2) == 0))
    def _():
        wb_ref[...] = w_ref[0].astype(BF16)

    o_ref[0, 0] = _dot_nt(wb_ref[...], x_ref[...])


def _proj_t(x, w_t, layer, row0, n_planes, batch, seq):
    k = x.shape[1]
    tt = 2 * MM_TILE_M if seq % (2 * MM_TILE_M) == 0 else MM_TILE_M
    assert row0 % GRP_WIDTH == 0 and seq % tt == 0
    rb = row0 // GRP_WIDTH
    nt = seq // tt
    return pl.pallas_call(
        _proj_t_kernel,
        grid=(n_planes, batch, nt),
        in_specs=[pl.BlockSpec((1, GRP_WIDTH, k), lambda p, b, i: (layer, rb + p, 0)),
                  pl.BlockSpec((tt, k), lambda p, b, i: (b * nt + i, 0))],
        out_specs=pl.BlockSpec((1, 1, GRP_WIDTH, tt), lambda p, b, i: (b, p, 0, i)),
        out_shape=jax.ShapeDtypeStruct((batch, n_planes, GRP_WIDTH, seq), F32),
        scratch_shapes=[pltpu.VMEM((GRP_WIDTH, k), BF16)],
        compiler_params=_cparams("arbitrary", "arbitrary", "arbitrary"),
        name="proj_t",
    )(w_t, x)


def _pool_prompt_kernel(cur_ref, prev_ref, pw_ref, sc_ref, o_ref, ext_ref, *, tt, pg):
    i = pl.program_id(1)
    halo = 2 * SUBLANES
    ext_ref[0:halo, :] = jnp.where(i == 0, 0.0, prev_ref[...])
    ext_ref[halo:, :] = cur_ref[...]
    pos = i * tt + lax.broadcasted_iota(I32, (tt, 1), 0)
    for g, w in enumerate(POOL_WINDOWS):
        cols = slice(g * pg, (g + 1) * pg)
        cur = ext_ref[halo:halo + tt, cols]
        acc = cur
        for j in range(1, w):
            acc = acc + ext_ref[halo - j:halo - j + tt, cols]
        count = jnp.minimum(pos + 1, w).astype(F32)
        dlt = acc / count - cur
        y = _dot(dlt.astype(BF16), pw_ref[g].astype(BF16)) * sc_ref[:, cols]
        o_ref[:, cols] = y.astype(o_ref.dtype)


def _pool_prompt(up, batch, seq, pool_w, pool_scale):
    pw = up.shape[1]
    pg = pw // len(POOL_WINDOWS)
    tt = MM_TILE_M
    halo = 2 * SUBLANES
    assert POOL_HIST <= halo and seq % tt == 0
    nt = seq // tt
    return pl.pallas_call(
        functools.partial(_pool_prompt_kernel, tt=tt, pg=pg),
        grid=(batch, nt),
        in_specs=[pl.BlockSpec((tt, pw), lambda b, i: (b * nt + i, 0)),
                  pl.BlockSpec((halo, pw), lambda b, i: (jnp.maximum((b * nt + i) * (tt // halo) - 1, 0), 0)),
                  pl.BlockSpec((len(POOL_WINDOWS), pg, pg), lambda b, i: (0, 0, 0)),
                  pl.BlockSpec((1, pw), lambda b, i: (0, 0))],
        out_specs=pl.BlockSpec((tt, pw), lambda b, i: (b * nt + i, 0)),
        out_shape=jax.ShapeDtypeStruct((batch * seq, pw), BF16),
        scratch_shapes=[pltpu.VMEM((halo + tt, pw), F32)],
        compiler_params=_cparams("arbitrary", "arbitrary"),
        name="pool_prompt",
    )(up, up, pool_w, pool_scale.reshape(1, pw))


def _pool_sample_kernel(ext_ref, pw_ref, sc_ref, o_ref, *, ts, pg, pos0):
    for t in range(ts):
        for g, w in enumerate(POOL_WINDOWS):
            cols = slice(g * pg, (g + 1) * pg)
            cur = ext_ref[POOL_HIST + t, :, cols]
            acc = cur
            for j in range(1, w):
                acc = acc + ext_ref[POOL_HIST + t - j, :, cols]
            count = float(min(pos0 + t + 1, w))
            dlt = acc / count - cur
            y = _dot(dlt.astype(BF16), pw_ref[g].astype(BF16)) * sc_ref[:, cols]
            o_ref[t, :, cols] = y.astype(o_ref.dtype)


def _pool_sample(ext, ts, pos0, pool_w, pool_scale):
    n_ext, nb, pw = ext.shape
    pg = pw // len(POOL_WINDOWS)
    return pl.pallas_call(
        functools.partial(_pool_sample_kernel, ts=ts, pg=pg, pos0=pos0),
        grid=(1,),
        in_specs=[pl.BlockSpec((n_ext, nb, pw), lambda i: (0, 0, 0)),
                  pl.BlockSpec((len(POOL_WINDOWS), pg, pg), lambda i: (0, 0, 0)),
                  pl.BlockSpec((1, pw), lambda i: (0, 0))],
        out_specs=pl.BlockSpec((ts, nb, pw), lambda i: (0, 0, 0)),
        out_shape=jax.ShapeDtypeStruct((ts, nb, pw), BF16),
        compiler_params=_cparams("arbitrary"),
        name="pool_sample",
    )(ext, pool_w, pool_scale.reshape(1, pw))


def _compress_tail(ha, hb, w2t_ref):
    n_chunk = ha.shape[0]
    h = ha + pltpu.roll(hb, n_chunk - 1, 0)
    act = jax.nn.gelu(h).astype(BF16)
    w2t = w2t_ref[0].astype(BF16)
    return _dot_nt(act, w2t), _dot_nt(w2t, act)


def _compress_prompt_kernel(c_ref, w1_ref, pe_ref, w2t_ref, o_ref, ot_ref):
    c = c_ref[0, 0, 0]
    ha = _dot((c + pe_ref[0, 0]).astype(BF16), w1_ref[0, 0].astype(BF16))
    hb = _dot((c + pe_ref[0, 1]).astype(BF16), w1_ref[0, 1].astype(BF16))
    o_ref[0, 0, 0], ot_ref[0, 0, 0] = _compress_tail(ha, hb, w2t_ref)


def _compress_prompt(chunks, w1, pe, w2t):
    _, b, g, n_chunk, kdim = chunks.shape
    hid = w1.shape[-1]
    return pl.pallas_call(
        _compress_prompt_kernel,
        grid=(2, b, g),
        in_specs=[pl.BlockSpec((1, 1, 1, n_chunk, kdim), lambda p, bi, gi: (p, bi, gi, 0, 0)),
                  pl.BlockSpec((1, CMP_SEGS, kdim, hid), lambda p, bi, gi: (p, 0, 0, 0)),
                  pl.BlockSpec((1, CMP_SEGS, 1, kdim), lambda p, bi, gi: (p, 0, 0, 0)),
                  pl.BlockSpec((1, HEAD_DIM, hid), lambda p, bi, gi: (p, 0, 0))],
        out_specs=[pl.BlockSpec((1, 1, 1, n_chunk, HEAD_DIM), lambda p, bi, gi: (p, bi, gi, 0, 0)),
                   pl.BlockSpec((1, 1, 1, HEAD_DIM, n_chunk), lambda p, bi, gi: (p, bi, gi, 0, 0))],
        out_shape=[jax.ShapeDtypeStruct((2, b, g, n_chunk, HEAD_DIM), F32),
                   jax.ShapeDtypeStruct((2, b, g, HEAD_DIM, n_chunk), F32)],
        compiler_params=_cparams("arbitrary", "arbitrary", "arbitrary"),
        name="compress_prompt",
    )(chunks, w1, pe, w2t)


def _compress_sample_kernel(pt_ref, cache_ref, w1_ref, pe_ref, w2t_ref, perm_ref, o_ref, buf_ref, rows_ref,
                            w1b_ref, peb_ref, sem, *, layer, n_pages, page, nb):
    p = pl.program_id(0)
    b = pl.program_id(1)
    step = p * nb + b
    slot = step % 2
    n_chunk = n_pages * page // CMP_STRIDE
    kg = CMP_K_GROUP
    n_q = CMP_STRIDE // kg

    def page_copy(pp, bb, sl, j):
        pid = pt_ref[bb * n_pages + j]
        return pltpu.make_async_copy(cache_ref.at[layer, pid, pp], buf_ref.at[sl, j], sem.at[sl])

    def fetch(pp, bb, sl):
        def issue(j, c):
            page_copy(pp, bb, sl, j).start()
            return c

        lax.fori_loop(0, n_pages, issue, 0)

    @pl.when(step == 0)
    def _():
        fetch(p, b, slot)

    @pl.when(step + 1 < 2 * nb)
    def _():
        fetch((step + 1) // nb, (step + 1) % nb, 1 - slot)

    @pl.when(b == 0)
    def _():
        w1b_ref[...] = w1_ref[0].astype(BF16)
        bias = jnp.zeros(peb_ref.shape, F32)
        for m in range(CMP_SEGS):
            for q in range(n_q):
                pe_rows = jnp.broadcast_to(pe_ref[0, m, q], (SUBLANES, kg * HEAD_DIM)).astype(BF16)
                bias = bias + _dot(pe_rows, w1b_ref[m, q])
        peb_ref[...] = bias

    def drain(j, c):
        page_copy(p, b, slot, j).wait()
        return c

    lax.fori_loop(0, n_pages, drain, 0)
    cpp = page // CMP_STRIDE
    for gp in range(KV_HEADS // 2):
        def to_rows(j, c):
            x_t = buf_ref[slot, j, 2 * gp:2 * gp + 2].reshape(2 * HEAD_DIM, page).astype(BF16)
            x = _dot_nt(perm_ref[...], x_t)
            c0 = pl.multiple_of(j * cpp, cpp)
            for s in range(CMP_STRIDE):
                rows_ref[s, pl.ds(c0, cpp), :] = x[s * cpp:(s + 1) * cpp, :]
            return c

        lax.fori_loop(0, n_pages, to_rows, 0, unroll=8)
        for half in range(2):
            lanes = slice(half * HEAD_DIM, (half + 1) * HEAD_DIM)
            acc = [None] * CMP_SEGS
            for q in range(n_q):
                piece = jnp.concatenate([rows_ref[q * kg + k][:, lanes] for k in range(kg)], axis=1).astype(BF16)
                for m in range(CMP_SEGS):
                    t = _dot(piece, w1b_ref[m, q])
                    acc[m] = t if acc[m] is None else acc[m] + t
            o_ref[0, 0, 2 * gp + half], _ = _compress_tail(acc[0] + peb_ref[0:1, :], acc[1], w2t_ref)


def _compress_sample(page_table, cache_t, layer, w1, pe, w2t):
    nb, n_pages = page_table.shape
    page = cache_t.shape[-1]
    n_chunk = n_pages * page // CMP_STRIDE
    hid = w1.shape[-1]
    kg = CMP_K_GROUP
    w1g = w1.reshape(2, CMP_SEGS, CMP_STRIDE // kg, kg * HEAD_DIM, hid)
    peg = pe.reshape(2, CMP_SEGS, CMP_STRIDE // kg, 1, kg * HEAD_DIM)
    cpp = page // CMP_STRIDE
    row = jnp.arange(page)
    perm = ((CMP_STRIDE * (row % cpp) + row // cpp)[:, None] == jnp.arange(page)[None, :]).astype(BF16)
    assert page == LANES and KV_HEADS % 2 == 0
    grid_spec = pltpu.PrefetchScalarGridSpec(
        num_scalar_prefetch=1,
        grid=(2, nb),
        in_specs=[pl.BlockSpec(memory_space=pl.ANY),
                  pl.BlockSpec((1,) + w1g.shape[1:], lambda p, bi, pt: (p, 0, 0, 0, 0)),
                  pl.BlockSpec((1,) + peg.shape[1:], lambda p, bi, pt: (p, 0, 0, 0, 0)),
                  pl.BlockSpec((1, HEAD_DIM, hid), lambda p, bi, pt: (p, 0, 0)),
                  pl.BlockSpec((page, page), lambda p, bi, pt: (0, 0))],
        out_specs=pl.BlockSpec((1, 1, KV_HEADS, n_chunk, HEAD_DIM), lambda p, bi, pt: (p, bi, 0, 0, 0)),
        scratch_shapes=[pltpu.VMEM((2, n_pages, KV_HEADS, HEAD_DIM, page), F32),
                        pltpu.VMEM((CMP_STRIDE, n_chunk, 2 * HEAD_DIM), F32),
                        pltpu.VMEM(w1g.shape[1:], BF16),
                        pltpu.VMEM((SUBLANES, hid), F32),
                        pltpu.SemaphoreType.DMA((2,))],
    )
    return pl.pallas_call(
        functools.partial(_compress_sample_kernel, layer=layer, n_pages=n_pages, page=page, nb=nb),
        grid_spec=grid_spec,
        out_shape=jax.ShapeDtypeStruct((2, nb, KV_HEADS, n_chunk, HEAD_DIM), F32),
        compiler_params=_cparams("arbitrary", "arbitrary"),
        name="compress_sample",
    )(page_table.reshape(-1), cache_t, w1g, peg, w2t, perm)


def _attn_prompt_kernel(slope_ref, q_ref, kc_ref, vct_ref, ks_ref, vs_ref, kw_ref, vw_ref, gate_ref, covert_ref,
                        o_ref, m_ref, l_ref, acc_ref, st_ref, bias_ref, *, seq, n_cmp, n_sel, n_top, wlen):
    qb = Q_BLOCK
    tk = SEL_KV_TILE
    g = pl.program_id(1)
    p0 = pl.program_id(2) * qb
    qpos = p0 + lax.broadcasted_iota(I32, (1, qb), 1)
    slopes = [slope_ref[g * Q_PER_KV + h] for h in range(Q_PER_KV)]
    q_t = q_ref[...].astype(F32).T.astype(BF16)
    q_all = jnp.concatenate([q_t[h * HEAD_DIM:(h + 1) * HEAD_DIM] for h in range(Q_PER_KV)], axis=1)
    head = lambda a, h: a[:, h * qb:(h + 1) * qb]

    def col_softmax(z):
        m = jnp.maximum(jnp.max(z, axis=0, keepdims=True), NEG_CLAMP)
        e = jnp.exp(z - m)
        l = jnp.sum(e, axis=0, keepdims=True)
        return e, jnp.where(l > 0.0, l, 1.0)

    n_chunk = kc_ref.shape[3]
    crow = lax.broadcasted_iota(I32, (n_chunk, 1), 0)
    d_c = qpos - (crow * CMP_STRIDE + (CMP_LEN - 1))
    mask_c = jnp.where((d_c >= 0) & (crow < n_cmp), 0.0, NEG_INF)
    d_cf = d_c.astype(F32)
    vct = vct_ref[0, 0, 0].astype(BF16)
    s_c = _dot(kc_ref[0, 0, 0].astype(BF16), q_all)
    p_sum = jnp.zeros((n_chunk, qb), F32)
    o_c = []
    for h in range(Q_PER_KV):
        e, l = col_softmax(head(s_c, h) - slopes[h] * d_cf + mask_c)
        p = e / l
        p_sum = p_sum + p
        o_c.append(_dot(vct, p.astype(BF16)))

    p_hi, p_lo = _split_bf16(p_sum)
    cover_t = covert_ref[...]
    imp_t = _dot(cover_t, p_hi) + _dot(cover_t, p_lo)
    n_blk = -(-n_sel // SUBLANES) * SUBLANES
    blk = lax.broadcasted_iota(I32, (n_blk, 1), 0)
    score = _block_scores(imp_t[0:n_blk], blk, qpos, n_sel)
    st_ref[0:n_blk, :] = score

    def count_beaten(k, cnt):
        row = st_ref[pl.ds(k, 1), :]
        return cnt + jnp.where((row > score) | ((row == score) & (k < blk)), 1.0, 0.0)

    rank = lax.fori_loop(0, n_sel, count_beaten, jnp.zeros(score.shape, F32), unroll=4)
    st_ref[0:n_blk, :] = jnp.where(rank < n_top, 0.0, NEG_INF)

    krow = lax.broadcasted_iota(I32, (tk, 1), 0)
    rel = (krow - (qpos - p0)).astype(F32)
    for h in range(Q_PER_KV):
        bias_ref[h] = slopes[h] * rel
    m_ref[...] = jnp.full(m_ref.shape, NEG_INF, F32)
    l_ref[...] = jnp.zeros(l_ref.shape, F32)
    acc_ref[...] = jnp.zeros(acc_ref.shape, F32)
    n_tiles = (p0 + qb + tk - 1) // tk

    def sweep_tile(jt, causal):
        k0 = pl.multiple_of(jt * tk, tk)
        k_rows = ks_ref[0, 0, :, pl.ds(k0, tk)].T.astype(BF16)
        v_t = vs_ref[0, 0, :, pl.ds(k0, tk)].astype(BF16)
        s = _dot(k_rows, q_all)
        j0 = k0 // SEL_BLOCK
        madd = jnp.concatenate([jnp.broadcast_to(st_ref[pl.ds(j0 + jb, 1), :], (SEL_BLOCK, qb))
                                for jb in range(tk // SEL_BLOCK)], axis=0)
        if causal:
            madd = jnp.where(k0 + krow <= qpos, madd, NEG_INF)
        off = (k0 - p0).astype(F32)
        for h in range(Q_PER_KV):
            z = head(s, h) + (bias_ref[h] + madd)
            shift = slopes[h] * off
            m_old = m_ref[h]
            m_new = jnp.maximum(m_old, jnp.max(z, axis=0, keepdims=True) + shift)
            m_use = jnp.maximum(m_new, NEG_CLAMP)
            e = jnp.exp(z - (m_use - shift))
            alpha = jnp.exp(m_old - m_use)
            l_ref[h] = alpha * l_ref[h] + jnp.sum(e, axis=0, keepdims=True)
            acc_ref[h] = alpha * acc_ref[h] + _dot(v_t, e.astype(BF16))
            m_ref[h] = m_new

    def sweep_body(jt, carry):
        sweep_tile(jt, False)
        return carry

    lax.fori_loop(0, n_tiles - 1, sweep_body, 0)
    sweep_tile(n_tiles - 1, True)

    w0 = pl.multiple_of(jnp.minimum(jnp.maximum(p0 - WINDOW, 0), seq - wlen), qb)
    relw = (w0 + lax.broadcasted_iota(I32, (wlen, 1), 0)) - qpos
    mask_w = jnp.where((relw <= 0) & (relw > -WINDOW), 0.0, NEG_INF)
    relw_f = relw.astype(F32)
    kw_rows = kw_ref[0, 0, :, pl.ds(w0, wlen)].T.astype(BF16)
    vw_t = vw_ref[0, 0, :, pl.ds(w0, wlen)].astype(BF16)
    s_w = _dot(kw_rows, q_all)

    out = []
    for h in range(Q_PER_KV):
        e, l = col_softmax(head(s_w, h) + slopes[h] * relw_f + mask_w)
        o_w = _dot(vw_t, e.astype(BF16)) / l
        l_s = l_ref[h]
        o_s = acc_ref[h] / jnp.where(l_s > 0.0, l_s, 1.0)
        g0 = gate_ref[0, 0, 3 * h + 0:3 * h + 1, :]
        g1 = gate_ref[0, 0, 3 * h + 1:3 * h + 2, :]
        g2 = gate_ref[0, 0, 3 * h + 2:3 * h + 3, :]
        out.append(g0 * o_c[h] + g1 * o_s + g2 * o_w)
    o_ref[...] = jnp.concatenate(out, axis=0).T.astype(o_ref.dtype)


def _attn_prompt(slopes, q, batch, seq, cmp_rows, cmp_t, kv_t, win_t, gates_t, cover_t):
    n_chunk = cmp_rows.shape[3]
    n_cmp = (seq - CMP_LEN) // CMP_STRIDE + 1
    n_sel = -(-seq // SEL_BLOCK)
    n_top = min(SEL_TOPK, n_sel)
    wlen = min(WINDOW + Q_BLOCK, seq)
    nsp = cover_t.shape[0]
    nqb = seq // Q_BLOCK
    assert seq % SEL_KV_TILE == 0 and seq % Q_BLOCK == 0 and n_chunk >= n_cmp and wlen % LANES == 0

    def plane(pidx):
        return pl.BlockSpec((1, 1, HEAD_DIM, seq), lambda bi, gi, i: (bi, pidx, gi, 0))

    return pl.pallas_call(
        functools.partial(_attn_prompt_kernel, seq=seq, n_cmp=n_cmp, n_sel=n_sel, n_top=n_top, wlen=wlen),
        grid=(batch, KV_HEADS, nqb),
        in_specs=[pl.BlockSpec(memory_space=pltpu.SMEM),
                  pl.BlockSpec((Q_BLOCK, GRP_WIDTH), lambda bi, gi, i: (bi * nqb + i, gi)),
                  pl.BlockSpec((1, 1, 1, n_chunk, HEAD_DIM), lambda bi, gi, i: (0, bi, gi, 0, 0)),
                  pl.BlockSpec((1, 1, 1, HEAD_DIM, n_chunk), lambda bi, gi, i: (1, bi, gi, 0, 0)),
                  plane(2), plane(3), plane(0), plane(1),
                  pl.BlockSpec((1, 1, 3 * Q_PER_KV, Q_BLOCK), lambda bi, gi, i: (bi, gi, 0, i)),
                  pl.BlockSpec(cover_t.shape, lambda bi, gi, i: (0, 0))],
        out_specs=pl.BlockSpec((Q_BLOCK, GRP_WIDTH), lambda bi, gi, i: (bi * nqb + i, gi)),
        out_shape=jax.ShapeDtypeStruct((batch * seq, N_HEADS * HEAD_DIM), BF16),
        scratch_shapes=[pltpu.VMEM((Q_PER_KV, 1, Q_BLOCK), F32), pltpu.VMEM((Q_PER_KV, 1, Q_BLOCK), F32),
                        pltpu.VMEM((Q_PER_KV, HEAD_DIM, Q_BLOCK), F32), pltpu.VMEM((nsp, Q_BLOCK), F32),
                        pltpu.VMEM((Q_PER_KV, SEL_KV_TILE, Q_BLOCK), F32)],
        compiler_params=_cparams("arbitrary", "arbitrary", "arbitrary"),
        name="attn_prompt",
    )(slopes, q, cmp_rows, cmp_t, kv_t, kv_t, win_t, win_t, gates_t, cover_t)


def _attn_sample_kernel(pt_ref, qbd_ref, kc_ref, vc_ref, cache_ref, new_ref, win_ref, gate_ref, slope_ref,
                        qpos_ref, cover_ref, expand_ref, o_ref, kt_ref, vt_ref, sem,
                        *, layer, n_pages, page, past, ts, n_cmp, n_sel, n_top, win_buf, gt):
    b = pl.program_id(0)
    nk = kt_ref.shape[2]

    def page_copies(j):
        pid = pt_ref[b * n_pages + j]
        dst = pl.ds(pl.multiple_of(j * page, page), page)
        return (pltpu.make_async_copy(cache_ref.at[layer, pid, 2], kt_ref.at[:, :, dst], sem),
                pltpu.make_async_copy(cache_ref.at[layer, pid, 3], vt_ref.at[:, :, dst], sem))

    def issue(j, c):
        for cp in page_copies(j):
            cp.start()
        return c

    def drain(j, c):
        for cp in page_copies(j):
            cp.wait()
        return c

    lax.fori_loop(0, n_pages, issue, 0)
    kt_ref[:, :, past:] = new_ref[0, 0].reshape(KV_HEADS, HEAD_DIM, nk - past)
    vt_ref[:, :, past:] = new_ref[0, 1].reshape(KV_HEADS, HEAD_DIM, nk - past)

    q = qbd_ref[0]
    nrow = q.shape[0]
    slope = slope_ref[...]
    qpos = qpos_ref[...]
    lane_grp = lax.broadcasted_iota(I32, (nrow, GRP_WIDTH), 1) // HEAD_DIM
    row_grp = (lax.broadcasted_iota(I32, (nrow, GRP_WIDTH), 0) % gt) // (gt // KV_HEADS)
    diag = lane_grp == row_grp

    def own_group(full):
        kept = jnp.where(diag, full, 0.0)
        out = kept[:, 0:HEAD_DIM]
        for g in range(1, KV_HEADS):
            out = out + kept[:, g * HEAD_DIM:(g + 1) * HEAD_DIM]
        return out

    n_chunk = kc_ref.shape[1]
    ccol = lax.broadcasted_iota(I32, (1, n_chunk), 1)
    d_c = qpos - (ccol * CMP_STRIDE + (CMP_LEN - 1))
    s_c = _dot_nt(q, kc_ref[0].astype(BF16)) - slope * d_c.astype(F32)
    p_c = _masked_softmax(s_c, (d_c >= 0) & (ccol < n_cmp))
    o_c = own_group(_dot(p_c.astype(BF16), vc_ref[0].astype(BF16)))

    p_sum = p_c[0:gt]
    for h in range(1, Q_PER_KV):
        p_sum = p_sum + p_c[h * gt:(h + 1) * gt]
    p_hi, p_lo = _split_bf16(p_sum)
    imp = _dot(p_hi, cover_ref[...]) + _dot(p_lo, cover_ref[...])
    scol = lax.broadcasted_iota(I32, (1, imp.shape[1]), 1)
    selmask = _top_k_mask(_block_scores(imp, scol, qpos[0:gt], n_sel), n_top).astype(BF16)

    n_newl = new_ref.shape[3]
    wcol = lax.broadcasted_iota(I32, (1, win_buf + n_newl), 1)
    kpos_w = jnp.where(wcol < win_buf, past - win_buf + wcol, past + wcol - win_buf)
    d_w = qpos - kpos_w
    valid_w = (d_w >= 0) & (d_w < WINDOW) & (wcol < win_buf + ts)
    kw_t = win_ref[0, 0, 0].reshape(GRP_WIDTH, win_buf).astype(BF16)
    vw_t = win_ref[0, 0, 1].reshape(GRP_WIDTH, win_buf).astype(BF16)
    s_w = jnp.concatenate([_dot(q, kw_t), _dot(q, new_ref[0, 2].astype(BF16))], axis=1)
    p_w = _masked_softmax(s_w - slope * d_w.astype(F32), valid_w).astype(BF16)
    o_w = own_group(_dot_nt(p_w[:, :win_buf], vw_t) + _dot_nt(p_w[:, win_buf:], new_ref[0, 3].astype(BF16)))

    lax.fori_loop(0, n_pages, drain, 0)
    selk = _dot(selmask, expand_ref[...])
    selk = jnp.concatenate([selk] * Q_PER_KV, axis=0)
    d_s = qpos - lax.broadcasted_iota(I32, (1, nk), 1)
    k_t = kt_ref[...].reshape(GRP_WIDTH, nk).astype(BF16)
    s_s = _dot(q, k_t) - slope * d_s.astype(F32)
    p_s = _masked_softmax(s_s, (selk > 0.5) & (d_s >= 0))
    v_t = vt_ref[...].reshape(GRP_WIDTH, nk).astype(BF16)
    o_s = own_group(_dot_nt(p_s.astype(BF16), v_t))

    gate = gate_ref[0]
    o_ref[0] = gate[:, 0:1] * o_c + gate[:, 1:2] * o_s + gate[:, 2:3] * o_w


def _attn_sample(page_table, qbd, kc_all, vc_all, cache_t, new_t, win_t, gates, slope_rows, qpos_rows,
                 cover, expand, *, layer, past, ts, n_cmp, n_sel, gt):
    nb, n_pages = page_table.shape
    page = cache_t.shape[-1]
    win_buf = win_t.shape[-1]
    nrow = qbd.shape[1]
    nk = expand.shape[1]
    n_top = min(SEL_TOPK, n_sel)
    assert nk == past + new_t.shape[3] and win_buf % LANES == 0

    def per_seq(shape):
        return pl.BlockSpec((1,) + tuple(shape[1:]), lambda bi, pt: (bi,) + (0,) * (len(shape) - 1))

    def whole(shape):
        return pl.BlockSpec(tuple(shape), lambda bi, pt: (0,) * len(shape))

    grid_spec = pltpu.PrefetchScalarGridSpec(
        num_scalar_prefetch=1,
        grid=(nb,),
        in_specs=[per_seq(qbd.shape), per_seq(kc_all.shape), per_seq(vc_all.shape),
                  pl.BlockSpec(memory_space=pl.ANY), per_seq(new_t.shape),
                  pl.BlockSpec((1, 1) + tuple(win_t.shape[2:]), lambda bi, pt: (layer, bi, 0, 0, 0, 0)),
                  per_seq(gates.shape), whole(slope_rows.shape), whole(qpos_rows.shape),
                  whole(cover.shape), whole(expand.shape)],
        out_specs=pl.BlockSpec((1, nrow, HEAD_DIM), lambda bi, pt: (bi, 0, 0)),
        scratch_shapes=[pltpu.VMEM((KV_HEADS, HEAD_DIM, nk), F32), pltpu.VMEM((KV_HEADS, HEAD_DIM, nk), F32),
                        pltpu.SemaphoreType.DMA(())],
    )
    return pl.pallas_call(
        functools.partial(_attn_sample_kernel, layer=layer, n_pages=n_pages, page=page, past=past, ts=ts,
                          n_cmp=n_cmp, n_sel=n_sel, n_top=n_top, win_buf=win_buf, gt=gt),
        grid_spec=grid_spec,
        out_shape=jax.ShapeDtypeStruct((nb, nrow, HEAD_DIM), F32),
        compiler_params=_cparams("arbitrary"),
        name="attn_sample",
    )(page_table.reshape(-1), qbd, kc_all, vc_all, cache_t, new_t, win_t, gates, slope_rows, qpos_rows,
      cover, expand)


def _merge_kernel(u_ref, yp_ref, ya_ref, wgp_ref, wga_ref, wp_ref, wa_ref, o_ref, wgpb_ref, wgab_ref, wpb_ref,
                  wab_ref):
    @pl.when(pl.program_id(1) == 0)
    def _():
        wgpb_ref[...] = wgp_ref[0].astype(BF16)
        wgab_ref[...] = wga_ref[0].astype(BF16)
        wpb_ref[...] = wp_ref[0].astype(BF16)
        wab_ref[...] = wa_ref[0].astype(BF16)

    u = u_ref[...]
    g_pool = jax.nn.sigmoid(_dot_nt(u, wgpb_ref[...]))
    g_attn = jax.nn.sigmoid(_dot_nt(u, wgab_ref[...]))
    merged = g_pool * _dot(yp_ref[...], wpb_ref[...]) + g_attn * _dot(ya_ref[...], wab_ref[...])
    o_ref[...] = merged.astype(o_ref.dtype)


def _merge(u, y_pool, y_attn, w_gm_t, w_up_pool, w_up_nsa, layer):
    m, d = u.shape
    kp = y_pool.shape[1]
    ka = y_attn.shape[1]
    tn = MM_TILE_N
    nn = d // tn
    row = lambda w: pl.BlockSpec((MM_TILE_M, w), lambda n, i: (i, 0))
    return pl.pallas_call(
        _merge_kernel,
        grid=(nn, m // MM_TILE_M),
        in_specs=[row(d), row(kp), row(ka),
                  pl.BlockSpec((1, tn, d), lambda n, i: (layer, n, 0)),
                  pl.BlockSpec((1, tn, d), lambda n, i: (layer, nn + n, 0)),
                  pl.BlockSpec((1, kp, tn), lambda n, i: (layer, 0, n)),
                  pl.BlockSpec((1, ka, tn), lambda n, i: (layer, 0, n))],
        out_specs=pl.BlockSpec((MM_TILE_M, tn), lambda n, i: (i, n)),
        out_shape=jax.ShapeDtypeStruct((m, d), BF16),
        scratch_shapes=[pltpu.VMEM((tn, d), BF16), pltpu.VMEM((tn, d), BF16),
                        pltpu.VMEM((kp, tn), BF16), pltpu.VMEM((ka, tn), BF16)],
        compiler_params=_cparams("arbitrary", "arbitrary"),
        name="merge",
    )(u, y_pool, y_attn, w_gm_t, w_gm_t, w_up_pool, w_up_nsa)


def _route(logits):
    lane = lax.broadcasted_iota(I32, logits.shape, 1)
    is_grp = lane < N_GROUPS
    p_grp = _masked_softmax(logits, is_grp)
    p_top = jnp.max(p_grp, axis=-1, keepdims=True)
    grp = jnp.min(jnp.where(is_grp & (p_grp == p_top), lane, LANES), axis=-1, keepdims=True)
    first = N_GROUPS + grp * EXPERTS_PER_GROUP
    in_grp = (lane >= first) & (lane < first + EXPERTS_PER_GROUP)
    v = _masked_softmax(logits, in_grp)
    v0 = jnp.max(jnp.where(in_grp, v, -1.0), axis=-1, keepdims=True)
    i0 = jnp.min(jnp.where(in_grp & (v == v0), lane, LANES), axis=-1, keepdims=True)
    rest = in_grp & (lane != i0)
    v1 = jnp.max(jnp.where(rest, v, -1.0), axis=-1, keepdims=True)
    i1 = jnp.min(jnp.where(rest & (v == v1), lane, LANES), axis=-1, keepdims=True)
    tot = v0 + v1
    out = jnp.where(lane == 0, (i0 - N_GROUPS).astype(F32), 0.0)
    out = jnp.where(lane == 1, (i1 - N_GROUPS).astype(F32), out)
    out = jnp.where(lane == 2, p_top * (v0 / tot), out)
    out = jnp.where(lane == 3, p_top * (v1 / tot), out)
    return out


def _outproj_kernel(mg_ref, x_ref, w_ref, mp_ref, ms_ref, g_ref, b_ref, wr_ref, br_ref,
                    x1_ref, u_ref, rt_ref, *, n_ptiles, alpha):
    is_s = pl.program_id(0) >= n_ptiles
    gate1 = _mod_row(is_s, mp_ref, ms_ref, 0)
    shift2 = _mod_row(is_s, mp_ref, ms_ref, 1)
    scale2 = _mod_row(is_s, mp_ref, ms_ref, 2)
    mix = _dot(mg_ref[...], w_ref[...])
    x1 = _layer_norm(alpha * x_ref[...] + gate1 * mix, g_ref[...], b_ref[...])
    x1_ref[...] = x1
    u = x1 * (1.0 + scale2) + shift2
    u_ref[...] = u
    wr_hi, wr_lo = _split_bf16(wr_ref[...])
    u_hi, u_lo = _split_bf16(u)
    logits = _dot_nt(u_hi, wr_hi) + _dot_nt(u_lo, wr_hi) + _dot_nt(u_hi, wr_lo) + br_ref[...]
    rt_ref[...] = _route(logits)


def _outproj(rows, merged, x, w_out_bf, modp, mods, ln_g, ln_b, w_router_t, b_router, alpha):
    d = x.shape[1]
    row = pl.BlockSpec((ROW_TILE, d), lambda i: (i, 0))
    vec = pl.BlockSpec((1, d), lambda i: (0, 0))
    return pl.pallas_call(
        functools.partial(_outproj_kernel, n_ptiles=rows.n_ptiles, alpha=alpha),
        grid=(rows.n_rows // ROW_TILE,),
        in_specs=[row, row, pl.BlockSpec((d, d), lambda i: (0, 0))] + rows.mod_specs(d, 1)
        + [vec, vec, pl.BlockSpec((LANES, d), lambda i: (0, 0)), pl.BlockSpec((1, LANES), lambda i: (0, 0))],
        out_specs=[row, row, pl.BlockSpec((ROW_TILE, LANES), lambda i: (i, 0))],
        out_shape=[jax.ShapeDtypeStruct((rows.n_rows, d), F32),
                   jax.ShapeDtypeStruct((rows.n_rows, d), F32),
                   jax.ShapeDtypeStruct((rows.n_rows, LANES), F32)],
        compiler_params=_cparams("arbitrary"),
        name="outproj_ln_route",
    )(merged, x, w_out_bf, modp, mods, ln_g.reshape(1, d), ln_b.reshape(1, d), w_router_t, b_router)


def _moe_scatter_kernel(dest_ref, pad_ref, u_ref, xs_ref, zero_ref, sem, *, n_real):
    i = pl.program_id(0)
    tm = u_ref.shape[0]

    def pad_copy(r):
        return pltpu.make_async_copy(zero_ref, xs_ref.at[pl.ds(r, 1)], sem)

    @pl.when(i == 0)
    def _():
        zero_ref[...] = jnp.zeros(zero_ref.shape, zero_ref.dtype)
        for e in range(N_EXPERTS + 1):
            lo, hi = pad_ref[2 * e], pad_ref[2 * e + 1]

            def issue_pad(r, c):
                pad_copy(r).start()
                return c

            def drain_pad(r, c):
                pad_copy(r).wait()
                return c

            lax.fori_loop(lo, hi, issue_pad, 0)
            lax.fori_loop(lo, hi, drain_pad, 0)

    def row_copy(r, dst):
        return pltpu.make_async_copy(u_ref.at[pl.ds(r, 1)], xs_ref.at[pl.ds(dst, 1)], sem)

    def issue(r, c):
        base = 2 * (i * tm + r)
        row_copy(r, dest_ref[base]).start()
        row_copy(r, dest_ref[base + 1]).start()
        return c

    def drain(r, c):
        row_copy(r, 0).wait()
        row_copy(r, 0).wait()
        return c

    n_here = jnp.minimum(tm, n_real - i * tm)
    lax.fori_loop(0, n_here, issue, 0)
    lax.fori_loop(0, n_here, drain, 0)


def _moe_scatter(dest, pads, u, n_real, r_pad):
    d = u.shape[1]
    tm = EXPERT_TILE
    grid_spec = pltpu.PrefetchScalarGridSpec(
        num_scalar_prefetch=2,
        grid=(-(-n_real // tm),),
        in_specs=[pl.BlockSpec((tm, d), lambda i, *_: (i, 0))],
        out_specs=pl.BlockSpec(memory_space=pl.ANY),
        scratch_shapes=[pltpu.VMEM((1, d), F32), pltpu.SemaphoreType.DMA(())],
    )
    return pl.pallas_call(
        functools.partial(_moe_scatter_kernel, n_real=n_real),
        grid_spec=grid_spec,
        out_shape=jax.ShapeDtypeStruct((r_pad, d), F32),
        compiler_params=_cparams("arbitrary"),
        name="moe_scatter",
    )(dest, pads, u)


def _moe_expert_kernel(te_ref, used_ref, x_ref, wg_ref, wu_ref, wd_ref, o_ref, wgb_ref, wub_ref, wdb_ref):
    t = pl.program_id(0)
    fresh = jnp.logical_or(t == 0, te_ref[t] != te_ref[jnp.maximum(t - 1, 0)])

    @pl.when(fresh)
    def _():
        wgb_ref[...] = wg_ref[0, 0].astype(BF16)
        wub_ref[...] = wu_ref[0, 0].astype(BF16)
        wdb_ref[...] = wd_ref[0, 0].astype(BF16)

    @pl.when(t < used_ref[0])
    def _():
        x = x_ref[...].astype(BF16)
        h = jax.nn.silu(_dot(x, wgb_ref[...])) * _dot(x, wub_ref[...])
        o_ref[...] = _dot(h.astype(BF16), wdb_ref[...])

    @pl.when(t >= used_ref[0])
    def _():
        o_ref[...] = jnp.zeros(o_ref.shape, o_ref.dtype)


def _moe_experts(tile_expert, n_used, xs, w_gate, w_up, w_down, layer):
    r_pad, d = xs.shape
    f = w_gate.shape[3]
    tm = EXPERT_TILE
    n_tiles = r_pad // tm

    def x_map(t, te, nu):
        return (jnp.minimum(t, jnp.maximum(nu[0] - 1, 0)), 0)

    grid_spec = pltpu.PrefetchScalarGridSpec(
        num_scalar_prefetch=2,
        grid=(n_tiles,),
        in_specs=[pl.BlockSpec((tm, d), x_map),
                  pl.BlockSpec((1, 1, d, f), lambda t, te, nu: (layer, te[t], 0, 0)),
                  pl.BlockSpec((1, 1, d, f), lambda t, te, nu: (layer, te[t], 0, 0)),
                  pl.BlockSpec((1, 1, f, d), lambda t, te, nu: (layer, te[t], 0, 0))],
        out_specs=pl.BlockSpec((tm, d), lambda t, te, nu: (t, 0)),
        scratch_shapes=[pltpu.VMEM((d, f), BF16), pltpu.VMEM((d, f), BF16), pltpu.VMEM((f, d), BF16)],
    )
    return pl.pallas_call(
        _moe_expert_kernel,
        grid_spec=grid_spec,
        out_shape=jax.ShapeDtypeStruct((r_pad, d), F32),
        compiler_params=_cparams("arbitrary"),
        name="moe_experts",
    )(tile_expert, n_used, xs, w_gate, w_up, w_down)


def _moe_combine_kernel(pos_ref, y_ref, x1_ref, rt_ref, mp_ref, ms_ref, g_ref, b_ref, x2_ref, *rest,
                        n_ptiles, alpha, emit_next):
    if emit_next:
        u_ref, buf_ref, sem = rest
    else:
        buf_ref, sem = rest
    i = pl.program_id(0)
    tm = x1_ref.shape[0]

    def row_copy(r, k, src):
        return pltpu.make_async_copy(y_ref.at[pl.ds(src, 1)], buf_ref.at[k, pl.ds(r, 1)], sem)

    def issue(r, c):
        base = 2 * (i * tm + r)
        row_copy(r, 0, pos_ref[base]).start()
        row_copy(r, 1, pos_ref[base + 1]).start()
        return c

    def drain(r, c):
        row_copy(r, 0, 0).wait()
        row_copy(r, 1, 0).wait()
        return c

    lax.fori_loop(0, tm, issue, 0)
    lax.fori_loop(0, tm, drain, 0)
    is_s = i >= n_ptiles
    gate2 = _mod_row(is_s, mp_ref, ms_ref, 0)
    ffn = rt_ref[:, 2:3] * buf_ref[0] + rt_ref[:, 3:4] * buf_ref[1]
    x2 = _layer_norm(alpha * x1_ref[...] + gate2 * ffn, g_ref[...], b_ref[...])
    x2_ref[...] = x2
    if emit_next:
        shift = _mod_row(is_s, mp_ref, ms_ref, 1)
        scale = _mod_row(is_s, mp_ref, ms_ref, 2)
        u_ref[...] = (x2 * (1.0 + scale) + shift).astype(u_ref.dtype)


def _moe_combine(rows, pos, y_rows, x1, route, modp, mods, ln_g, ln_b, alpha, emit_next):
    d = x1.shape[1]
    row = pl.BlockSpec((ROW_TILE, d), lambda i, *_: (i, 0))
    vec = pl.BlockSpec((1, d), lambda i, *_: (0, 0))
    out_specs = [row]
    out_shape = [jax.ShapeDtypeStruct((rows.n_rows, d), F32)]
    if emit_next:
        out_specs.append(row)
        out_shape.append(jax.ShapeDtypeStruct((rows.n_rows, d), BF16))
    grid_spec = pltpu.PrefetchScalarGridSpec(
        num_scalar_prefetch=1,
        grid=(rows.n_rows // ROW_TILE,),
        in_specs=[pl.BlockSpec(memory_space=pl.ANY), row, pl.BlockSpec((ROW_TILE, LANES), lambda i, *_: (i, 0))]
        + rows.mod_specs(d, 2) + [vec, vec],
        out_specs=out_specs,
        scratch_shapes=[pltpu.VMEM((2, ROW_TILE, d), F32), pltpu.SemaphoreType.DMA(())],
    )
    out = pl.pallas_call(
        functools.partial(_moe_combine_kernel, n_ptiles=rows.n_ptiles, alpha=alpha, emit_next=emit_next),
        grid_spec=grid_spec,
        out_shape=out_shape,
        compiler_params=_cparams("arbitrary"),
        name="moe_combine_ln",
    )(pos, y_rows, x1, route, modp, mods, ln_g.reshape(1, d), ln_b.reshape(1, d))
    return out if emit_next else (out[0], None)


def _moe_dispatch(route, n_real, n_rows):
    tm = EXPERT_TILE
    n_pairs = 2 * n_real
    n_tiles = -(-(n_pairs + N_EXPERTS * (tm - 1)) // tm)
    eid = route[:n_real, 0:2].astype(I32).reshape(n_pairs, 1)
    onehot = (eid == jnp.arange(N_EXPERTS, dtype=I32)[None, :]).astype(I32)
    seen = jnp.cumsum(onehot, axis=0)
    counts = seen[-1]
    padded = -(-counts // tm) * tm
    ends_pad = jnp.cumsum(padded)
    starts_pad = ends_pad - padded
    dest = jnp.sum(onehot * (starts_pad[None, :] + seen - 1), axis=1).astype(I32)
    pos = jnp.concatenate([dest, jnp.zeros((2 * (n_rows - n_real),), I32)])
    pads = jnp.stack([starts_pad + counts, ends_pad], axis=1).reshape(-1)
    pads = jnp.concatenate([pads, ends_pad[-1:], jnp.full((1,), n_tiles * tm)]).astype(I32)
    n_used = (ends_pad[-1] // tm).astype(I32)
    tile_start = jnp.arange(n_tiles, dtype=I32) * tm
    tile_e = jnp.minimum(jnp.sum((tile_start[:, None] >= ends_pad[None, :]).astype(I32), axis=1), N_EXPERTS - 1)
    last_e = jnp.sum(jnp.where(jnp.arange(n_tiles) == jnp.maximum(n_used - 1, 0), tile_e, 0))
    tile_e = jnp.where(jnp.arange(n_tiles) < n_used, tile_e, last_e).astype(I32)
    return pos, pads, tile_e, n_used.reshape(1), n_tiles * tm


def _cover_matrix(n_chunk, n_cmp, n_sel, n_sel_pad):
    ci = jnp.arange(n_chunk)[:, None]
    sj = jnp.arange(n_sel_pad)[None, :]
    hit = ((ci * CMP_STRIDE < sj * SEL_BLOCK + SEL_BLOCK)
           & (ci * CMP_STRIDE + CMP_LEN - 1 >= sj * SEL_BLOCK)
           & (ci < n_cmp) & (sj < n_sel))
    return hit.astype(BF16)


def kernel(x_prompt, x_sample, c_prompt, c_sample, cache_kv, cache_win, state_pool, page_table, w_ada, b_ada,
           w_in, cmp_w1, cmp_pe, cmp_w2, pool_w, pool_scale, w_up_pool, w_up_nsa, w_out, ln1_g, ln1_b, w_rg,
           b_rg, w_re, b_re, w_gate, w_up, w_down, ln2_g, ln2_b):
    batch, seq, d = x_prompt.shape
    nb, ts, _ = x_sample.shape
    n_layers = w_ada.shape[0]
    page = cache_kv.shape[2]
    n_pages = page_table.shape[1]
    past = n_pages * page
    win_buf = cache_win.shape[2]
    pool_width = pool_w.shape[1] * pool_w.shape[2]
    q_width = N_HEADS * HEAD_DIM
    kv_width = N_KV_PLANES * GRP_WIDTH
    gn_width = 3 * N_HEADS
    alpha = (2 * n_layers) ** 0.25
    cmp_hid = cmp_w1.shape[-1]
    chunk_w = CMP_STRIDE * HEAD_DIM

    rows = _Rows(batch, seq, nb * ts)
    n_p, n_s, n_rows = rows.n_prompt, rows.n_sample, rows.n_rows
    n_real = n_p + n_s

    x = jnp.concatenate([x_prompt.reshape(n_p, d), x_sample.reshape(n_s, d),
                         jnp.zeros((rows.sample_pad - n_s, d), F32)], axis=0)

    n_seq = batch + nb
    c_all = jnp.concatenate([c_prompt, c_sample, jnp.zeros((-n_seq % SUBLANES, d), F32)], axis=0)
    mod = _adaln(c_all, w_ada, b_ada)[:, :n_seq].reshape(n_layers, n_seq, 6, d)
    nxt = jnp.concatenate([mod[1:], jnp.zeros_like(mod[:1])], axis=0)
    mod9 = jnp.stack([mod[:, :, 0], mod[:, :, 1], jnp.zeros_like(mod[:, :, 0]),
                      mod[:, :, 2], mod[:, :, 3], mod[:, :, 4],
                      mod[:, :, 5], nxt[:, :, 0], nxt[:, :, 1]], axis=2).reshape(n_layers, n_seq, 3, 3, d)
    modp_all = jnp.pad(mod9[:, :batch], ((0, 0), (0, 0), (0, 0), (0, SUBLANES - 3), (0, 0)))
    mods_all = jnp.repeat(mod9[:, batch:], ts, axis=1).transpose(0, 2, 3, 1, 4)
    mods_all = jnp.pad(mods_all, ((0, 0), (0, 0), (0, 0), (0, rows.sample_pad - n_s), (0, 0)))

    w_in_t = w_in.transpose(0, 2, 1)
    o1 = pool_width
    o2 = o1 + q_width
    o3 = o2 + kv_width
    o4 = o3 + gn_width
    w_gm_t = w_in_t[:, o4:]
    w_gn_t = jnp.pad(w_in_t[:, o3:o4], ((0, 0), (0, -gn_width % LANES), (0, 0)))
    w_out_bf = w_out.astype(BF16)
    n_rt = N_GROUPS + N_EXPERTS
    w_router_t = jnp.concatenate([w_rg.transpose(0, 2, 1), w_re.transpose(0, 2, 1),
                                  jnp.zeros((n_layers, LANES - n_rt, d), F32)], axis=1)
    b_router = jnp.concatenate([b_rg, b_re, jnp.zeros((n_layers, LANES - n_rt), F32)], axis=1)
    cache_t = cache_kv.transpose(0, 1, 3, 4, 5, 2)
    win_t = cache_win.transpose(0, 1, 3, 4, 5, 2)
    state_pm = state_pool.transpose(0, 2, 1, 3)
    w2_t = cmp_w2.transpose(0, 1, 3, 2)

    slopes = jnp.exp2(-8.0 * jnp.arange(1, N_HEADS + 1, dtype=F32) / N_HEADS)
    nch_p = seq // CMP_STRIDE
    ncmp_p = (seq - CMP_LEN) // CMP_STRIDE + 1
    nsel_p = -(-seq // SEL_BLOCK)
    cover_p_t = _cover_matrix(nch_p, ncmp_p, nsel_p, -(-nsel_p // LANES) * LANES).T

    tk_s = past + ts
    ncmp_s = (tk_s - CMP_LEN) // CMP_STRIDE + 1
    nch_s = past // CMP_STRIDE
    assert (ncmp_s + CMP_SEGS - 1) * CMP_STRIDE <= past and nch_s >= ncmp_s
    assert page % CMP_STRIDE == 0 and page % SEL_BLOCK == 0 and page % LANES == 0 and ts <= LANES
    nsel_s = -(-tk_s // SEL_BLOCK)
    nsel_s_pad = -(-nsel_s // LANES) * LANES
    cover_s = _cover_matrix(nch_s, ncmp_s, nsel_s, nsel_s_pad)
    nk_s = past + LANES
    expand_s = (jnp.arange(nsel_s_pad)[:, None] == jnp.arange(nk_s)[None, :] // SEL_BLOCK).astype(BF16)
    gt = KV_HEADS * ts
    slopes_gh = slopes.reshape(KV_HEADS, Q_PER_KV)
    slope_s = jnp.broadcast_to(slopes_gh.T[:, :, None], (Q_PER_KV, KV_HEADS, ts)).reshape(Q_PER_KV * gt, 1)
    qpos_s = jnp.broadcast_to(past + jnp.arange(ts, dtype=I32), (Q_PER_KV, KV_HEADS, ts)).reshape(Q_PER_KV * gt, 1)
    eye_g = jnp.eye(KV_HEADS, dtype=BF16)

    u = _modulate(rows, x, modp_all[0], mods_all[0])
    outs = {k: [] for k in ("kv_p", "win_p", "pool_p", "kv_s", "win_s", "pool_s")}
    for l in range(n_layers):
        up = _proj_rows(u, w_in_t, l, 0, pool_width, MM_TILE_N, F32)
        q = _proj_rows(u, w_in_t, l, o1, q_width, MM_TILE_N, BF16, scale=HEAD_DIM ** -0.5)
        kvc = _proj_rows(u, w_in_t, l, o2, 2 * GRP_WIDTH, MM_TILE_N, F32)
        kv_rest_s = _proj_rows(u, w_in_t, l, o2 + 2 * GRP_WIDTH, 4 * GRP_WIDTH, MM_TILE_N, F32,
                               m0=n_p, m_rows=rows.sample_pad)
        gn = _proj_rows(u, w_gn_t, l, 0, LANES, LANES, F32, act="sigmoid")
        kv_t = _proj_t(u, w_in_t, l, o2, N_CACHED_PLANES, batch, seq)
        kw_t = _proj_t(u, w_in_t, l, o2 + N_CACHED_PLANES * GRP_WIDTH, 2, batch, seq)

        kv_s = jnp.concatenate([kvc[n_p:n_real], kv_rest_s[:n_s]], axis=1)
        kv_s = kv_s.reshape(nb, ts, N_KV_PLANES, KV_HEADS, HEAD_DIM)
        up_p = up[:n_p].reshape(batch, seq, pool_width)
        up_s = up[n_p:n_real].reshape(nb, ts, pool_width)
        outs["kv_p"].append(kv_t.reshape(batch, N_CACHED_PLANES, KV_HEADS, HEAD_DIM, seq))
        n_keep = min(WINDOW, seq)
        outs["win_p"].append(kw_t[:, :, :, seq - n_keep:].reshape(batch, 2, KV_HEADS, HEAD_DIM, n_keep))
        hist_p = jnp.concatenate([jnp.zeros((batch, POOL_HIST, pool_width), F32), up_p[:, -POOL_HIST:]], axis=1)
        outs["pool_p"].append(hist_p[:, -POOL_HIST:].transpose(1, 0, 2))
        new_t = kv_s.transpose(0, 2, 3, 4, 1)
        win_new = jnp.concatenate([win_t[l], new_t[:, N_CACHED_PLANES:]], axis=-1)[..., -win_buf:]
        outs["kv_s"].append(kv_s[:, :, :N_CACHED_PLANES])
        outs["win_s"].append(win_new)
        pool_ext = jnp.concatenate([state_pm[l], up_s.transpose(1, 0, 2)], axis=0)
        outs["pool_s"].append(pool_ext[-POOL_HIST:])

        yp_p = _pool_prompt(up, batch, seq, pool_w[l], pool_scale[l])
        yp_s = _pool_sample(pool_ext, ts, past, pool_w[l], pool_scale[l])
        y_pool = jnp.concatenate([yp_p, yp_s.transpose(1, 0, 2).reshape(n_s, pool_width),
                                  jnp.zeros((rows.sample_pad - n_s, pool_width), BF16)], axis=0)

        w1 = cmp_w1[l].reshape(2, CMP_SEGS, chunk_w, cmp_hid)
        pe = cmp_pe[l].reshape(2, CMP_SEGS, 1, chunk_w)
        chunks_p = kvc[:n_p].reshape(batch, nch_p, CMP_STRIDE, 2, KV_HEADS, HEAD_DIM)
        chunks_p = chunks_p.transpose(3, 0, 4, 1, 2, 5).reshape(2, batch, KV_HEADS, nch_p, chunk_w)
        cmp_rows_p, cmp_t_p = _compress_prompt(chunks_p, w1, pe, w2_t[l])
        cmp_s = _compress_sample(page_table, cache_t, l, w1, pe, w2_t[l])

        gates_p = gn[:n_p, :gn_width].reshape(batch, seq, KV_HEADS, 3 * Q_PER_KV).transpose(0, 2, 3, 1)
        ya_p = _attn_prompt(slopes, q, batch, seq, cmp_rows_p, cmp_t_p, kv_t, kw_t, gates_p, cover_p_t)

        q_s = q[n_p:n_real].reshape(nb, ts, KV_HEADS, Q_PER_KV, HEAD_DIM).transpose(0, 3, 2, 1, 4)
        qbd = (q_s[:, :, :, :, None, :] * eye_g[None, None, :, None, :, None]).reshape(nb, Q_PER_KV * gt, GRP_WIDTH)
        kvc_all = cmp_s.transpose(0, 1, 3, 2, 4).reshape(2, nb, nch_s, GRP_WIDTH)
        new_blk = jnp.pad(new_t[:, 2:].reshape(nb, 4, GRP_WIDTH, ts), ((0, 0), (0, 0), (0, 0), (0, LANES - ts)))
        gates_s = gn[n_p:n_real, :gn_width].reshape(nb, ts, KV_HEADS, Q_PER_KV, 3).transpose(0, 3, 2, 1, 4)
        gates_s = gates_s.reshape(nb, Q_PER_KV * gt, 3)
        oa_s = _attn_sample(page_table, qbd, kvc_all[0], kvc_all[1], cache_t, new_blk, win_t, gates_s,
                            slope_s, qpos_s, cover_s, expand_s, layer=l, past=past, ts=ts, n_cmp=ncmp_s,
                            n_sel=nsel_s, gt=gt)
        ya_s = oa_s.reshape(nb, Q_PER_KV, KV_HEADS, ts, HEAD_DIM).transpose(0, 3, 2, 1, 4).reshape(n_s, q_width)
        y_attn = jnp.concatenate([ya_p, ya_s.astype(BF16),
                                  jnp.zeros((rows.sample_pad - n_s, q_width), BF16)], axis=0)

        merged = _merge(u, y_pool, y_attn, w_gm_t, w_up_pool, w_up_nsa, l)
        x1, u2, route = _outproj(rows, merged, x, w_out_bf[l], modp_all[l], mods_all[l], ln1_g[l], ln1_b[l],
                                 w_router_t[l], b_router[l].reshape(1, LANES), alpha)

        pos, pads, tile_e, n_used, r_pad = _moe_dispatch(route, n_real, n_rows)
        xs = _moe_scatter(pos, pads, u2, n_real, r_pad)
        y_rows = _moe_experts(tile_e, n_used, xs, w_gate, w_up, w_down, l)
        x, u = _moe_combine(rows, pos, y_rows, x1, route, modp_all[l], mods_all[l], ln2_g[l], ln2_b[l], alpha,
                            emit_next=l + 1 < n_layers)

    y_prompt = x[:n_p].reshape(batch, seq, d)
    y_sample = x[n_p:n_real].reshape(nb, ts, d)
    new_kv_p = jnp.stack(outs["kv_p"]).transpose(0, 1, 5, 2, 3, 4)
    new_win_p = jnp.stack(outs["win_p"]).transpose(0, 1, 5, 2, 3, 4)
    new_pool_p = jnp.stack(outs["pool_p"]).transpose(0, 2, 1, 3)
    new_win_s = jnp.stack(outs["win_s"]).transpose(0, 1, 5, 2, 3, 4)
    new_pool_s = jnp.stack(outs["pool_s"]).transpose(0, 2, 1, 3)
    return (y_prompt, y_sample, new_kv_p, new_win_p, new_pool_p, jnp.stack(outs["kv_s"]), new_win_s, new_pool_s)
```

```python
import functools

import jax
import jax.numpy as jnp
from jax import lax
from jax.experimental import pallas as pl
from jax.experimental.pallas import tpu as pltpu

F32 = jnp.float32
BF16 = jnp.bfloat16
I32 = jnp.int32

POOL_WINDOWS = (2, 4, 8, 16)
POOL_HIST = max(POOL_WINDOWS) - 1
N_HEADS = 16
KV_HEADS = 4
HEAD_DIM = 64
Q_PER_KV = N_HEADS // KV_HEADS
GRP_WIDTH = KV_HEADS * HEAD_DIM
CMP_LEN = 32
CMP_STRIDE = 16
CMP_SEGS = CMP_LEN // CMP_STRIDE
SEL_BLOCK = 64
SEL_TOPK = 16
WINDOW = 512
Q_BLOCK = 128
N_KV_PLANES = 6
N_CACHED_PLANES = 4
FORCE_BONUS = 1.0e4
N_GROUPS = 4
EXPERTS_PER_GROUP = 4
N_EXPERTS = N_GROUPS * EXPERTS_PER_GROUP
LN_EPS = 1e-5
NEG_INF = -1e30
NEG_CLAMP = -1e29

LANES = 128
SUBLANES = 8
VMEM_LIMIT = 52 * 1024 * 1024

ROW_TILE = 256
MM_TILE_M = 512
MM_TILE_N = 512
SEL_KV_TILE = 512
EXPERT_TILE = 256
CMP_K_GROUP = 4


def _cparams(*sem):
    return pltpu.CompilerParams(dimension_semantics=sem, vmem_limit_bytes=VMEM_LIMIT)


def _dot(a, b):
    return jnp.dot(a, b, preferred_element_type=F32)


def _dot_nt(a, b):
    return lax.dot_general(a, b, (((1,), (1,)), ((), ())), preferred_element_type=F32)


def _pv_t(v_t, p):
    return _dot_nt(v_t, p).T


def _split_bf16(a):
    hi = a.astype(BF16)
    return hi, (a - hi.astype(F32)).astype(BF16)


def _masked_softmax(s, valid):
    s = jnp.where(valid, s, NEG_INF)
    m = jnp.max(s, axis=-1, keepdims=True)
    e = jnp.where(valid, jnp.exp(s - m), 0.0)
    l = jnp.sum(e, axis=-1, keepdims=True)
    return e / jnp.where(l > 0.0, l, 1.0)


def _softmax_addmask(z):
    m = jnp.maximum(jnp.max(z, axis=-1, keepdims=True), NEG_CLAMP)
    e = jnp.exp(z - m)
    l = jnp.sum(e, axis=-1, keepdims=True)
    return e, jnp.where(l > 0.0, l, 1.0)


def _layer_norm(x, g, b):
    mu = jnp.mean(x, axis=-1, keepdims=True)
    xc = x - mu
    var = jnp.mean(xc * xc, axis=-1, keepdims=True)
    return xc * lax.rsqrt(var + LN_EPS) * g + b


def _top_k_mask(score, n_top):
    col = lax.broadcasted_iota(I32, score.shape, 1)
    big = jnp.int32(score.shape[1])

    def body(_, carry):
        sc, sel = carry
        m = jnp.max(sc, axis=-1, keepdims=True)
        idx = jnp.min(jnp.where(sc == m, col, big), axis=-1, keepdims=True)
        hit = col == idx
        return jnp.where(hit, -jnp.inf, sc), jnp.where(hit, 1.0, sel)

    _, sel = lax.fori_loop(0, n_top, body, (score, jnp.zeros(score.shape, F32)))
    return sel


def _block_scores(imp, blk, qpos, n_sel):
    cur = qpos // SEL_BLOCK
    forced = (blk == 0) | (blk == cur) | (blk == cur - 1)
    visible = blk * SEL_BLOCK <= qpos
    score = jnp.where(visible, imp + jnp.where(forced, FORCE_BONUS, 0.0), -1.0)
    return jnp.where(blk < n_sel, score, -2.0)


def _adaln_kernel(c_ref, w_ref, b_ref, o_ref):
    c = c_ref[...]
    a = (c * jax.nn.sigmoid(c)).astype(BF16)
    o_ref[0] = _dot(a, w_ref[0].astype(BF16)) + b_ref[0]


def _adaln(c_all, w_ada, b_ada):
    n_layers, d, n6 = w_ada.shape
    mp = c_all.shape[0]
    tn = 1024
    assert n6 % tn == 0
    return pl.pallas_call(
        _adaln_kernel,
        grid=(n_layers, n6 // tn),
        in_specs=[pl.BlockSpec((mp, d), lambda l, n: (0, 0)),
                  pl.BlockSpec((1, d, tn), lambda l, n: (l, 0, n)),
                  pl.BlockSpec((1, 1, tn), lambda l, n: (l, 0, n))],
        out_specs=pl.BlockSpec((1, mp, tn), lambda l, n: (l, 0, n)),
        out_shape=jax.ShapeDtypeStruct((n_layers, mp, n6), F32),
        compiler_params=_cparams("arbitrary", "arbitrary"),
        name="adaln",
    )(c_all, w_ada, b_ada.reshape(n_layers, 1, n6))


class _Rows:
    def __init__(self, batch, seq, n_sample):
        self.batch = batch
        self.seq = seq
        self.n_prompt = batch * seq
        self.n_sample = n_sample
        self.sample_pad = -(-n_sample // MM_TILE_M) * MM_TILE_M
        self.n_rows = self.n_prompt + self.sample_pad
        assert seq % MM_TILE_M == 0 and MM_TILE_M % ROW_TILE == 0
        self.n_ptiles = self.n_prompt // ROW_TILE

    def mod_specs(self, d, group):
        tiles_per_seq = self.seq // ROW_TILE
        last = self.batch - 1
        n_pt = self.n_ptiles
        return [pl.BlockSpec((1, 1, SUBLANES, d),
                             lambda i, *_: (jnp.minimum(i // tiles_per_seq, last), group, 0, 0)),
                pl.BlockSpec((1, 3, ROW_TILE, d),
                             lambda i, *_: (group, 0, jnp.maximum(i - n_pt, 0), 0))]


def _mod_row(is_sample, mp_ref, ms_ref, k):
    return jnp.where(is_sample, ms_ref[0, k], mp_ref[0, 0, k:k + 1, :])


def _modulate_kernel(x_ref, mp_ref, ms_ref, u_ref, *, n_ptiles):
    is_s = pl.program_id(0) >= n_ptiles
    shift = _mod_row(is_s, mp_ref, ms_ref, 0)
    scale = _mod_row(is_s, mp_ref, ms_ref, 1)
    u_ref[...] = (x_ref[...] * (1.0 + scale) + shift).astype(BF16)


def _modulate(rows, x, modp, mods):
    d = x.shape[1]
    return pl.pallas_call(
        functools.partial(_modulate_kernel, n_ptiles=rows.n_ptiles),
        grid=(rows.n_rows // ROW_TILE,),
        in_specs=[pl.BlockSpec((ROW_TILE, d), lambda i: (i, 0))] + rows.mod_specs(d, 0),
        out_specs=pl.BlockSpec((ROW_TILE, d), lambda i: (i, 0)),
        out_shape=jax.ShapeDtypeStruct((rows.n_rows, d), BF16),
        compiler_params=_cparams("arbitrary"),
        name="modulate",
    )(x, modp, mods)


def _proj_rows_kernel(x_ref, w_ref, o_ref, wb_ref, *, act, scale):
    @pl.when(pl.program_id(1) == 0)
    def _():
        wb_ref[...] = w_ref[0].astype(BF16)

    y = _dot_nt(x_ref[...], wb_ref[...])
    if act == "sigmoid":
        y = jax.nn.sigmoid(y)
    if scale != 1.0:
        y = y * scale
    o_ref[...] = y.astype(o_ref.dtype)


def _proj_rows(x, w_t, layer, row0, ncols, tn, out_dtype, act=None, scale=1.0, m0=0, m_rows=None):
    k = x.shape[1]
    m_rows = x.shape[0] - m0 if m_rows is None else m_rows
    assert m0 % MM_TILE_M == 0 and m_rows % MM_TILE_M == 0 and row0 % tn == 0 and ncols % tn == 0
    mb, rb = m0 // MM_TILE_M, row0 // tn
    return pl.pallas_call(
        functools.partial(_proj_rows_kernel, act=act, scale=scale),
        grid=(ncols // tn, m_rows // MM_TILE_M),
        in_specs=[pl.BlockSpec((MM_TILE_M, k), lambda n, i: (mb + i, 0)),
                  pl.BlockSpec((1, tn, k), lambda n, i: (layer, rb + n, 0))],
        out_specs=pl.BlockSpec((MM_TILE_M, tn), lambda n, i: (i, n)),
        out_shape=jax.ShapeDtypeStruct((m_rows, ncols), out_dtype),
        scratch_shapes=[pltpu.VMEM((tn, k), BF16)],
        compiler_params=_cparams("arbitrary", "arbitrary"),
        name="proj_rows",
    )(x, w_t)


def _proj_t_kernel(w_ref, x_ref, o_ref, wb_ref):
    @pl.when((pl.program_id(1) == 0) & (pl.program_id(2) == 0))
    def _():
        wb_ref[...] = w_ref[0].astype(BF16)

    o_ref[0, 0] = _dot_nt(wb_ref[...], x_ref[...])


def _proj_t(x, w_t, layer, row0, n_planes, batch, seq):
    k = x.shape[1]
    tt = 2 * MM_TILE_M if seq % (2 * MM_TILE_M) == 0 else MM_TILE_M
    assert row0 % GRP_WIDTH == 0 and seq % tt == 0
    rb = row0 // GRP_WIDTH
    nt = seq // tt
    return pl.pallas_call(
        _proj_t_kernel,
        grid=(n_planes, batch, nt),
        in_specs=[pl.BlockSpec((1, GRP_WIDTH, k), lambda p, b, i: (layer, rb + p, 0)),
                  pl.BlockSpec((tt, k), lambda p, b, i: (b * nt + i, 0))],
        out_specs=pl.BlockSpec((1, 1, GRP_WIDTH, tt), lambda p, b, i: (b, p, 0, i)),
        out_shape=jax.ShapeDtypeStruct((batch, n_planes, GRP_WIDTH, seq), F32),
        scratch_shapes=[pltpu.VMEM((GRP_WIDTH, k), BF16)],
        compiler_params=_cparams("arbitrary", "arbitrary", "arbitrary"),
        name="proj_t",
    )(w_t, x)


def _pool_prompt_kernel(cur_ref, prev_ref, pw_ref, sc_ref, o_ref, ext_ref, *, tt, pg):
    i = pl.program_id(1)
    halo = 2 * SUBLANES
    ext_ref[0:halo, :] = jnp.where(i == 0, 0.0, prev_ref[...])
    ext_ref[halo:, :] = cur_ref[...]
    pos = i * tt + lax.broadcasted_iota(I32, (tt, 1), 0)
    for g, w in enumerate(POOL_WINDOWS):
        cols = slice(g * pg, (g + 1) * pg)
        cur = ext_ref[halo:halo + tt, cols]
        acc = cur
        for j in range(1, w):
            acc = acc + ext_ref[halo - j:halo - j + tt, cols]
        count = jnp.minimum(pos + 1, w).astype(F32)
        dlt = acc / count - cur
        y = _dot(dlt.astype(BF16), pw_ref[g].astype(BF16)) * sc_ref[:, cols]
        o_ref[:, cols] = y.astype(o_ref.dtype)


def _pool_prompt(up, batch, seq, pool_w, pool_scale):
    pw = up.shape[1]
    pg = pw // len(POOL_WINDOWS)
    tt = MM_TILE_M
    halo = 2 * SUBLANES
    assert POOL_HIST <= halo and seq % tt == 0
    nt = seq // tt
    return pl.pallas_call(
        functools.partial(_pool_prompt_kernel, tt=tt, pg=pg),
        grid=(batch, nt),
        in_specs=[pl.BlockSpec((tt, pw), lambda b, i: (b * nt + i, 0)),
                  pl.BlockSpec((halo, pw), lambda b, i: (jnp.maximum((b * nt + i) * (tt // halo) - 1, 0), 0)),
                  pl.BlockSpec((len(POOL_WINDOWS), pg, pg), lambda b, i: (0, 0, 0)),
                  pl.BlockSpec((1, pw), lambda b, i: (0, 0))],
        out_specs=pl.BlockSpec((tt, pw), lambda b, i: (b * nt + i, 0)),
        out_shape=jax.ShapeDtypeStruct((batch * seq, pw), BF16),
        scratch_shapes=[pltpu.VMEM((halo + tt, pw), F32)],
        compiler_params=_cparams("arbitrary", "arbitrary"),
        name="pool_prompt",
    )(up, up, pool_w, pool_scale.reshape(1, pw))


def _pool_sample_kernel(ext_ref, pw_ref, sc_ref, o_ref, *, ts, pg, pos0):
    for t in range(ts):
        for g, w in enumerate(POOL_WINDOWS):
            cols = slice(g * pg, (g + 1) * pg)
            cur = ext_ref[POOL_HIST + t, :, cols]
            acc = cur
            for j in range(1, w):
                acc = acc + ext_ref[POOL_HIST + t - j, :, cols]
            count = float(min(pos0 + t + 1, w))
            dlt = acc / count - cur
            y = _dot(dlt.astype(BF16), pw_ref[g].astype(BF16)) * sc_ref[:, cols]
            o_ref[t, :, cols] = y.astype(o_ref.dtype)


def _pool_sample(ext, ts, pos0, pool_w, pool_scale):
    n_ext, nb, pw = ext.shape
    pg = pw // len(POOL_WINDOWS)
    return pl.pallas_call(
        functools.partial(_pool_sample_kernel, ts=ts, pg=pg, pos0=pos0),
        grid=(1,),
        in_specs=[pl.BlockSpec((n_ext, nb, pw), lambda i: (0, 0, 0)),
                  pl.BlockSpec((len(POOL_WINDOWS), pg, pg), lambda i: (0, 0, 0)),
                  pl.BlockSpec((1, pw), lambda i: (0, 0))],
        out_specs=pl.BlockSpec((ts, nb, pw), lambda i: (0, 0, 0)),
        out_shape=jax.ShapeDtypeStruct((ts, nb, pw), BF16),
        compiler_params=_cparams("arbitrary"),
        name="pool_sample",
    )(ext, pool_w, pool_scale.reshape(1, pw))


def _compress_tail(ha, hb, w2t_ref):
    n_chunk = ha.shape[0]
    h = ha + pltpu.roll(hb, n_chunk - 1, 0)
    act = jax.nn.gelu(h).astype(BF16)
    w2t = w2t_ref[0].astype(BF16)
    return _dot_nt(act, w2t), _dot_nt(w2t, act)


def _compress_prompt_kernel(c_ref, w1_ref, pe_ref, w2t_ref, o_ref, ot_ref):
    c = c_ref[0, 0, 0]
    ha = _dot((c + pe_ref[0, 0]).astype(BF16), w1_ref[0, 0].astype(BF16))
    hb = _dot((c + pe_ref[0, 1]).astype(BF16), w1_ref[0, 1].astype(BF16))
    o_ref[0, 0, 0], ot_ref[0, 0, 0] = _compress_tail(ha, hb, w2t_ref)


def _compress_prompt(chunks, w1, pe, w2t):
    _, b, g, n_chunk, kdim = chunks.shape
    hid = w1.shape[-1]
    return pl.pallas_call(
        _compress_prompt_kernel,
        grid=(2, b, g),
        in_specs=[pl.BlockSpec((1, 1, 1, n_chunk, kdim), lambda p, bi, gi: (p, bi, gi, 0, 0)),
                  pl.BlockSpec((1, CMP_SEGS, kdim, hid), lambda p, bi, gi: (p, 0, 0, 0)),
                  pl.BlockSpec((1, CMP_SEGS, 1, kdim), lambda p, bi, gi: (p, 0, 0, 0)),
                  pl.BlockSpec((1, HEAD_DIM, hid), lambda p, bi, gi: (p, 0, 0))],
        out_specs=[pl.BlockSpec((1, 1, 1, n_chunk, HEAD_DIM), lambda p, bi, gi: (p, bi, gi, 0, 0)),
                   pl.BlockSpec((1, 1, 1, HEAD_DIM, n_chunk), lambda p, bi, gi: (p, bi, gi, 0, 0))],
        out_shape=[jax.ShapeDtypeStruct((2, b, g, n_chunk, HEAD_DIM), F32),
                   jax.ShapeDtypeStruct((2, b, g, HEAD_DIM, n_chunk), F32)],
        compiler_params=_cparams("arbitrary", "arbitrary", "arbitrary"),
        name="compress_prompt",
    )(chunks, w1, pe, w2t)


def _compress_sample_kernel(pt_ref, cache_ref, w1_ref, pe_ref, w2t_ref, perm_ref, o_ref, buf_ref, rows_ref,
                            w1b_ref, peb_ref, sem, *, layer, n_pages, page, nb):
    p = pl.program_id(0)
    b = pl.program_id(1)
    step = p * nb + b
    slot = step % 2
    n_chunk = n_pages * page // CMP_STRIDE
    kg = CMP_K_GROUP
    n_q = CMP_STRIDE // kg

    def page_copy(pp, bb, sl, j):
        pid = pt_ref[bb * n_pages + j]
        return pltpu.make_async_copy(cache_ref.at[layer, pid, pp], buf_ref.at[sl, j], sem.at[sl])

    def fetch(pp, bb, sl):
        def issue(j, c):
            page_copy(pp, bb, sl, j).start()
            return c

        lax.fori_loop(0, n_pages, issue, 0)

    @pl.when(step == 0)
    def _():
        fetch(p, b, slot)

    @pl.when(step + 1 < 2 * nb)
    def _():
        fetch((step + 1) // nb, (step + 1) % nb, 1 - slot)

    @pl.when(b == 0)
    def _():
        w1b_ref[...] = w1_ref[0].astype(BF16)
        bias = jnp.zeros(peb_ref.shape, F32)
        for m in range(CMP_SEGS):
            for q in range(n_q):
                pe_rows = jnp.broadcast_to(pe_ref[0, m, q], (SUBLANES, kg * HEAD_DIM)).astype(BF16)
                bias = bias + _dot(pe_rows, w1b_ref[m, q])
        peb_ref[...] = bias

    def drain(j, c):
        page_copy(p, b, slot, j).wait()
        return c

    lax.fori_loop(0, n_pages, drain, 0)
    cpp = page // CMP_STRIDE

    def to_rows(j, c):
        x_t = buf_ref[slot, j].reshape(GRP_WIDTH, page).astype(BF16)
        x = _dot_nt(perm_ref[...], x_t)
        c0 = pl.multiple_of(j * cpp, cpp)
        for s in range(CMP_STRIDE):
            rows_ref[s, pl.ds(c0, cpp), :] = x[s * cpp:(s + 1) * cpp, :]
        return c

    lax.fori_loop(0, n_pages, to_rows, 0, unroll=8)
    out = []
    for g in range(KV_HEADS):
        lanes = slice(g * HEAD_DIM, (g + 1) * HEAD_DIM)
        acc = [None] * CMP_SEGS
        for q in range(n_q):
            piece = jnp.concatenate([rows_ref[q * kg + k][:, lanes] for k in range(kg)], axis=1).astype(BF16)
            for m in range(CMP_SEGS):
                t = _dot(piece, w1b_ref[m, q])
                acc[m] = t if acc[m] is None else acc[m] + t
        out.append(_compress_tail(acc[0] + peb_ref[0:1, :], acc[1], w2t_ref)[0])
    o_ref[0, 0] = jnp.concatenate(out, axis=1)


def _compress_sample(page_table, cache_t, layer, w1, pe, w2t):
    nb, n_pages = page_table.shape
    page = cache_t.shape[-1]
    n_chunk = n_pages * page // CMP_STRIDE
    hid = w1.shape[-1]
    kg = CMP_K_GROUP
    w1g = w1.reshape(2, CMP_SEGS, CMP_STRIDE // kg, kg * HEAD_DIM, hid)
    peg = pe.reshape(2, CMP_SEGS, CMP_STRIDE // kg, 1, kg * HEAD_DIM)
    cpp = page // CMP_STRIDE
    row = jnp.arange(page)
    perm = ((CMP_STRIDE * (row % cpp) + row // cpp)[:, None] == jnp.arange(page)[None, :]).astype(BF16)
    assert page == LANES and KV_HEADS % 2 == 0
    grid_spec = pltpu.PrefetchScalarGridSpec(
        num_scalar_prefetch=1,
        grid=(2, nb),
        in_specs=[pl.BlockSpec(memory_space=pl.ANY),
                  pl.BlockSpec((1,) + w1g.shape[1:], lambda p, bi, pt: (p, 0, 0, 0, 0)),
                  pl.BlockSpec((1,) + peg.shape[1:], lambda p, bi, pt: (p, 0, 0, 0, 0)),
                  pl.BlockSpec((1, HEAD_DIM, hid), lambda p, bi, pt: (p, 0, 0)),
                  pl.BlockSpec((page, page), lambda p, bi, pt: (0, 0))],
        out_specs=pl.BlockSpec((1, 1, n_chunk, GRP_WIDTH), lambda p, bi, pt: (p, bi, 0, 0)),
        scratch_shapes=[pltpu.VMEM((2, n_pages, KV_HEADS, HEAD_DIM, page), F32),
                        pltpu.VMEM((CMP_STRIDE, n_chunk, GRP_WIDTH), F32),
                        pltpu.VMEM(w1g.shape[1:], BF16),
                        pltpu.VMEM((SUBLANES, hid), F32),
                        pltpu.SemaphoreType.DMA((2,))],
    )
    return pl.pallas_call(
        functools.partial(_compress_sample_kernel, layer=layer, n_pages=n_pages, page=page, nb=nb),
        grid_spec=grid_spec,
        out_shape=jax.ShapeDtypeStruct((2, nb, n_chunk, GRP_WIDTH), F32),
        compiler_params=_cparams("arbitrary", "arbitrary"),
        name="compress_sample",
    )(page_table.reshape(-1), cache_t, w1g, peg, w2t, perm)


def _attn_prompt_kernel(slope_ref, q_ref, kc_ref, vct_ref, ks_ref, vs_ref, kw_ref, vw_ref, gate_ref, covert_ref,
                        o_ref, m_ref, l_ref, acc_ref, st_ref, bias_ref, flag_ref, *, seq, n_cmp, n_sel, n_top, wlen):
    qb = Q_BLOCK
    tk = SEL_KV_TILE
    g = pl.program_id(1)
    p0 = pl.program_id(2) * qb
    qpos = p0 + lax.broadcasted_iota(I32, (1, qb), 1)
    slopes = [slope_ref[g * Q_PER_KV + h] for h in range(Q_PER_KV)]
    q_t = q_ref[...].astype(F32).T.astype(BF16)
    q_all = jnp.concatenate([q_t[h * HEAD_DIM:(h + 1) * HEAD_DIM] for h in range(Q_PER_KV)], axis=1)
    head = lambda a, h: a[:, h * qb:(h + 1) * qb]

    def col_softmax(z):
        m = jnp.maximum(jnp.max(z, axis=0, keepdims=True), NEG_CLAMP)
        e = jnp.exp(z - m)
        l = jnp.sum(e, axis=0, keepdims=True)
        return e, jnp.where(l > 0.0, l, 1.0)

    n_chunk = kc_ref.shape[3]
    crow = lax.broadcasted_iota(I32, (n_chunk, 1), 0)
    d_c = qpos - (crow * CMP_STRIDE + (CMP_LEN - 1))
    mask_c = jnp.where((d_c >= 0) & (crow < n_cmp), 0.0, NEG_INF)
    d_cf = d_c.astype(F32)
    vct = vct_ref[0, 0, 0].astype(BF16)
    s_c = _dot(kc_ref[0, 0, 0].astype(BF16), q_all)
    p_sum = jnp.zeros((n_chunk, qb), F32)
    o_c = []
    for h in range(Q_PER_KV):
        e, l = col_softmax(head(s_c, h) - slopes[h] * d_cf + mask_c)
        p = e / l
        p_sum = p_sum + p
        o_c.append(_dot(vct, p.astype(BF16)))

    p_hi, p_lo = _split_bf16(p_sum)
    cover_t = covert_ref[...]
    imp_t = _dot(cover_t, p_hi) + _dot(cover_t, p_lo)
    n_blk = -(-n_sel // SUBLANES) * SUBLANES
    blk = lax.broadcasted_iota(I32, (n_blk, 1), 0)
    score = _block_scores(imp_t[0:n_blk], blk, qpos, n_sel)
    st_ref[0:n_blk, :] = score

    def count_beaten(k, cnt):
        row = st_ref[pl.ds(k, 1), :]
        return cnt + jnp.where((row > score) | ((row == score) & (k < blk)), 1.0, 0.0)

    rank = lax.fori_loop(0, n_sel, count_beaten, jnp.zeros(score.shape, F32), unroll=4)
    chosen = rank < n_top
    st_ref[0:n_blk, :] = jnp.where(chosen, 0.0, NEG_INF)
    bpt = SEL_KV_TILE // SEL_BLOCK
    for t in range(n_blk // bpt):
        flag_ref[t] = jnp.max(jnp.where(chosen[t * bpt:(t + 1) * bpt], 1.0, 0.0)).astype(I32)

    krow = lax.broadcasted_iota(I32, (tk, 1), 0)
    rel = (krow - (qpos - p0)).astype(F32)
    for h in range(Q_PER_KV):
        bias_ref[h] = slopes[h] * rel
    m_ref[...] = jnp.full(m_ref.shape, NEG_INF, F32)
    l_ref[...] = jnp.zeros(l_ref.shape, F32)
    acc_ref[...] = jnp.zeros(acc_ref.shape, F32)
    n_tiles = (p0 + qb + tk - 1) // tk

    def sweep_tile(jt, causal):
        k0 = pl.multiple_of(jt * tk, tk)
        k_rows = ks_ref[0, 0, :, pl.ds(k0, tk)].T.astype(BF16)
        v_t = vs_ref[0, 0, :, pl.ds(k0, tk)].astype(BF16)
        s = _dot(k_rows, q_all)
        j0 = k0 // SEL_BLOCK
        madd = jnp.concatenate([jnp.broadcast_to(st_ref[pl.ds(j0 + jb, 1), :], (SEL_BLOCK, qb))
                                for jb in range(tk // SEL_BLOCK)], axis=0)
        if causal:
            madd = jnp.where(k0 + krow <= qpos, madd, NEG_INF)
        off = (k0 - p0).astype(F32)
        for h in range(Q_PER_KV):
            z = head(s, h) + (bias_ref[h] + madd)
            shift = slopes[h] * off
            m_old = m_ref[h]
            m_new = jnp.maximum(m_old, jnp.max(z, axis=0, keepdims=True) + shift)
            m_use = jnp.maximum(m_new, NEG_CLAMP)
            e = jnp.exp(z - (m_use - shift))
            alpha = jnp.exp(m_old - m_use)
            l_ref[h] = alpha * l_ref[h] + jnp.sum(e, axis=0, keepdims=True)
            acc_ref[h] = alpha * acc_ref[h] + _dot(v_t, e.astype(BF16))
            m_ref[h] = m_new

    def sweep_body(jt, carry):
        @pl.when(flag_ref[jt] > 0)
        def _():
            sweep_tile(jt, False)

        return carry

    lax.fori_loop(0, n_tiles - 1, sweep_body, 0)
    sweep_tile(n_tiles - 1, True)

    w0 = pl.multiple_of(jnp.minimum(jnp.maximum(p0 - WINDOW, 0), seq - wlen), qb)
    relw = (w0 + lax.broadcasted_iota(I32, (wlen, 1), 0)) - qpos
    mask_w = jnp.where((relw <= 0) & (relw > -WINDOW), 0.0, NEG_INF)
    relw_f = relw.astype(F32)
    kw_rows = kw_ref[0, 0, :, pl.ds(w0, wlen)].T.astype(BF16)
    vw_t = vw_ref[0, 0, :, pl.ds(w0, wlen)].astype(BF16)
    s_w = _dot(kw_rows, q_all)

    out = []
    for h in range(Q_PER_KV):
        e, l = col_softmax(head(s_w, h) + slopes[h] * relw_f + mask_w)
        o_w = _dot(vw_t, e.astype(BF16)) / l
        l_s = l_ref[h]
        o_s = acc_ref[h] / jnp.where(l_s > 0.0, l_s, 1.0)
        g0 = gate_ref[0, 0, 3 * h + 0:3 * h + 1, :]
        g1 = gate_ref[0, 0, 3 * h + 1:3 * h + 2, :]
        g2 = gate_ref[0, 0, 3 * h + 2:3 * h + 3, :]
        out.append(g0 * o_c[h] + g1 * o_s + g2 * o_w)
    o_ref[...] = jnp.concatenate(out, axis=0).T.astype(o_ref.dtype)


def _attn_prompt(slopes, q, batch, seq, cmp_rows, cmp_t, kv_t, win_t, gates_t, cover_t):
    n_chunk = cmp_rows.shape[3]
    n_cmp = (seq - CMP_LEN) // CMP_STRIDE + 1
    n_sel = -(-seq // SEL_BLOCK)
    n_top = min(SEL_TOPK, n_sel)
    wlen = min(WINDOW + Q_BLOCK, seq)
    nsp = cover_t.shape[0]
    nqb = seq // Q_BLOCK
    assert seq % SEL_KV_TILE == 0 and seq % Q_BLOCK == 0 and n_chunk >= n_cmp and wlen % LANES == 0

    def plane(pidx):
        return pl.BlockSpec((1, 1, HEAD_DIM, seq), lambda bi, gi, i: (bi, pidx, gi, 0))

    return pl.pallas_call(
        functools.partial(_attn_prompt_kernel, seq=seq, n_cmp=n_cmp, n_sel=n_sel, n_top=n_top, wlen=wlen),
        grid=(batch, KV_HEADS, nqb),
        in_specs=[pl.BlockSpec(memory_space=pltpu.SMEM),
                  pl.BlockSpec((Q_BLOCK, GRP_WIDTH), lambda bi, gi, i: (bi * nqb + i, gi)),
                  pl.BlockSpec((1, 1, 1, n_chunk, HEAD_DIM), lambda bi, gi, i: (0, bi, gi, 0, 0)),
                  pl.BlockSpec((1, 1, 1, HEAD_DIM, n_chunk), lambda bi, gi, i: (1, bi, gi, 0, 0)),
                  plane(2), plane(3), plane(0), plane(1),
                  pl.BlockSpec((1, 1, 3 * Q_PER_KV, Q_BLOCK), lambda bi, gi, i: (bi, gi, 0, i)),
                  pl.BlockSpec(cover_t.shape, lambda bi, gi, i: (0, 0))],
        out_specs=pl.BlockSpec((Q_BLOCK, GRP_WIDTH), lambda bi, gi, i: (bi * nqb + i, gi)),
        out_shape=jax.ShapeDtypeStruct((batch * seq, N_HEADS * HEAD_DIM), BF16),
        scratch_shapes=[pltpu.VMEM((Q_PER_KV, 1, Q_BLOCK), F32), pltpu.VMEM((Q_PER_KV, 1, Q_BLOCK), F32),
                        pltpu.VMEM((Q_PER_KV, HEAD_DIM, Q_BLOCK), F32), pltpu.VMEM((nsp, Q_BLOCK), F32),
                        pltpu.VMEM((Q_PER_KV, SEL_KV_TILE, Q_BLOCK), F32),
                        pltpu.SMEM((seq // SEL_KV_TILE,), I32)],
        compiler_params=_cparams("arbitrary", "arbitrary", "arbitrary"),
        name="attn_prompt",
    )(slopes, q, cmp_rows, cmp_t, kv_t, kv_t, win_t, win_t, gates_t, cover_t)


def _attn_sample_kernel(pt_ref, qbd_ref, kc_ref, vc_ref, cache_ref, new_ref, win_ref, gate_ref, slope_ref,
                        qpos_ref, cover_ref, expand_ref, o_ref, kt_ref, vt_ref, sem,
                        *, layer, n_pages, page, past, ts, n_cmp, n_sel, n_top, win_buf, gt):
    b = pl.program_id(0)
    nk = kt_ref.shape[2]

    def page_copies(j):
        pid = pt_ref[b * n_pages + j]
        dst = pl.ds(pl.multiple_of(j * page, page), page)
        return (pltpu.make_async_copy(cache_ref.at[layer, pid, 2], kt_ref.at[:, :, dst], sem),
                pltpu.make_async_copy(cache_ref.at[layer, pid, 3], vt_ref.at[:, :, dst], sem))

    def issue(j, c):
        for cp in page_copies(j):
            cp.start()
        return c

    def drain(j, c):
        for cp in page_copies(j):
            cp.wait()
        return c

    lax.fori_loop(0, n_pages, issue, 0)
    kt_ref[:, :, past:] = new_ref[0, 0].reshape(KV_HEADS, HEAD_DIM, nk - past)
    vt_ref[:, :, past:] = new_ref[0, 1].reshape(KV_HEADS, HEAD_DIM, nk - past)

    q = qbd_ref[0]
    nrow = q.shape[0]
    slope = slope_ref[...]
    qpos = qpos_ref[...]
    lane_grp = lax.broadcasted_iota(I32, (nrow, GRP_WIDTH), 1) // HEAD_DIM
    row_grp = (lax.broadcasted_iota(I32, (nrow, GRP_WIDTH), 0) % gt) // (gt // KV_HEADS)
    diag = lane_grp == row_grp

    def own_group(full):
        kept = jnp.where(diag, full, 0.0)
        out = kept[:, 0:HEAD_DIM]
        for g in range(1, KV_HEADS):
            out = out + kept[:, g * HEAD_DIM:(g + 1) * HEAD_DIM]
        return out

    n_chunk = kc_ref.shape[2]
    ccol = lax.broadcasted_iota(I32, (1, n_chunk), 1)
    d_c = qpos - (ccol * CMP_STRIDE + (CMP_LEN - 1))
    s_c = _dot_nt(q, kc_ref[0, 0].astype(BF16)) - slope * d_c.astype(F32)
    p_c = _masked_softmax(s_c, (d_c >= 0) & (ccol < n_cmp))
    o_c = own_group(_dot(p_c.astype(BF16), vc_ref[0, 0].astype(BF16)))

    p_sum = p_c[0:gt]
    for h in range(1, Q_PER_KV):
        p_sum = p_sum + p_c[h * gt:(h + 1) * gt]
    p_hi, p_lo = _split_bf16(p_sum)
    imp = _dot(p_hi, cover_ref[...]) + _dot(p_lo, cover_ref[...])
    scol = lax.broadcasted_iota(I32, (1, imp.shape[1]), 1)
    selmask = _top_k_mask(_block_scores(imp, scol, qpos[0:gt], n_sel), n_top).astype(BF16)

    n_newl = new_ref.shape[3]
    wcol = lax.broadcasted_iota(I32, (1, win_buf + n_newl), 1)
    kpos_w = jnp.where(wcol < win_buf, past - win_buf + wcol, past + wcol - win_buf)
    d_w = qpos - kpos_w
    valid_w = (d_w >= 0) & (d_w < WINDOW) & (wcol < win_buf + ts)
    kw_t = win_ref[0, 0, 0].reshape(GRP_WIDTH, win_buf).astype(BF16)
    vw_t = win_ref[0, 0, 1].reshape(GRP_WIDTH, win_buf).astype(BF16)
    s_w = jnp.concatenate([_dot(q, kw_t), _dot(q, new_ref[0, 2].astype(BF16))], axis=1)
    p_w = _masked_softmax(s_w - slope * d_w.astype(F32), valid_w).astype(BF16)
    o_w = own_group(_dot_nt(p_w[:, :win_buf], vw_t) + _dot_nt(p_w[:, win_buf:], new_ref[0, 3].astype(BF16)))

    lax.fori_loop(0, n_pages, drain, 0)
    selk = _dot(selmask, expand_ref[...])
    selk = jnp.concatenate([selk] * Q_PER_KV, axis=0)
    d_s = qpos - lax.broadcasted_iota(I32, (1, nk), 1)
    k_t = kt_ref[...].reshape(GRP_WIDTH, nk).astype(BF16)
    s_s = _dot(q, k_t) - slope * d_s.astype(F32)
    p_s = _masked_softmax(s_s, (selk > 0.5) & (d_s >= 0))
    v_t = vt_ref[...].reshape(GRP_WIDTH, nk).astype(BF16)
    o_s = own_group(_dot_nt(p_s.astype(BF16), v_t))

    gate = gate_ref[0]
    o_ref[0] = gate[:, 0:1] * o_c + gate[:, 1:2] * o_s + gate[:, 2:3] * o_w


def _attn_sample(page_table, qbd, cmp_kv, cache_t, new_t, win_t, gates, slope_rows, qpos_rows,
                 cover, expand, *, layer, past, ts, n_cmp, n_sel, gt):
    nb, n_pages = page_table.shape
    page = cache_t.shape[-1]
    win_buf = win_t.shape[-1]
    nrow = qbd.shape[1]
    nk = expand.shape[1]
    n_top = min(SEL_TOPK, n_sel)
    assert nk == past + new_t.shape[3] and win_buf % LANES == 0

    def per_seq(shape):
        return pl.BlockSpec((1,) + tuple(shape[1:]), lambda bi, pt: (bi,) + (0,) * (len(shape) - 1))

    def whole(shape):
        return pl.BlockSpec(tuple(shape), lambda bi, pt: (0,) * len(shape))

    grid_spec = pltpu.PrefetchScalarGridSpec(
        num_scalar_prefetch=1,
        grid=(nb,),
        in_specs=[per_seq(qbd.shape),
                  pl.BlockSpec((1, 1) + tuple(cmp_kv.shape[2:]), lambda bi, pt: (0, bi, 0, 0)),
                  pl.BlockSpec((1, 1) + tuple(cmp_kv.shape[2:]), lambda bi, pt: (1, bi, 0, 0)),
                  pl.BlockSpec(memory_space=pl.ANY), per_seq(new_t.shape),
                  pl.BlockSpec((1, 1) + tuple(win_t.shape[2:]), lambda bi, pt: (layer, bi, 0, 0, 0, 0)),
                  per_seq(gates.shape), whole(slope_rows.shape), whole(qpos_rows.shape),
                  whole(cover.shape), whole(expand.shape)],
        out_specs=pl.BlockSpec((1, nrow, HEAD_DIM), lambda bi, pt: (bi, 0, 0)),
        scratch_shapes=[pltpu.VMEM((KV_HEADS, HEAD_DIM, nk), F32), pltpu.VMEM((KV_HEADS, HEAD_DIM, nk), F32),
                        pltpu.SemaphoreType.DMA(())],
    )
    return pl.pallas_call(
        functools.partial(_attn_sample_kernel, layer=layer, n_pages=n_pages, page=page, past=past, ts=ts,
                          n_cmp=n_cmp, n_sel=n_sel, n_top=n_top, win_buf=win_buf, gt=gt),
        grid_spec=grid_spec,
        out_shape=jax.ShapeDtypeStruct((nb, nrow, HEAD_DIM), F32),
        compiler_params=_cparams("arbitrary"),
        name="attn_sample",
    )(page_table.reshape(-1), qbd, cmp_kv, cmp_kv, cache_t, new_t, win_t, gates, slope_rows, qpos_rows,
      cover, expand)


def _merge_kernel(u_ref, yp_ref, ya_ref, wgp_ref, wga_ref, wp_ref, wa_ref, o_ref, wgpb_ref, wgab_ref, wpb_ref,
                  wab_ref):
    @pl.when(pl.program_id(1) == 0)
    def _():
        wgpb_ref[...] = wgp_ref[0].astype(BF16)
        wgab_ref[...] = wga_ref[0].astype(BF16)
        wpb_ref[...] = wp_ref[0].astype(BF16)
        wab_ref[...] = wa_ref[0].astype(BF16)

    u = u_ref[...]
    g_pool = jax.nn.sigmoid(_dot_nt(u, wgpb_ref[...]))
    g_attn = jax.nn.sigmoid(_dot_nt(u, wgab_ref[...]))
    merged = g_pool * _dot(yp_ref[...], wpb_ref[...]) + g_attn * _dot(ya_ref[...], wab_ref[...])
    o_ref[...] = merged.astype(o_ref.dtype)


def _merge(u, y_pool, y_attn, w_gm_t, w_up_pool, w_up_nsa, layer):
    m, d = u.shape
    kp = y_pool.shape[1]
    ka = y_attn.shape[1]
    tn = MM_TILE_N
    nn = d // tn
    row = lambda w: pl.BlockSpec((MM_TILE_M, w), lambda n, i: (i, 0))
    return pl.pallas_call(
        _merge_kernel,
        grid=(nn, m // MM_TILE_M),
        in_specs=[row(d), row(kp), row(ka),
                  pl.BlockSpec((1, tn, d), lambda n, i: (layer, n, 0)),
                  pl.BlockSpec((1, tn, d), lambda n, i: (layer, nn + n, 0)),
                  pl.BlockSpec((1, kp, tn), lambda n, i: (layer, 0, n)),
                  pl.BlockSpec((1, ka, tn), lambda n, i: (layer, 0, n))],
        out_specs=pl.BlockSpec((MM_TILE_M, tn), lambda n, i: (i, n)),
        out_shape=jax.ShapeDtypeStruct((m, d), BF16),
        scratch_shapes=[pltpu.VMEM((tn, d), BF16), pltpu.VMEM((tn, d), BF16),
                        pltpu.VMEM((kp, tn), BF16), pltpu.VMEM((ka, tn), BF16)],
        compiler_params=_cparams("arbitrary", "arbitrary"),
        name="merge",
    )(u, y_pool, y_attn, w_gm_t, w_gm_t, w_up_pool, w_up_nsa)


def _route(logits):
    lane = lax.broadcasted_iota(I32, logits.shape, 1)
    is_grp = lane < N_GROUPS
    p_grp = _masked_softmax(logits, is_grp)
    p_top = jnp.max(p_grp, axis=-1, keepdims=True)
    grp = jnp.min(jnp.where(is_grp & (p_grp == p_top), lane, LANES), axis=-1, keepdims=True)
    first = N_GROUPS + grp * EXPERTS_PER_GROUP
    in_grp = (lane >= first) & (lane < first + EXPERTS_PER_GROUP)
    v = _masked_softmax(logits, in_grp)
    v0 = jnp.max(jnp.where(in_grp, v, -1.0), axis=-1, keepdims=True)
    i0 = jnp.min(jnp.where(in_grp & (v == v0), lane, LANES), axis=-1, keepdims=True)
    rest = in_grp & (lane != i0)
    v1 = jnp.max(jnp.where(rest, v, -1.0), axis=-1, keepdims=True)
    i1 = jnp.min(jnp.where(rest & (v == v1), lane, LANES), axis=-1, keepdims=True)
    tot = v0 + v1
    out = jnp.where(lane == 0, (i0 - N_GROUPS).astype(F32), 0.0)
    out = jnp.where(lane == 1, (i1 - N_GROUPS).astype(F32), out)
    out = jnp.where(lane == 2, p_top * (v0 / tot), out)
    out = jnp.where(lane == 3, p_top * (v1 / tot), out)
    return out


def _outproj_kernel(mg_ref, x_ref, w_ref, mp_ref, ms_ref, g_ref, b_ref, wr_ref, br_ref,
                    x1_ref, u_ref, rt_ref, *, n_ptiles, alpha):
    is_s = pl.program_id(0) >= n_ptiles
    gate1 = _mod_row(is_s, mp_ref, ms_ref, 0)
    shift2 = _mod_row(is_s, mp_ref, ms_ref, 1)
    scale2 = _mod_row(is_s, mp_ref, ms_ref, 2)
    mix = _dot(mg_ref[...], w_ref[...])
    x1 = _layer_norm(alpha * x_ref[...] + gate1 * mix, g_ref[...], b_ref[...])
    x1_ref[...] = x1
    u = x1 * (1.0 + scale2) + shift2
    u_ref[...] = u
    wr_hi, wr_lo = _split_bf16(wr_ref[...])
    u_hi, u_lo = _split_bf16(u)
    logits = _dot_nt(u_hi, wr_hi) + _dot_nt(u_lo, wr_hi) + _dot_nt(u_hi, wr_lo) + br_ref[...]
    rt_ref[...] = _route(logits)


def _outproj(rows, merged, x, w_out_bf, modp, mods, ln_g, ln_b, w_router_t, b_router, alpha):
    d = x.shape[1]
    row = pl.BlockSpec((ROW_TILE, d), lambda i: (i, 0))
    vec = pl.BlockSpec((1, d), lambda i: (0, 0))
    return pl.pallas_call(
        functools.partial(_outproj_kernel, n_ptiles=rows.n_ptiles, alpha=alpha),
        grid=(rows.n_rows // ROW_TILE,),
        in_specs=[row, row, pl.BlockSpec((d, d), lambda i: (0, 0))] + rows.mod_specs(d, 1)
        + [vec, vec, pl.BlockSpec((LANES, d), lambda i: (0, 0)), pl.BlockSpec((1, LANES), lambda i: (0, 0))],
        out_specs=[row, row, pl.BlockSpec((ROW_TILE, LANES), lambda i: (i, 0))],
        out_shape=[jax.ShapeDtypeStruct((rows.n_rows, d), F32),
                   jax.ShapeDtypeStruct((rows.n_rows, d), F32),
                   jax.ShapeDtypeStruct((rows.n_rows, LANES), F32)],
        compiler_params=_cparams("arbitrary"),
        name="outproj_ln_route",
    )(merged, x, w_out_bf, modp, mods, ln_g.reshape(1, d), ln_b.reshape(1, d), w_router_t, b_router)


def _moe_scatter_kernel(dest_ref, pad_ref, u_ref, xs_ref, zero_ref, sem, *, n_real):
    i = pl.program_id(0)
    tm = u_ref.shape[0]

    def pad_copy(r):
        return pltpu.make_async_copy(zero_ref, xs_ref.at[pl.ds(r, 1)], sem)

    @pl.when(i == 0)
    def _():
        zero_ref[...] = jnp.zeros(zero_ref.shape, zero_ref.dtype)
        for e in range(N_EXPERTS + 1):
            lo, hi = pad_ref[2 * e], pad_ref[2 * e + 1]

            def issue_pad(r, c):
                pad_copy(r).start()
                return c

            def drain_pad(r, c):
                pad_copy(r).wait()
                return c

            lax.fori_loop(lo, hi, issue_pad, 0)
            lax.fori_loop(lo, hi, drain_pad, 0)

    def row_copy(r, dst):
        return pltpu.make_async_copy(u_ref.at[pl.ds(r, 1)], xs_ref.at[pl.ds(dst, 1)], sem)

    def issue(r, c):
        base = 2 * (i * tm + r)
        row_copy(r, dest_ref[base]).start()
        row_copy(r, dest_ref[base + 1]).start()
        return c

    def drain(r, c):
        row_copy(r, 0).wait()
        row_copy(r, 0).wait()
        return c

    @pl.when((i + 1) * tm <= n_real)
    def _():
        lax.fori_loop(0, tm, issue, 0, unroll=8)
        lax.fori_loop(0, tm, drain, 0, unroll=8)

    @pl.when((i + 1) * tm > n_real)
    def _():
        lax.fori_loop(0, n_real - i * tm, issue, 0)
        lax.fori_loop(0, n_real - i * tm, drain, 0)


def _moe_scatter(dest, pads, u, n_real, r_pad):
    d = u.shape[1]
    tm = EXPERT_TILE
    grid_spec = pltpu.PrefetchScalarGridSpec(
        num_scalar_prefetch=2,
        grid=(-(-n_real // tm),),
        in_specs=[pl.BlockSpec((tm, d), lambda i, *_: (i, 0))],
        out_specs=pl.BlockSpec(memory_space=pl.ANY),
        scratch_shapes=[pltpu.VMEM((1, d), F32), pltpu.SemaphoreType.DMA(())],
    )
    return pl.pallas_call(
        functools.partial(_moe_scatter_kernel, n_real=n_real),
        grid_spec=grid_spec,
        out_shape=jax.ShapeDtypeStruct((r_pad, d), F32),
        compiler_params=_cparams("arbitrary"),
        name="moe_scatter",
    )(dest, pads, u)


def _moe_expert_kernel(te_ref, used_ref, x_ref, wg_ref, wu_ref, wd_ref, o_ref, wgb_ref, wub_ref, wdb_ref):
    t = pl.program_id(0)
    fresh = jnp.logical_or(t == 0, te_ref[t] != te_ref[jnp.maximum(t - 1, 0)])

    @pl.when(fresh)
    def _():
        wgb_ref[...] = wg_ref[0, 0].astype(BF16)
        wub_ref[...] = wu_ref[0, 0].astype(BF16)
        wdb_ref[...] = wd_ref[0, 0].astype(BF16)

    @pl.when(t < used_ref[0])
    def _():
        x = x_ref[...].astype(BF16)
        h = jax.nn.silu(_dot(x, wgb_ref[...])) * _dot(x, wub_ref[...])
        o_ref[...] = _dot(h.astype(BF16), wdb_ref[...])

    @pl.when(t >= used_ref[0])
    def _():
        o_ref[...] = jnp.zeros(o_ref.shape, o_ref.dtype)


def _moe_experts(tile_expert, n_used, xs, w_gate, w_up, w_down, layer):
    r_pad, d = xs.shape
    f = w_gate.shape[3]
    tm = EXPERT_TILE
    n_tiles = r_pad // tm

    def x_map(t, te, nu):
        return (jnp.minimum(t, jnp.maximum(nu[0] - 1, 0)), 0)

    grid_spec = pltpu.PrefetchScalarGridSpec(
        num_scalar_prefetch=2,
        grid=(n_tiles,),
        in_specs=[pl.BlockSpec((tm, d), x_map),
                  pl.BlockSpec((1, 1, d, f), lambda t, te, nu: (layer, te[t], 0, 0)),
                  pl.BlockSpec((1, 1, d, f), lambda t, te, nu: (layer, te[t], 0, 0)),
                  pl.BlockSpec((1, 1, f, d), lambda t, te, nu: (layer, te[t], 0, 0))],
        out_specs=pl.BlockSpec((tm, d), lambda t, te, nu: (t, 0)),
        scratch_shapes=[pltpu.VMEM((d, f), BF16), pltpu.VMEM((d, f), BF16), pltpu.VMEM((f, d), BF16)],
    )
    return pl.pallas_call(
        _moe_expert_kernel,
        grid_spec=grid_spec,
        out_shape=jax.ShapeDtypeStruct((r_pad, d), F32),
        compiler_params=_cparams("arbitrary"),
        name="moe_experts",
    )(tile_expert, n_used, xs, w_gate, w_up, w_down)


def _moe_combine_kernel(pos_ref, y_ref, x1_ref, rt_ref, mp_ref, ms_ref, g_ref, b_ref, x2_ref, *rest,
                        n_ptiles, alpha, emit_next):
    if emit_next:
        u_ref, buf_ref, sem = rest
    else:
        buf_ref, sem = rest
    i = pl.program_id(0)
    tm = x1_ref.shape[0]

    def row_copy(r, k, src):
        return pltpu.make_async_copy(y_ref.at[pl.ds(src, 1)], buf_ref.at[k, pl.ds(r, 1)], sem)

    def issue(r, c):
        base = 2 * (i * tm + r)
        row_copy(r, 0, pos_ref[base]).start()
        row_copy(r, 1, pos_ref[base + 1]).start()
        return c

    def drain(r, c):
        row_copy(r, 0, 0).wait()
        row_copy(r, 1, 0).wait()
        return c

    lax.fori_loop(0, tm, issue, 0, unroll=8)
    lax.fori_loop(0, tm, drain, 0, unroll=8)
    is_s = i >= n_ptiles
    gate2 = _mod_row(is_s, mp_ref, ms_ref, 0)
    ffn = rt_ref[:, 2:3] * buf_ref[0] + rt_ref[:, 3:4] * buf_ref[1]
    x2 = _layer_norm(alpha * x1_ref[...] + gate2 * ffn, g_ref[...], b_ref[...])
    x2_ref[...] = x2
    if emit_next:
        shift = _mod_row(is_s, mp_ref, ms_ref, 1)
        scale = _mod_row(is_s, mp_ref, ms_ref, 2)
        u_ref[...] = (x2 * (1.0 + scale) + shift).astype(u_ref.dtype)


def _moe_combine(rows, pos, y_rows, x1, route, modp, mods, ln_g, ln_b, alpha, emit_next):
    d = x1.shape[1]
    row = pl.BlockSpec((ROW_TILE, d), lambda i, *_: (i, 0))
    vec = pl.BlockSpec((1, d), lambda i, *_: (0, 0))
    out_specs = [row]
    out_shape = [jax.ShapeDtypeStruct((rows.n_rows, d), F32)]
    if emit_next:
        out_specs.append(row)
        out_shape.append(jax.ShapeDtypeStruct((rows.n_rows, d), BF16))
    grid_spec = pltpu.PrefetchScalarGridSpec(
        num_scalar_prefetch=1,
        grid=(rows.n_rows // ROW_TILE,),
        in_specs=[pl.BlockSpec(memory_space=pl.ANY), row, pl.BlockSpec((ROW_TILE, LANES), lambda i, *_: (i, 0))]
        + rows.mod_specs(d, 2) + [vec, vec],
        out_specs=out_specs,
        scratch_shapes=[pltpu.VMEM((2, ROW_TILE, d), F32), pltpu.SemaphoreType.DMA(())],
    )
    out = pl.pallas_call(
        functools.partial(_moe_combine_kernel, n_ptiles=rows.n_ptiles, alpha=alpha, emit_next=emit_next),
        grid_spec=grid_spec,
        out_shape=out_shape,
        compiler_params=_cparams("arbitrary"),
        name="moe_combine_ln",
    )(pos, y_rows, x1, route, modp, mods, ln_g.reshape(1, d), ln_b.reshape(1, d))
    return out if emit_next else (out[0], None)


def _moe_dispatch(route, n_real, n_rows):
    tm = EXPERT_TILE
    n_pairs = 2 * n_real
    n_tiles = -(-(n_pairs + N_EXPERTS * (tm - 1)) // tm)
    eid = route[:n_real, 0:2].astype(I32).reshape(n_pairs, 1)
    onehot = (eid == jnp.arange(N_EXPERTS, dtype=I32)[None, :]).astype(I32)
    seen = jnp.cumsum(onehot, axis=0)
    counts = seen[-1]
    padded = -(-counts // tm) * tm
    ends_pad = jnp.cumsum(padded)
    starts_pad = ends_pad - padded
    dest = jnp.sum(onehot * (starts_pad[None, :] + seen - 1), axis=1).astype(I32)
    pos = jnp.concatenate([dest, jnp.zeros((2 * (n_rows - n_real),), I32)])
    pads = jnp.stack([starts_pad + counts, ends_pad], axis=1).reshape(-1)
    pads = jnp.concatenate([pads, ends_pad[-1:], jnp.full((1,), n_tiles * tm)]).astype(I32)
    n_used = (ends_pad[-1] // tm).astype(I32)
    tile_start = jnp.arange(n_tiles, dtype=I32) * tm
    tile_e = jnp.minimum(jnp.sum((tile_start[:, None] >= ends_pad[None, :]).astype(I32), axis=1), N_EXPERTS - 1)
    last_e = jnp.sum(jnp.where(jnp.arange(n_tiles) == jnp.maximum(n_used - 1, 0), tile_e, 0))
    tile_e = jnp.where(jnp.arange(n_tiles) < n_used, tile_e, last_e).astype(I32)
    return pos, pads, tile_e, n_used.reshape(1), n_tiles * tm


def _cover_matrix(n_chunk, n_cmp, n_sel, n_sel_pad):
    ci = jnp.arange(n_chunk)[:, None]
    sj = jnp.arange(n_sel_pad)[None, :]
    hit = ((ci * CMP_STRIDE < sj * SEL_BLOCK + SEL_BLOCK)
           & (ci * CMP_STRIDE + CMP_LEN - 1 >= sj * SEL_BLOCK)
           & (ci < n_cmp) & (sj < n_sel))
    return hit.astype(BF16)


def kernel(x_prompt, x_sample, c_prompt, c_sample, cache_kv, cache_win, state_pool, page_table, w_ada, b_ada,
           w_in, cmp_w1, cmp_pe, cmp_w2, pool_w, pool_scale, w_up_pool, w_up_nsa, w_out, ln1_g, ln1_b, w_rg,
           b_rg, w_re, b_re, w_gate, w_up, w_down, ln2_g, ln2_b):
    batch, seq, d = x_prompt.shape
    nb, ts, _ = x_sample.shape
    n_layers = w_ada.shape[0]
    page = cache_kv.shape[2]
    n_pages = page_table.shape[1]
    past = n_pages * page
    win_buf = cache_win.shape[2]
    pool_width = pool_w.shape[1] * pool_w.shape[2]
    q_width = N_HEADS * HEAD_DIM
    kv_width = N_KV_PLANES * GRP_WIDTH
    gn_width = 3 * N_HEADS
    alpha = (2 * n_layers) ** 0.25
    cmp_hid = cmp_w1.shape[-1]
    chunk_w = CMP_STRIDE * HEAD_DIM

    rows = _Rows(batch, seq, nb * ts)
    n_p, n_s, n_rows = rows.n_prompt, rows.n_sample, rows.n_rows
    n_real = n_p + n_s

    x = jnp.concatenate([x_prompt.reshape(n_p, d), x_sample.reshape(n_s, d),
                         jnp.zeros((rows.sample_pad - n_s, d), F32)], axis=0)

    n_seq = batch + nb
    c_all = jnp.concatenate([c_prompt, c_sample, jnp.zeros((-n_seq % SUBLANES, d), F32)], axis=0)
    mod = _adaln(c_all, w_ada, b_ada)[:, :n_seq].reshape(n_layers, n_seq, 6, d)
    nxt = jnp.concatenate([mod[1:], jnp.zeros_like(mod[:1])], axis=0)
    mod9 = jnp.stack([mod[:, :, 0], mod[:, :, 1], jnp.zeros_like(mod[:, :, 0]),
                      mod[:, :, 2], mod[:, :, 3], mod[:, :, 4],
                      mod[:, :, 5], nxt[:, :, 0], nxt[:, :, 1]], axis=2).reshape(n_layers, n_seq, 3, 3, d)
    modp_all = jnp.pad(mod9[:, :batch], ((0, 0), (0, 0), (0, 0), (0, SUBLANES - 3), (0, 0)))
    mods_all = jnp.repeat(mod9[:, batch:], ts, axis=1).transpose(0, 2, 3, 1, 4)
    mods_all = jnp.pad(mods_all, ((0, 0), (0, 0), (0, 0), (0, rows.sample_pad - n_s), (0, 0)))

    w_in_t = w_in.transpose(0, 2, 1)
    o1 = pool_width
    o2 = o1 + q_width
    o3 = o2 + kv_width
    o4 = o3 + gn_width
    w_gm_t = w_in_t[:, o4:]
    w_gn_t = jnp.pad(w_in_t[:, o3:o4], ((0, 0), (0, -gn_width % LANES), (0, 0)))
    w_out_bf = w_out.astype(BF16)
    n_rt = N_GROUPS + N_EXPERTS
    w_router_t = jnp.concatenate([w_rg.transpose(0, 2, 1), w_re.transpose(0, 2, 1),
                                  jnp.zeros((n_layers, LANES - n_rt, d), F32)], axis=1)
    b_router = jnp.concatenate([b_rg, b_re, jnp.zeros((n_layers, LANES - n_rt), F32)], axis=1)
    cache_t = cache_kv.transpose(0, 1, 3, 4, 5, 2)
    win_t = cache_win.transpose(0, 1, 3, 4, 5, 2)
    state_pm = state_pool.transpose(0, 2, 1, 3)
    w2_t = cmp_w2.transpose(0, 1, 3, 2)

    slopes = jnp.exp2(-8.0 * jnp.arange(1, N_HEADS + 1, dtype=F32) / N_HEADS)
    nch_p = seq // CMP_STRIDE
    ncmp_p = (seq - CMP_LEN) // CMP_STRIDE + 1
    nsel_p = -(-seq // SEL_BLOCK)
    cover_p_t = _cover_matrix(nch_p, ncmp_p, nsel_p, -(-nsel_p // LANES) * LANES).T

    tk_s = past + ts
    ncmp_s = (tk_s - CMP_LEN) // CMP_STRIDE + 1
    nch_s = past // CMP_STRIDE
    assert (ncmp_s + CMP_SEGS - 1) * CMP_STRIDE <= past and nch_s >= ncmp_s
    assert page % CMP_STRIDE == 0 and page % SEL_BLOCK == 0 and page % LANES == 0 and ts <= LANES
    nsel_s = -(-tk_s // SEL_BLOCK)
    nsel_s_pad = -(-nsel_s // LANES) * LANES
    cover_s = _cover_matrix(nch_s, ncmp_s, nsel_s, nsel_s_pad)
    nk_s = past + LANES
    expand_s = (jnp.arange(nsel_s_pad)[:, None] == jnp.arange(nk_s)[None, :] // SEL_BLOCK).astype(BF16)
    gt = KV_HEADS * ts
    slopes_gh = slopes.reshape(KV_HEADS, Q_PER_KV)
    slope_s = jnp.broadcast_to(slopes_gh.T[:, :, None], (Q_PER_KV, KV_HEADS, ts)).reshape(Q_PER_KV * gt, 1)
    qpos_s = jnp.broadcast_to(past + jnp.arange(ts, dtype=I32), (Q_PER_KV, KV_HEADS, ts)).reshape(Q_PER_KV * gt, 1)
    eye_g = jnp.eye(KV_HEADS, dtype=BF16)

    u = _modulate(rows, x, modp_all[0], mods_all[0])
    outs = {k: [] for k in ("kv_p", "win_p", "pool_p", "kv_s", "win_s", "pool_s")}
    for l in range(n_layers):
        up = _proj_rows(u, w_in_t, l, 0, pool_width, MM_TILE_N, F32)
        q = _proj_rows(u, w_in_t, l, o1, q_width, MM_TILE_N, BF16, scale=HEAD_DIM ** -0.5)
        kvc = _proj_rows(u, w_in_t, l, o2, 2 * GRP_WIDTH, MM_TILE_N, F32)
        kv_rest_s = _proj_rows(u, w_in_t, l, o2 + 2 * GRP_WIDTH, 4 * GRP_WIDTH, MM_TILE_N, F32,
                               m0=n_p, m_rows=rows.sample_pad)
        gn = _proj_rows(u, w_gn_t, l, 0, LANES, LANES, F32, act="sigmoid")
        kv_t = _proj_t(u, w_in_t, l, o2, N_CACHED_PLANES, batch, seq)
        kw_t = _proj_t(u, w_in_t, l, o2 + N_CACHED_PLANES * GRP_WIDTH, 2, batch, seq)

        kv_s = jnp.concatenate([kvc[n_p:n_real], kv_rest_s[:n_s]], axis=1)
        kv_s = kv_s.reshape(nb, ts, N_KV_PLANES, KV_HEADS, HEAD_DIM)
        up_p = up[:n_p].reshape(batch, seq, pool_width)
        up_s = up[n_p:n_real].reshape(nb, ts, pool_width)
        outs["kv_p"].append(kv_t.reshape(batch, N_CACHED_PLANES, KV_HEADS, HEAD_DIM, seq))
        n_keep = min(WINDOW, seq)
        outs["win_p"].append(kw_t[:, :, :, seq - n_keep:].reshape(batch, 2, KV_HEADS, HEAD_DIM, n_keep))
        hist_p = jnp.concatenate([jnp.zeros((batch, POOL_HIST, pool_width), F32), up_p[:, -POOL_HIST:]], axis=1)
        outs["pool_p"].append(hist_p[:, -POOL_HIST:].transpose(1, 0, 2))
        new_t = kv_s.transpose(0, 2, 3, 4, 1)
        win_new = jnp.concatenate([win_t[l], new_t[:, N_CACHED_PLANES:]], axis=-1)[..., -win_buf:]
        outs["kv_s"].append(kv_s[:, :, :N_CACHED_PLANES])
        outs["win_s"].append(win_new)
        pool_ext = jnp.concatenate([state_pm[l], up_s.transpose(1, 0, 2)], axis=0)
        outs["pool_s"].append(pool_ext[-POOL_HIST:])

        yp_p = _pool_prompt(up, batch, seq, pool_w[l], pool_scale[l])
        yp_s = _pool_sample(pool_ext, ts, past, pool_w[l], pool_scale[l])
        y_pool = jnp.concatenate([yp_p, yp_s.transpose(1, 0, 2).reshape(n_s, pool_width),
                                  jnp.zeros((rows.sample_pad - n_s, pool_width), BF16)], axis=0)

        w1 = cmp_w1[l].reshape(2, CMP_SEGS, chunk_w, cmp_hid)
        pe = cmp_pe[l].reshape(2, CMP_SEGS, 1, chunk_w)
        chunks_p = kvc[:n_p].reshape(batch, nch_p, CMP_STRIDE, 2, KV_HEADS, HEAD_DIM)
        chunks_p = chunks_p.transpose(3, 0, 4, 1, 2, 5).reshape(2, batch, KV_HEADS, nch_p, chunk_w)
        cmp_rows_p, cmp_t_p = _compress_prompt(chunks_p, w1, pe, w2_t[l])
        cmp_s = _compress_sample(page_table, cache_t, l, w1, pe, w2_t[l])

        gates_p = gn[:n_p, :gn_width].reshape(batch, seq, KV_HEADS, 3 * Q_PER_KV).transpose(0, 2, 3, 1)
        ya_p = _attn_prompt(slopes, q, batch, seq, cmp_rows_p, cmp_t_p, kv_t, kw_t, gates_p, cover_p_t)

        q_s = q[n_p:n_real].reshape(nb, ts, KV_HEADS, Q_PER_KV, HEAD_DIM).transpose(0, 3, 2, 1, 4)
        qbd = (q_s[:, :, :, :, None, :] * eye_g[None, None, :, None, :, None]).reshape(nb, Q_PER_KV * gt, GRP_WIDTH)
        new_blk = jnp.pad(new_t[:, 2:].reshape(nb, 4, GRP_WIDTH, ts), ((0, 0), (0, 0), (0, 0), (0, LANES - ts)))
        gates_s = gn[n_p:n_real, :gn_width].reshape(nb, ts, KV_HEADS, Q_PER_KV, 3).transpose(0, 3, 2, 1, 4)
        gates_s = gates_s.reshape(nb, Q_PER_KV * gt, 3)
        oa_s = _attn_sample(page_table, qbd, cmp_s, cache_t, new_blk, win_t, gates_s,
                            slope_s, qpos_s, cover_s, expand_s, layer=l, past=past, ts=ts, n_cmp=ncmp_s,
                            n_sel=nsel_s, gt=gt)
        ya_s = oa_s.reshape(nb, Q_PER_KV, KV_HEADS, ts, HEAD_DIM).transpose(0, 3, 2, 1, 4).reshape(n_s, q_width)
        y_attn = jnp.concatenate([ya_p, ya_s.astype(BF16),
                                  jnp.zeros((rows.sample_pad - n_s, q_width), BF16)], axis=0)

        merged = _merge(u, y_pool, y_attn, w_gm_t, w_up_pool, w_up_nsa, l)
        x1, u2, route = _outproj(rows, merged, x, w_out_bf[l], modp_all[l], mods_all[l], ln1_g[l], ln1_b[l],
                                 w_router_t[l], b_router[l].reshape(1, LANES), alpha)

        pos, pads, tile_e, n_used, r_pad = _moe_dispatch(route, n_real, n_rows)
        xs = _moe_scatter(pos, pads, u2, n_real, r_pad)
        y_rows = _moe_experts(tile_e, n_used, xs, w_gate, w_up, w_down, l)
        x, u = _moe_combine(rows, pos, y_rows, x1, route, modp_all[l], mods_all[l], ln2_g[l], ln2_b[l], alpha,
                            emit_next=l + 1 < n_layers)

    y_prompt = x[:n_p].reshape(batch, seq, d)
    y_sample = x[n_p:n_real].reshape(nb, ts, d)
    new_kv_p = jnp.stack(outs["kv_p"]).transpose(0, 1, 5, 2, 3, 4)
    new_win_p = jnp.stack(outs["win_p"]).transpose(0, 1, 5, 2, 3, 4)
    new_pool_p = jnp.stack(outs["pool_p"]).transpose(0, 2, 1, 3)
    new_win_s = jnp.stack(outs["win_s"]).transpose(0, 1, 5, 2, 3, 4)
    new_pool_s = jnp.stack(outs["pool_s"]).transpose(0, 2, 1, 3)
    return (y_prompt, y_sample, new_kv_p, new_win_p, new_pool_p, jnp.stack(outs["kv_s"]), new_win_s, new_pool_s)
```

```python
import functools

import jax
import jax.numpy as jnp
from jax import lax
from jax.experimental import pallas as pl
from jax.experimental.pallas import tpu as pltpu

F32 = jnp.float32
BF16 = jnp.bfloat16
I32 = jnp.int32

POOL_WINDOWS = (2, 4, 8, 16)
POOL_HIST = max(POOL_WINDOWS) - 1
N_HEADS = 16
KV_HEADS = 4
HEAD_DIM = 64
Q_PER_KV = N_HEADS // KV_HEADS
GRP_WIDTH = KV_HEADS * HEAD_DIM
CMP_LEN = 32
CMP_STRIDE = 16
CMP_SEGS = CMP_LEN // CMP_STRIDE
SEL_BLOCK = 64
SEL_TOPK = 16
WINDOW = 512
Q_BLOCK = 256
N_KV_PLANES = 6
N_CACHED_PLANES = 4
FORCE_BONUS = 1.0e4
N_GROUPS = 4
EXPERTS_PER_GROUP = 4
N_EXPERTS = N_GROUPS * EXPERTS_PER_GROUP
LN_EPS = 1e-5
NEG_INF = -1e30
NEG_CLAMP = -1e29

LANES = 128
SUBLANES = 8
VMEM_LIMIT = 52 * 1024 * 1024

ROW_TILE = 256
MM_TILE_M = 512
MM_TILE_N = 512
SEL_KV_TILE = 512
EXPERT_TILE = 256
CMP_K_GROUP = 4


def _cparams(*sem):
    return pltpu.CompilerParams(dimension_semantics=sem, vmem_limit_bytes=VMEM_LIMIT)


def _dot(a, b):
    return jnp.dot(a, b, preferred_element_type=F32)


def _dot_nt(a, b):
    return lax.dot_general(a, b, (((1,), (1,)), ((), ())), preferred_element_type=F32)


def _pv_t(v_t, p):
    return _dot_nt(v_t, p).T


def _split_bf16(a):
    hi = a.astype(BF16)
    return hi, (a - hi.astype(F32)).astype(BF16)


def _masked_softmax(s, valid):
    s = jnp.where(valid, s, NEG_INF)
    m = jnp.max(s, axis=-1, keepdims=True)
    e = jnp.where(valid, jnp.exp(s - m), 0.0)
    l = jnp.sum(e, axis=-1, keepdims=True)
    return e / jnp.where(l > 0.0, l, 1.0)


def _softmax_addmask(z):
    m = jnp.maximum(jnp.max(z, axis=-1, keepdims=True), NEG_CLAMP)
    e = jnp.exp(z - m)
    l = jnp.sum(e, axis=-1, keepdims=True)
    return e, jnp.where(l > 0.0, l, 1.0)


def _layer_norm(x, g, b):
    mu = jnp.mean(x, axis=-1, keepdims=True)
    xc = x - mu
    var = jnp.mean(xc * xc, axis=-1, keepdims=True)
    return xc * lax.rsqrt(var + LN_EPS) * g + b


def _top_k_mask(score, n_top):
    col = lax.broadcasted_iota(I32, score.shape, 1)
    big = jnp.int32(score.shape[1])

    def body(_, carry):
        sc, sel = carry
        m = jnp.max(sc, axis=-1, keepdims=True)
        idx = jnp.min(jnp.where(sc == m, col, big), axis=-1, keepdims=True)
        hit = col == idx
        return jnp.where(hit, -jnp.inf, sc), jnp.where(hit, 1.0, sel)

    _, sel = lax.fori_loop(0, n_top, body, (score, jnp.zeros(score.shape, F32)))
    return sel


def _block_scores(imp, blk, qpos, n_sel):
    cur = qpos // SEL_BLOCK
    forced = (blk == 0) | (blk == cur) | (blk == cur - 1)
    visible = blk * SEL_BLOCK <= qpos
    score = jnp.where(visible, imp + jnp.where(forced, FORCE_BONUS, 0.0), -1.0)
    return jnp.where(blk < n_sel, score, -2.0)


def _adaln_kernel(c_ref, w_ref, b_ref, o_ref):
    c = c_ref[...]
    a = (c * jax.nn.sigmoid(c)).astype(BF16)
    o_ref[0] = _dot(a, w_ref[0].astype(BF16)) + b_ref[0]


def _adaln(c_all, w_ada, b_ada):
    n_layers, d, n6 = w_ada.shape
    mp = c_all.shape[0]
    tn = 1024
    assert n6 % tn == 0
    return pl.pallas_call(
        _adaln_kernel,
        grid=(n_layers, n6 // tn),
        in_specs=[pl.BlockSpec((mp, d), lambda l, n: (0, 0)),
                  pl.BlockSpec((1, d, tn), lambda l, n: (l, 0, n)),
                  pl.BlockSpec((1, 1, tn), lambda l, n: (l, 0, n))],
        out_specs=pl.BlockSpec((1, mp, tn), lambda l, n: (l, 0, n)),
        out_shape=jax.ShapeDtypeStruct((n_layers, mp, n6), F32),
        compiler_params=_cparams("arbitrary", "arbitrary"),
        name="adaln",
    )(c_all, w_ada, b_ada.reshape(n_layers, 1, n6))


class _Rows:
    def __init__(self, batch, seq, n_sample):
        self.batch = batch
        self.seq = seq
        self.n_prompt = batch * seq
        self.n_sample = n_sample
        self.sample_pad = -(-n_sample // MM_TILE_M) * MM_TILE_M
        self.n_rows = self.n_prompt + self.sample_pad
        assert seq % MM_TILE_M == 0 and MM_TILE_M % ROW_TILE == 0
        self.n_ptiles = self.n_prompt // ROW_TILE

    def mod_specs(self, d, group):
        tiles_per_seq = self.seq // ROW_TILE
        last = self.batch - 1
        n_pt = self.n_ptiles
        return [pl.BlockSpec((1, 1, SUBLANES, d),
                             lambda i, *_: (jnp.minimum(i // tiles_per_seq, last), group, 0, 0)),
                pl.BlockSpec((1, 3, ROW_TILE, d),
                             lambda i, *_: (group, 0, jnp.maximum(i - n_pt, 0), 0))]


def _mod_row(is_sample, mp_ref, ms_ref, k):
    return jnp.where(is_sample, ms_ref[0, k], mp_ref[0, 0, k:k + 1, :])


def _modulate_kernel(x_ref, mp_ref, ms_ref, u_ref, *, n_ptiles):
    is_s = pl.program_id(0) >= n_ptiles
    shift = _mod_row(is_s, mp_ref, ms_ref, 0)
    scale = _mod_row(is_s, mp_ref, ms_ref, 1)
    u_ref[...] = (x_ref[...] * (1.0 + scale) + shift).astype(BF16)


def _modulate(rows, x, modp, mods):
    d = x.shape[1]
    return pl.pallas_call(
        functools.partial(_modulate_kernel, n_ptiles=rows.n_ptiles),
        grid=(rows.n_rows // ROW_TILE,),
        in_specs=[pl.BlockSpec((ROW_TILE, d), lambda i: (i, 0))] + rows.mod_specs(d, 0),
        out_specs=pl.BlockSpec((ROW_TILE, d), lambda i: (i, 0)),
        out_shape=jax.ShapeDtypeStruct((rows.n_rows, d), BF16),
        compiler_params=_cparams("arbitrary"),
        name="modulate",
    )(x, modp, mods)


def _proj_rows_kernel(x_ref, w_ref, o_ref, wb_ref, *, act, scale):
    @pl.when(pl.program_id(1) == 0)
    def _():
        wb_ref[...] = w_ref[0].astype(BF16)

    y = _dot_nt(x_ref[...], wb_ref[...])
    if act == "sigmoid":
        y = jax.nn.sigmoid(y)
    if scale != 1.0:
        y = y * scale
    o_ref[...] = y.astype(o_ref.dtype)


def _proj_rows(x, w_t, layer, row0, ncols, tn, out_dtype, act=None, scale=1.0, m0=0, m_rows=None):
    k = x.shape[1]
    m_rows = x.shape[0] - m0 if m_rows is None else m_rows
    assert m0 % MM_TILE_M == 0 and m_rows % MM_TILE_M == 0 and row0 % tn == 0 and ncols % tn == 0
    mb, rb = m0 // MM_TILE_M, row0 // tn
    return pl.pallas_call(
        functools.partial(_proj_rows_kernel, act=act, scale=scale),
        grid=(ncols // tn, m_rows // MM_TILE_M),
        in_specs=[pl.BlockSpec((MM_TILE_M, k), lambda n, i: (mb + i, 0)),
                  pl.BlockSpec((1, tn, k), lambda n, i: (layer, rb + n, 0))],
        out_specs=pl.BlockSpec((MM_TILE_M, tn), lambda n, i: (i, n)),
        out_shape=jax.ShapeDtypeStruct((m_rows, ncols), out_dtype),
        scratch_shapes=[pltpu.VMEM((tn, k), BF16)],
        compiler_params=_cparams("arbitrary", "arbitrary"),
        name="proj_rows",
    )(x, w_t)


def _proj_t_kernel(w_ref, x_ref, o_ref, wb_ref):
    @pl.when((pl.program_id(1) == 0) & (pl.program_id(2) == 0))
    def _():
        wb_ref[...] = w_ref[0].astype(BF16)

    o_ref[0, 0] = _dot_nt(wb_ref[...], x_ref[...])


def _proj_t(x, w_t, layer, row0, n_planes, batch, seq):
    k = x.shape[1]
    tt = 2 * MM_TILE_M if seq % (2 * MM_TILE_M) == 0 else MM_TILE_M
    assert row0 % GRP_WIDTH == 0 and seq % tt == 0
    rb = row0 // GRP_WIDTH
    nt = seq // tt
    return pl.pallas_call(
        _proj_t_kernel,
        grid=(n_planes, batch, nt),
        in_specs=[pl.BlockSpec((1, GRP_WIDTH, k), lambda p, b, i: (layer, rb + p, 0)),
                  pl.BlockSpec((tt, k), lambda p, b, i: (b * nt + i, 0))],
        out_specs=pl.BlockSpec((1, 1, GRP_WIDTH, tt), lambda p, b, i: (b, p, 0, i)),
        out_shape=jax.ShapeDtypeStruct((batch, n_planes, GRP_WIDTH, seq), F32),
        scratch_shapes=[pltpu.VMEM((GRP_WIDTH, k), BF16)],
        compiler_params=_cparams("arbitrary", "arbitrary", "arbitrary"),
        name="proj_t",
    )(w_t, x)


def _pool_prompt_kernel(cur_ref, prev_ref, pw_ref, sc_ref, o_ref, ext_ref, *, tt, pg):
    i = pl.program_id(1)
    halo = 2 * SUBLANES
    ext_ref[0:halo, :] = jnp.where(i == 0, 0.0, prev_ref[...])
    ext_ref[halo:, :] = cur_ref[...]
    pos = i * tt + lax.broadcasted_iota(I32, (tt, 1), 0)
    for g, w in enumerate(POOL_WINDOWS):
        cols = slice(g * pg, (g + 1) * pg)
        cur = ext_ref[halo:halo + tt, cols]
        acc = cur
        for j in range(1, w):
            acc = acc + ext_ref[halo - j:halo - j + tt, cols]
        count = jnp.minimum(pos + 1, w).astype(F32)
        dlt = acc / count - cur
        y = _dot(dlt.astype(BF16), pw_ref[g].astype(BF16)) * sc_ref[:, cols]
        o_ref[:, cols] = y.astype(o_ref.dtype)


def _pool_prompt(up, batch, seq, pool_w, pool_scale):
    pw = up.shape[1]
    pg = pw // len(POOL_WINDOWS)
    tt = MM_TILE_M
    halo = 2 * SUBLANES
    assert POOL_HIST <= halo and seq % tt == 0
    nt = seq // tt
    return pl.pallas_call(
        functools.partial(_pool_prompt_kernel, tt=tt, pg=pg),
        grid=(batch, nt),
        in_specs=[pl.BlockSpec((tt, pw), lambda b, i: (b * nt + i, 0)),
                  pl.BlockSpec((halo, pw), lambda b, i: (jnp.maximum((b * nt + i) * (tt // halo) - 1, 0), 0)),
                  pl.BlockSpec((len(POOL_WINDOWS), pg, pg), lambda b, i: (0, 0, 0)),
                  pl.BlockSpec((1, pw), lambda b, i: (0, 0))],
        out_specs=pl.BlockSpec((tt, pw), lambda b, i: (b * nt + i, 0)),
        out_shape=jax.ShapeDtypeStruct((batch * seq, pw), BF16),
        scratch_shapes=[pltpu.VMEM((halo + tt, pw), F32)],
        compiler_params=_cparams("arbitrary", "arbitrary"),
        name="pool_prompt",
    )(up, up, pool_w, pool_scale.reshape(1, pw))


def _pool_sample_kernel(ext_ref, pw_ref, sc_ref, o_ref, *, ts, pg, pos0):
    for t in range(ts):
        for g, w in enumerate(POOL_WINDOWS):
            cols = slice(g * pg, (g + 1) * pg)
            cur = ext_ref[POOL_HIST + t, :, cols]
            acc = cur
            for j in range(1, w):
                acc = acc + ext_ref[POOL_HIST + t - j, :, cols]
            count = float(min(pos0 + t + 1, w))
            dlt = acc / count - cur
            y = _dot(dlt.astype(BF16), pw_ref[g].astype(BF16)) * sc_ref[:, cols]
            o_ref[t, :, cols] = y.astype(o_ref.dtype)


def _pool_sample(ext, ts, pos0, pool_w, pool_scale):
    n_ext, nb, pw = ext.shape
    pg = pw // len(POOL_WINDOWS)
    return pl.pallas_call(
        functools.partial(_pool_sample_kernel, ts=ts, pg=pg, pos0=pos0),
        grid=(1,),
        in_specs=[pl.BlockSpec((n_ext, nb, pw), lambda i: (0, 0, 0)),
                  pl.BlockSpec((len(POOL_WINDOWS), pg, pg), lambda i: (0, 0, 0)),
                  pl.BlockSpec((1, pw), lambda i: (0, 0))],
        out_specs=pl.BlockSpec((ts, nb, pw), lambda i: (0, 0, 0)),
        out_shape=jax.ShapeDtypeStruct((ts, nb, pw), BF16),
        compiler_params=_cparams("arbitrary"),
        name="pool_sample",
    )(ext, pool_w, pool_scale.reshape(1, pw))


def _compress_tail(ha, hb, w2t_ref):
    n_chunk = ha.shape[0]
    h = ha + pltpu.roll(hb, n_chunk - 1, 0)
    act = jax.nn.gelu(h).astype(BF16)
    w2t = w2t_ref[0].astype(BF16)
    return _dot_nt(act, w2t), _dot_nt(w2t, act)


def _compress_prompt_kernel(c_ref, w1_ref, pe_ref, w2t_ref, o_ref, ot_ref):
    c = c_ref[0, 0, 0]
    ha = _dot((c + pe_ref[0, 0]).astype(BF16), w1_ref[0, 0].astype(BF16))
    hb = _dot((c + pe_ref[0, 1]).astype(BF16), w1_ref[0, 1].astype(BF16))
    o_ref[0, 0, 0], ot_ref[0, 0, 0] = _compress_tail(ha, hb, w2t_ref)


def _compress_prompt(chunks, w1, pe, w2t):
    _, b, g, n_chunk, kdim = chunks.shape
    hid = w1.shape[-1]
    return pl.pallas_call(
        _compress_prompt_kernel,
        grid=(2, b, g),
        in_specs=[pl.BlockSpec((1, 1, 1, n_chunk, kdim), lambda p, bi, gi: (p, bi, gi, 0, 0)),
                  pl.BlockSpec((1, CMP_SEGS, kdim, hid), lambda p, bi, gi: (p, 0, 0, 0)),
                  pl.BlockSpec((1, CMP_SEGS, 1, kdim), lambda p, bi, gi: (p, 0, 0, 0)),
                  pl.BlockSpec((1, HEAD_DIM, hid), lambda p, bi, gi: (p, 0, 0))],
        out_specs=[pl.BlockSpec((1, 1, 1, n_chunk, HEAD_DIM), lambda p, bi, gi: (p, bi, gi, 0, 0)),
                   pl.BlockSpec((1, 1, 1, HEAD_DIM, n_chunk), lambda p, bi, gi: (p, bi, gi, 0, 0))],
        out_shape=[jax.ShapeDtypeStruct((2, b, g, n_chunk, HEAD_DIM), F32),
                   jax.ShapeDtypeStruct((2, b, g, HEAD_DIM, n_chunk), F32)],
        compiler_params=_cparams("arbitrary", "arbitrary", "arbitrary"),
        name="compress_prompt",
    )(chunks, w1, pe, w2t)


def _compress_sample_kernel(pt_ref, cache_ref, w1_ref, pe_ref, w2t_ref, perm_ref, o_ref, buf_ref, rows_ref,
                            w1b_ref, peb_ref, sem, *, layer, n_pages, page, nb):
    p = pl.program_id(0)
    b = pl.program_id(1)
    step = p * nb + b
    slot = step % 2
    n_chunk = n_pages * page // CMP_STRIDE
    kg = CMP_K_GROUP
    n_q = CMP_STRIDE // kg

    def page_copy(pp, bb, sl, j):
        pid = pt_ref[bb * n_pages + j]
        return pltpu.make_async_copy(cache_ref.at[layer, pid, pp], buf_ref.at[sl, j], sem.at[sl])

    def fetch(pp, bb, sl):
        def issue(j, c):
            page_copy(pp, bb, sl, j).start()
            return c

        lax.fori_loop(0, n_pages, issue, 0)

    @pl.when(step == 0)
    def _():
        fetch(p, b, slot)

    @pl.when(step + 1 < 2 * nb)
    def _():
        fetch((step + 1) // nb, (step + 1) % nb, 1 - slot)

    @pl.when(b == 0)
    def _():
        w1b_ref[...] = w1_ref[0].astype(BF16)
        bias = jnp.zeros(peb_ref.shape, F32)
        for m in range(CMP_SEGS):
            for q in range(n_q):
                pe_rows = jnp.broadcast_to(pe_ref[0, m, q], (SUBLANES, kg * HEAD_DIM)).astype(BF16)
                bias = bias + _dot(pe_rows, w1b_ref[m, q])
        peb_ref[...] = bias

    def drain(j, c):
        page_copy(p, b, slot, j).wait()
        return c

    lax.fori_loop(0, n_pages, drain, 0)
    cpp = page // CMP_STRIDE

    def to_rows(j, c):
        x_t = buf_ref[slot, j].reshape(GRP_WIDTH, page).astype(BF16)
        x = _dot_nt(perm_ref[...], x_t)
        c0 = pl.multiple_of(j * cpp, cpp)
        for s in range(CMP_STRIDE):
            rows_ref[s, pl.ds(c0, cpp), :] = x[s * cpp:(s + 1) * cpp, :]
        return c

    lax.fori_loop(0, n_pages, to_rows, 0, unroll=8)
    out = []
    for g in range(KV_HEADS):
        lanes = slice(g * HEAD_DIM, (g + 1) * HEAD_DIM)
        acc = [None] * CMP_SEGS
        for q in range(n_q):
            piece = jnp.concatenate([rows_ref[q * kg + k][:, lanes] for k in range(kg)], axis=1).astype(BF16)
            for m in range(CMP_SEGS):
                t = _dot(piece, w1b_ref[m, q])
                acc[m] = t if acc[m] is None else acc[m] + t
        out.append(_compress_tail(acc[0] + peb_ref[0:1, :], acc[1], w2t_ref)[0])
    o_ref[0, 0] = jnp.concatenate(out, axis=1)


def _compress_sample(page_table, cache_t, layer, w1, pe, w2t):
    nb, n_pages = page_table.shape
    page = cache_t.shape[-1]
    n_chunk = n_pages * page // CMP_STRIDE
    hid = w1.shape[-1]
    kg = CMP_K_GROUP
    w1g = w1.reshape(2, CMP_SEGS, CMP_STRIDE // kg, kg * HEAD_DIM, hid)
    peg = pe.reshape(2, CMP_SEGS, CMP_STRIDE // kg, 1, kg * HEAD_DIM)
    cpp = page // CMP_STRIDE
    row = jnp.arange(page)
    perm = ((CMP_STRIDE * (row % cpp) + row // cpp)[:, None] == jnp.arange(page)[None, :]).astype(BF16)
    assert page == LANES and KV_HEADS % 2 == 0
    grid_spec = pltpu.PrefetchScalarGridSpec(
        num_scalar_prefetch=1,
        grid=(2, nb),
        in_specs=[pl.BlockSpec(memory_space=pl.ANY),
                  pl.BlockSpec((1,) + w1g.shape[1:], lambda p, bi, pt: (p, 0, 0, 0, 0)),
                  pl.BlockSpec((1,) + peg.shape[1:], lambda p, bi, pt: (p, 0, 0, 0, 0)),
                  pl.BlockSpec((1, HEAD_DIM, hid), lambda p, bi, pt: (p, 0, 0)),
                  pl.BlockSpec((page, page), lambda p, bi, pt: (0, 0))],
        out_specs=pl.BlockSpec((1, 1, n_chunk, GRP_WIDTH), lambda p, bi, pt: (p, bi, 0, 0)),
        scratch_shapes=[pltpu.VMEM((2, n_pages, KV_HEADS, HEAD_DIM, page), F32),
                        pltpu.VMEM((CMP_STRIDE, n_chunk, GRP_WIDTH), F32),
                        pltpu.VMEM(w1g.shape[1:], BF16),
                        pltpu.VMEM((SUBLANES, hid), F32),
                        pltpu.SemaphoreType.DMA((2,))],
    )
    return pl.pallas_call(
        functools.partial(_compress_sample_kernel, layer=layer, n_pages=n_pages, page=page, nb=nb),
        grid_spec=grid_spec,
        out_shape=jax.ShapeDtypeStruct((2, nb, n_chunk, GRP_WIDTH), F32),
        compiler_params=_cparams("arbitrary", "arbitrary"),
        name="compress_sample",
    )(page_table.reshape(-1), cache_t, w1g, peg, w2t, perm)


def _attn_prompt_kernel(slope_ref, q_ref, kc_ref, vct_ref, ks_ref, vs_ref, kw_ref, vw_ref, gate_ref, covert_ref,
                        o_ref, m_ref, l_ref, acc_ref, st_ref, bias_ref, s_ref, z_ref, e_ref, flag_ref,
                        *, seq, n_cmp, n_sel, n_top, wlen):
    qb = Q_BLOCK
    tk = SEL_KV_TILE
    g = pl.program_id(1)
    p0 = pl.program_id(2) * qb
    qpos = p0 + lax.broadcasted_iota(I32, (1, qb), 1)
    slopes = [slope_ref[g * Q_PER_KV + h] for h in range(Q_PER_KV)]
    q_t = q_ref[...].astype(F32).T.astype(BF16)
    q_all = jnp.concatenate([q_t[h * HEAD_DIM:(h + 1) * HEAD_DIM] for h in range(Q_PER_KV)], axis=1)
    head = lambda a, h: a[:, h * qb:(h + 1) * qb]

    def col_softmax(z):
        m = jnp.maximum(jnp.max(z, axis=0, keepdims=True), NEG_CLAMP)
        e = jnp.exp(z - m)
        l = jnp.sum(e, axis=0, keepdims=True)
        return e, jnp.where(l > 0.0, l, 1.0)

    n_chunk = kc_ref.shape[3]
    crow = lax.broadcasted_iota(I32, (n_chunk, 1), 0)
    d_c = qpos - (crow * CMP_STRIDE + (CMP_LEN - 1))
    mask_c = jnp.where((d_c >= 0) & (crow < n_cmp), 0.0, NEG_INF)
    d_cf = d_c.astype(F32)
    vct = vct_ref[0, 0, 0].astype(BF16)
    s_c = _dot(kc_ref[0, 0, 0].astype(BF16), q_all)
    p_sum = jnp.zeros((n_chunk, qb), F32)
    o_c = []
    for h in range(Q_PER_KV):
        e, l = col_softmax(head(s_c, h) - slopes[h] * d_cf + mask_c)
        p = e / l
        p_sum = p_sum + p
        o_c.append(_dot(vct, p.astype(BF16)))

    p_hi, p_lo = _split_bf16(p_sum)
    cover_t = covert_ref[...]
    imp_t = _dot(cover_t, p_hi) + _dot(cover_t, p_lo)
    n_blk = -(-n_sel // SUBLANES) * SUBLANES
    blk = lax.broadcasted_iota(I32, (n_blk, 1), 0)
    score = _block_scores(imp_t[0:n_blk], blk, qpos, n_sel)
    st_ref[0:n_blk, :] = score

    def count_beaten(k, cnt):
        row = st_ref[pl.ds(k, 1), :]
        return cnt + jnp.where((row > score) | ((row == score) & (k < blk)), 1.0, 0.0)

    rank = lax.fori_loop(0, n_sel, count_beaten, jnp.zeros(score.shape, F32), unroll=4)
    chosen = rank < n_top
    st_ref[0:n_blk, :] = jnp.where(chosen, 0.0, NEG_INF)
    bpt = SEL_KV_TILE // SEL_BLOCK
    for t in range(n_blk // bpt):
        flag_ref[t] = jnp.max(jnp.where(chosen[t * bpt:(t + 1) * bpt], 1.0, 0.0)).astype(I32)

    krow = lax.broadcasted_iota(I32, (tk, 1), 0)
    @pl.when(pl.program_id(2) == 0)
    def _():
        rel = (krow - (qpos - p0)).astype(F32)
        for h in range(Q_PER_KV):
            bias_ref[h] = slopes[h] * rel
    m_ref[...] = jnp.full(m_ref.shape, NEG_INF, F32)
    l_ref[...] = jnp.zeros(l_ref.shape, F32)
    acc_ref[...] = jnp.zeros(acc_ref.shape, F32)
    n_tiles = (p0 + qb + tk - 1) // tk

    def sweep_tile(jt, causal):
        k0 = pl.multiple_of(jt * tk, tk)
        k_rows = ks_ref[0, 0, :, pl.ds(k0, tk)].T.astype(BF16)
        v_t = vs_ref[0, 0, :, pl.ds(k0, tk)].astype(BF16)
        s_ref[...] = _dot(k_rows, q_all)
        j0 = k0 // SEL_BLOCK
        off = (k0 - p0).astype(F32)
        nblk = tk // SEL_BLOCK
        fold = lambda a, op: op(a.reshape(SEL_BLOCK // SUBLANES, SUBLANES, qb), axis=0)
        for h in range(Q_PER_KV):
            mx = None
            for c in range(nblk):
                rows = slice(c * SEL_BLOCK, (c + 1) * SEL_BLOCK)
                madd = jnp.broadcast_to(st_ref[pl.ds(j0 + c, 1), :], (SEL_BLOCK, qb))
                if causal:
                    madd = jnp.where(k0 + krow[rows] <= qpos, madd, NEG_INF)
                z = s_ref[rows, h * qb:(h + 1) * qb] + bias_ref[h, rows, :] + madd
                z_ref[h, rows, :] = z
                cm = fold(z, jnp.max)
                mx = cm if mx is None else jnp.maximum(mx, cm)
            shift = slopes[h] * off
            m_old = m_ref[h]
            m_new = jnp.maximum(m_old, jnp.max(mx, axis=0, keepdims=True) + shift)
            m_use = jnp.maximum(m_new, NEG_CLAMP)
            sm = None
            for c in range(nblk):
                rows = slice(c * SEL_BLOCK, (c + 1) * SEL_BLOCK)
                e = jnp.exp(z_ref[h, rows, :] - (m_use - shift))
                e_ref[h, rows, :] = e.astype(BF16)
                cs = fold(e, jnp.sum)
                sm = cs if sm is None else sm + cs
            alpha = jnp.exp(m_old - m_use)
            l_ref[h] = alpha * l_ref[h] + jnp.sum(sm, axis=0, keepdims=True)
            acc_ref[h] = alpha * acc_ref[h] + _dot(v_t, e_ref[h])
            m_ref[h] = m_new

    def sweep_body(jt, carry):
        @pl.when(flag_ref[jt] > 0)
        def _():
            sweep_tile(jt, False)

        return carry

    lax.fori_loop(0, n_tiles - 1, sweep_body, 0)
    sweep_tile(n_tiles - 1, True)

    w0 = pl.multiple_of(jnp.minimum(jnp.maximum(p0 - WINDOW, 0), seq - wlen), qb)
    relw = (w0 + lax.broadcasted_iota(I32, (wlen, 1), 0)) - qpos
    mask_w = jnp.where((relw <= 0) & (relw > -WINDOW), 0.0, NEG_INF)
    relw_f = relw.astype(F32)
    kw_rows = kw_ref[0, 0, :, pl.ds(w0, wlen)].T.astype(BF16)
    vw_t = vw_ref[0, 0, :, pl.ds(w0, wlen)].astype(BF16)
    s_w = _dot(kw_rows, q_all)

    out = []
    for h in range(Q_PER_KV):
        e, l = col_softmax(head(s_w, h) + slopes[h] * relw_f + mask_w)
        o_w = _dot(vw_t, e.astype(BF16)) / l
        l_s = l_ref[h]
        o_s = acc_ref[h] / jnp.where(l_s > 0.0, l_s, 1.0)
        g0 = gate_ref[0, 0, 3 * h + 0:3 * h + 1, :]
        g1 = gate_ref[0, 0, 3 * h + 1:3 * h + 2, :]
        g2 = gate_ref[0, 0, 3 * h + 2:3 * h + 3, :]
        out.append(g0 * o_c[h] + g1 * o_s + g2 * o_w)
    o_ref[...] = jnp.concatenate(out, axis=0).T.astype(o_ref.dtype)


def _attn_prompt(slopes, q, batch, seq, cmp_rows, cmp_t, kv_t, win_t, gates_t, cover_t):
    n_chunk = cmp_rows.shape[3]
    n_cmp = (seq - CMP_LEN) // CMP_STRIDE + 1
    n_sel = -(-seq // SEL_BLOCK)
    n_top = min(SEL_TOPK, n_sel)
    wlen = min(WINDOW + Q_BLOCK, seq)
    nsp = cover_t.shape[0]
    nqb = seq // Q_BLOCK
    assert seq % SEL_KV_TILE == 0 and seq % Q_BLOCK == 0 and n_chunk >= n_cmp and wlen % LANES == 0

    def plane(pidx):
        return pl.BlockSpec((1, 1, HEAD_DIM, seq), lambda bi, gi, i: (bi, pidx, gi, 0))

    return pl.pallas_call(
        functools.partial(_attn_prompt_kernel, seq=seq, n_cmp=n_cmp, n_sel=n_sel, n_top=n_top, wlen=wlen),
        grid=(batch, KV_HEADS, nqb),
        in_specs=[pl.BlockSpec(memory_space=pltpu.SMEM),
                  pl.BlockSpec((Q_BLOCK, GRP_WIDTH), lambda bi, gi, i: (bi * nqb + i, gi)),
                  pl.BlockSpec((1, 1, 1, n_chunk, HEAD_DIM), lambda bi, gi, i: (0, bi, gi, 0, 0)),
                  pl.BlockSpec((1, 1, 1, HEAD_DIM, n_chunk), lambda bi, gi, i: (1, bi, gi, 0, 0)),
                  plane(2), plane(3), plane(0), plane(1),
                  pl.BlockSpec((1, 1, 3 * Q_PER_KV, Q_BLOCK), lambda bi, gi, i: (bi, gi, 0, i)),
                  pl.BlockSpec(cover_t.shape, lambda bi, gi, i: (0, 0))],
        out_specs=pl.BlockSpec((Q_BLOCK, GRP_WIDTH), lambda bi, gi, i: (bi * nqb + i, gi)),
        out_shape=jax.ShapeDtypeStruct((batch * seq, N_HEADS * HEAD_DIM), BF16),
        scratch_shapes=[pltpu.VMEM((Q_PER_KV, 1, Q_BLOCK), F32), pltpu.VMEM((Q_PER_KV, 1, Q_BLOCK), F32),
                        pltpu.VMEM((Q_PER_KV, HEAD_DIM, Q_BLOCK), F32), pltpu.VMEM((nsp, Q_BLOCK), F32),
                        pltpu.VMEM((Q_PER_KV, SEL_KV_TILE, Q_BLOCK), F32),
                        pltpu.VMEM((SEL_KV_TILE, Q_PER_KV * Q_BLOCK), F32),
                        pltpu.VMEM((Q_PER_KV, SEL_KV_TILE, Q_BLOCK), F32),
                        pltpu.VMEM((Q_PER_KV, SEL_KV_TILE, Q_BLOCK), BF16),
                        pltpu.SMEM((seq // SEL_KV_TILE,), I32)],
        compiler_params=_cparams("arbitrary", "arbitrary", "arbitrary"),
        name="attn_prompt",
    )(slopes, q, cmp_rows, cmp_t, kv_t, kv_t, win_t, win_t, gates_t, cover_t)


def _attn_sample_kernel(pt_ref, qbd_ref, kc_ref, vc_ref, cache_ref, new_ref, win_ref, gate_ref, slope_ref,
                        qpos_ref, cover_ref, expand_ref, o_ref, kt_ref, vt_ref, sem,
                        *, layer, n_pages, page, past, ts, n_cmp, n_sel, n_top, win_buf, gt):
    b = pl.program_id(0)
    nk = kt_ref.shape[2]

    def page_copies(j):
        pid = pt_ref[b * n_pages + j]
        dst = pl.ds(pl.multiple_of(j * page, page), page)
        return (pltpu.make_async_copy(cache_ref.at[layer, pid, 2], kt_ref.at[:, :, dst], sem),
                pltpu.make_async_copy(cache_ref.at[layer, pid, 3], vt_ref.at[:, :, dst], sem))

    def issue(j, c):
        for cp in page_copies(j):
            cp.start()
        return c

    def drain(j, c):
        for cp in page_copies(j):
            cp.wait()
        return c

    lax.fori_loop(0, n_pages, issue, 0)
    kt_ref[:, :, past:] = new_ref[0, 0].reshape(KV_HEADS, HEAD_DIM, nk - past)
    vt_ref[:, :, past:] = new_ref[0, 1].reshape(KV_HEADS, HEAD_DIM, nk - past)

    q = qbd_ref[0]
    nrow = q.shape[0]
    slope = slope_ref[...]
    qpos = qpos_ref[...]
    lane_grp = lax.broadcasted_iota(I32, (nrow, GRP_WIDTH), 1) // HEAD_DIM
    row_grp = (lax.broadcasted_iota(I32, (nrow, GRP_WIDTH), 0) % gt) // (gt // KV_HEADS)
    diag = lane_grp == row_grp

    def own_group(full):
        kept = jnp.where(diag, full, 0.0)
        out = kept[:, 0:HEAD_DIM]
        for g in range(1, KV_HEADS):
            out = out + kept[:, g * HEAD_DIM:(g + 1) * HEAD_DIM]
        return out

    n_chunk = kc_ref.shape[2]
    ccol = lax.broadcasted_iota(I32, (1, n_chunk), 1)
    d_c = qpos - (ccol * CMP_STRIDE + (CMP_LEN - 1))
    s_c = _dot_nt(q, kc_ref[0, 0].astype(BF16)) - slope * d_c.astype(F32)
    p_c = _masked_softmax(s_c, (d_c >= 0) & (ccol < n_cmp))
    o_c = own_group(_dot(p_c.astype(BF16), vc_ref[0, 0].astype(BF16)))

    p_sum = p_c[0:gt]
    for h in range(1, Q_PER_KV):
        p_sum = p_sum + p_c[h * gt:(h + 1) * gt]
    p_hi, p_lo = _split_bf16(p_sum)
    imp = _dot(p_hi, cover_ref[...]) + _dot(p_lo, cover_ref[...])
    scol = lax.broadcasted_iota(I32, (1, imp.shape[1]), 1)
    selmask = _top_k_mask(_block_scores(imp, scol, qpos[0:gt], n_sel), n_top).astype(BF16)

    n_newl = new_ref.shape[3]
    wcol = lax.broadcasted_iota(I32, (1, win_buf + n_newl), 1)
    kpos_w = jnp.where(wcol < win_buf, past - win_buf + wcol, past + wcol - win_buf)
    d_w = qpos - kpos_w
    valid_w = (d_w >= 0) & (d_w < WINDOW) & (wcol < win_buf + ts)
    kw_t = win_ref[0, 0, 0].reshape(GRP_WIDTH, win_buf).astype(BF16)
    vw_t = win_ref[0, 0, 1].reshape(GRP_WIDTH, win_buf).astype(BF16)
    s_w = jnp.concatenate([_dot(q, kw_t), _dot(q, new_ref[0, 2].astype(BF16))], axis=1)
    p_w = _masked_softmax(s_w - slope * d_w.astype(F32), valid_w).astype(BF16)
    o_w = own_group(_dot_nt(p_w[:, :win_buf], vw_t) + _dot_nt(p_w[:, win_buf:], new_ref[0, 3].astype(BF16)))

    lax.fori_loop(0, n_pages, drain, 0)
    selk = _dot(selmask, expand_ref[...])
    selk = jnp.concatenate([selk] * Q_PER_KV, axis=0)
    d_s = qpos - lax.broadcasted_iota(I32, (1, nk), 1)
    k_t = kt_ref[...].reshape(GRP_WIDTH, nk).astype(BF16)
    s_s = _dot(q, k_t) - slope * d_s.astype(F32)
    p_s = _masked_softmax(s_s, (selk > 0.5) & (d_s >= 0))
    v_t = vt_ref[...].reshape(GRP_WIDTH, nk).astype(BF16)
    o_s = own_group(_dot_nt(p_s.astype(BF16), v_t))

    gate = gate_ref[0]
    o_ref[0] = gate[:, 0:1] * o_c + gate[:, 1:2] * o_s + gate[:, 2:3] * o_w


def _attn_sample(page_table, qbd, cmp_kv, cache_t, new_t, win_t, gates, slope_rows, qpos_rows,
                 cover, expand, *, layer, past, ts, n_cmp, n_sel, gt):
    nb, n_pages = page_table.shape
    page = cache_t.shape[-1]
    win_buf = win_t.shape[-1]
    nrow = qbd.shape[1]
    nk = expand.shape[1]
    n_top = min(SEL_TOPK, n_sel)
    assert nk == past + new_t.shape[3] and win_buf % LANES == 0

    def per_seq(shape):
        return pl.BlockSpec((1,) + tuple(shape[1:]), lambda bi, pt: (bi,) + (0,) * (len(shape) - 1))

    def whole(shape):
        return pl.BlockSpec(tuple(shape), lambda bi, pt: (0,) * len(shape))

    grid_spec = pltpu.PrefetchScalarGridSpec(
        num_scalar_prefetch=1,
        grid=(nb,),
        in_specs=[per_seq(qbd.shape),
                  pl.BlockSpec((1, 1) + tuple(cmp_kv.shape[2:]), lambda bi, pt: (0, bi, 0, 0)),
                  pl.BlockSpec((1, 1) + tuple(cmp_kv.shape[2:]), lambda bi, pt: (1, bi, 0, 0)),
                  pl.BlockSpec(memory_space=pl.ANY), per_seq(new_t.shape),
                  pl.BlockSpec((1, 1) + tuple(win_t.shape[2:]), lambda bi, pt: (layer, bi, 0, 0, 0, 0)),
                  per_seq(gates.shape), whole(slope_rows.shape), whole(qpos_rows.shape),
                  whole(cover.shape), whole(expand.shape)],
        out_specs=pl.BlockSpec((1, nrow, HEAD_DIM), lambda bi, pt: (bi, 0, 0)),
        scratch_shapes=[pltpu.VMEM((KV_HEADS, HEAD_DIM, nk), F32), pltpu.VMEM((KV_HEADS, HEAD_DIM, nk), F32),
                        pltpu.SemaphoreType.DMA(())],
    )
    return pl.pallas_call(
        functools.partial(_attn_sample_kernel, layer=layer, n_pages=n_pages, page=page, past=past, ts=ts,
                          n_cmp=n_cmp, n_sel=n_sel, n_top=n_top, win_buf=win_buf, gt=gt),
        grid_spec=grid_spec,
        out_shape=jax.ShapeDtypeStruct((nb, nrow, HEAD_DIM), F32),
        compiler_params=_cparams("arbitrary"),
        name="attn_sample",
    )(page_table.reshape(-1), qbd, cmp_kv, cmp_kv, cache_t, new_t, win_t, gates, slope_rows, qpos_rows,
      cover, expand)


def _merge_kernel(u_ref, yp_ref, ya_ref, wgp_ref, wga_ref, wp_ref, wa_ref, o_ref, wgpb_ref, wgab_ref, wpb_ref,
                  wab_ref):
    @pl.when(pl.program_id(1) == 0)
    def _():
        wgpb_ref[...] = wgp_ref[0].astype(BF16)
        wgab_ref[...] = wga_ref[0].astype(BF16)
        wpb_ref[...] = wp_ref[0].astype(BF16)
        wab_ref[...] = wa_ref[0].astype(BF16)

    u = u_ref[...]
    g_pool = jax.nn.sigmoid(_dot_nt(u, wgpb_ref[...]))
    g_attn = jax.nn.sigmoid(_dot_nt(u, wgab_ref[...]))
    merged = g_pool * _dot(yp_ref[...], wpb_ref[...]) + g_attn * _dot(ya_ref[...], wab_ref[...])
    o_ref[...] = merged.astype(o_ref.dtype)


def _merge(u, y_pool, y_attn, w_gm_t, w_up_pool, w_up_nsa, layer):
    m, d = u.shape
    kp = y_pool.shape[1]
    ka = y_attn.shape[1]
    tn = MM_TILE_N
    nn = d // tn
    row = lambda w: pl.BlockSpec((MM_TILE_M, w), lambda n, i: (i, 0))
    return pl.pallas_call(
        _merge_kernel,
        grid=(nn, m // MM_TILE_M),
        in_specs=[row(d), row(kp), row(ka),
                  pl.BlockSpec((1, tn, d), lambda n, i: (layer, n, 0)),
                  pl.BlockSpec((1, tn, d), lambda n, i: (layer, nn + n, 0)),
                  pl.BlockSpec((1, kp, tn), lambda n, i: (layer, 0, n)),
                  pl.BlockSpec((1, ka, tn), lambda n, i: (layer, 0, n))],
        out_specs=pl.BlockSpec((MM_TILE_M, tn), lambda n, i: (i, n)),
        out_shape=jax.ShapeDtypeStruct((m, d), BF16),
        scratch_shapes=[pltpu.VMEM((tn, d), BF16), pltpu.VMEM((tn, d), BF16),
                        pltpu.VMEM((kp, tn), BF16), pltpu.VMEM((ka, tn), BF16)],
        compiler_params=_cparams("arbitrary", "arbitrary"),
        name="merge",
    )(u, y_pool, y_attn, w_gm_t, w_gm_t, w_up_pool, w_up_nsa)


def _route(logits):
    lane = lax.broadcasted_iota(I32, logits.shape, 1)
    is_grp = lane < N_GROUPS
    p_grp = _masked_softmax(logits, is_grp)
    p_top = jnp.max(p_grp, axis=-1, keepdims=True)
    grp = jnp.min(jnp.where(is_grp & (p_grp == p_top), lane, LANES), axis=-1, keepdims=True)
    first = N_GROUPS + grp * EXPERTS_PER_GROUP
    in_grp = (lane >= first) & (lane < first + EXPERTS_PER_GROUP)
    v = _masked_softmax(logits, in_grp)
    v0 = jnp.max(jnp.where(in_grp, v, -1.0), axis=-1, keepdims=True)
    i0 = jnp.min(jnp.where(in_grp & (v == v0), lane, LANES), axis=-1, keepdims=True)
    rest = in_grp & (lane != i0)
    v1 = jnp.max(jnp.where(rest, v, -1.0), axis=-1, keepdims=True)
    i1 = jnp.min(jnp.where(rest & (v == v1), lane, LANES), axis=-1, keepdims=True)
    tot = v0 + v1
    out = jnp.where(lane == 0, (i0 - N_GROUPS).astype(F32), 0.0)
    out = jnp.where(lane == 1, (i1 - N_GROUPS).astype(F32), out)
    out = jnp.where(lane == 2, p_top * (v0 / tot), out)
    out = jnp.where(lane == 3, p_top * (v1 / tot), out)
    return out


def _outproj_kernel(mg_ref, x_ref, w_ref, mp_ref, ms_ref, g_ref, b_ref, wr_ref, br_ref,
                    x1_ref, u_ref, rt_ref, *, n_ptiles, alpha):
    is_s = pl.program_id(0) >= n_ptiles
    gate1 = _mod_row(is_s, mp_ref, ms_ref, 0)
    shift2 = _mod_row(is_s, mp_ref, ms_ref, 1)
    scale2 = _mod_row(is_s, mp_ref, ms_ref, 2)
    mix = _dot(mg_ref[...], w_ref[...])
    x1 = _layer_norm(alpha * x_ref[...] + gate1 * mix, g_ref[...], b_ref[...])
    x1_ref[...] = x1
    u = x1 * (1.0 + scale2) + shift2
    u_ref[...] = u
    wr_hi, wr_lo = _split_bf16(wr_ref[...])
    u_hi, u_lo = _split_bf16(u)
    logits = _dot_nt(u_hi, wr_hi) + _dot_nt(u_lo, wr_hi) + _dot_nt(u_hi, wr_lo) + br_ref[...]
    rt_ref[...] = _route(logits)


def _outproj(rows, merged, x, w_out_bf, modp, mods, ln_g, ln_b, w_router_t, b_router, alpha):
    d = x.shape[1]
    row = pl.BlockSpec((ROW_TILE, d), lambda i: (i, 0))
    vec = pl.BlockSpec((1, d), lambda i: (0, 0))
    return pl.pallas_call(
        functools.partial(_outproj_kernel, n_ptiles=rows.n_ptiles, alpha=alpha),
        grid=(rows.n_rows // ROW_TILE,),
        in_specs=[row, row, pl.BlockSpec((d, d), lambda i: (0, 0))] + rows.mod_specs(d, 1)
        + [vec, vec, pl.BlockSpec((LANES, d), lambda i: (0, 0)), pl.BlockSpec((1, LANES), lambda i: (0, 0))],
        out_specs=[row, row, pl.BlockSpec((ROW_TILE, LANES), lambda i: (i, 0))],
        out_shape=[jax.ShapeDtypeStruct((rows.n_rows, d), F32),
                   jax.ShapeDtypeStruct((rows.n_rows, d), F32),
                   jax.ShapeDtypeStruct((rows.n_rows, LANES), F32)],
        compiler_params=_cparams("arbitrary"),
        name="outproj_ln_route",
    )(merged, x, w_out_bf, modp, mods, ln_g.reshape(1, d), ln_b.reshape(1, d), w_router_t, b_router)


def _moe_scatter_kernel(dest_ref, pad_ref, u_ref, xs_ref, zero_ref, sem, *, n_real):
    i = pl.program_id(0)
    tm = u_ref.shape[0]

    def pad_copy(r):
        return pltpu.make_async_copy(zero_ref, xs_ref.at[pl.ds(r, 1)], sem)

    @pl.when(i == 0)
    def _():
        zero_ref[...] = jnp.zeros(zero_ref.shape, zero_ref.dtype)
        for e in range(N_EXPERTS + 1):
            lo, hi = pad_ref[2 * e], pad_ref[2 * e + 1]

            def issue_pad(r, c):
                pad_copy(r).start()
                return c

            def drain_pad(r, c):
                pad_copy(r).wait()
                return c

            lax.fori_loop(lo, hi, issue_pad, 0)
            lax.fori_loop(lo, hi, drain_pad, 0)

    def row_copy(r, dst):
        return pltpu.make_async_copy(u_ref.at[pl.ds(r, 1)], xs_ref.at[pl.ds(dst, 1)], sem)

    def issue(r, c):
        base = 2 * (i * tm + r)
        row_copy(r, dest_ref[base]).start()
        row_copy(r, dest_ref[base + 1]).start()
        return c

    def drain(r, c):
        row_copy(r, 0).wait()
        row_copy(r, 0).wait()
        return c

    @pl.when((i + 1) * tm <= n_real)
    def _():
        lax.fori_loop(0, tm, issue, 0, unroll=8)
        lax.fori_loop(0, tm, drain, 0, unroll=8)

    @pl.when((i + 1) * tm > n_real)
    def _():
        lax.fori_loop(0, n_real - i * tm, issue, 0)
        lax.fori_loop(0, n_real - i * tm, drain, 0)


def _moe_scatter(dest, pads, u, n_real, r_pad):
    d = u.shape[1]
    tm = EXPERT_TILE
    grid_spec = pltpu.PrefetchScalarGridSpec(
        num_scalar_prefetch=2,
        grid=(-(-n_real // tm),),
        in_specs=[pl.BlockSpec((tm, d), lambda i, *_: (i, 0))],
        out_specs=pl.BlockSpec(memory_space=pl.ANY),
        scratch_shapes=[pltpu.VMEM((1, d), F32), pltpu.SemaphoreType.DMA(())],
    )
    return pl.pallas_call(
        functools.partial(_moe_scatter_kernel, n_real=n_real),
        grid_spec=grid_spec,
        out_shape=jax.ShapeDtypeStruct((r_pad, d), F32),
        compiler_params=_cparams("arbitrary"),
        name="moe_scatter",
    )(dest, pads, u)


def _moe_expert_kernel(te_ref, used_ref, x_ref, wg_ref, wu_ref, wd_ref, o_ref, wgb_ref, wub_ref, wdb_ref):
    t = pl.program_id(0)
    fresh = jnp.logical_or(t == 0, te_ref[t] != te_ref[jnp.maximum(t - 1, 0)])

    @pl.when(fresh)
    def _():
        wgb_ref[...] = wg_ref[0, 0].astype(BF16)
        wub_ref[...] = wu_ref[0, 0].astype(BF16)
        wdb_ref[...] = wd_ref[0, 0].astype(BF16)

    @pl.when(t < used_ref[0])
    def _():
        x = x_ref[...].astype(BF16)
        h = jax.nn.silu(_dot(x, wgb_ref[...])) * _dot(x, wub_ref[...])
        o_ref[...] = _dot(h.astype(BF16), wdb_ref[...])

    @pl.when(t >= used_ref[0])
    def _():
        o_ref[...] = jnp.zeros(o_ref.shape, o_ref.dtype)


def _moe_experts(tile_expert, n_used, xs, w_gate, w_up, w_down, layer):
    r_pad, d = xs.shape
    f = w_gate.shape[3]
    tm = EXPERT_TILE
    n_tiles = r_pad // tm

    def x_map(t, te, nu):
        return (jnp.minimum(t, jnp.maximum(nu[0] - 1, 0)), 0)

    grid_spec = pltpu.PrefetchScalarGridSpec(
        num_scalar_prefetch=2,
        grid=(n_tiles,),
        in_specs=[pl.BlockSpec((tm, d), x_map),
                  pl.BlockSpec((1, 1, d, f), lambda t, te, nu: (layer, te[t], 0, 0)),
                  pl.BlockSpec((1, 1, d, f), lambda t, te, nu: (layer, te[t], 0, 0)),
                  pl.BlockSpec((1, 1, f, d), lambda t, te, nu: (layer, te[t], 0, 0))],
        out_specs=pl.BlockSpec((tm, d), lambda t, te, nu: (t, 0)),
        scratch_shapes=[pltpu.VMEM((d, f), BF16), pltpu.VMEM((d, f), BF16), pltpu.VMEM((f, d), BF16)],
    )
    return pl.pallas_call(
        _moe_expert_kernel,
        grid_spec=grid_spec,
        out_shape=jax.ShapeDtypeStruct((r_pad, d), F32),
        compiler_params=_cparams("arbitrary"),
        name="moe_experts",
    )(tile_expert, n_used, xs, w_gate, w_up, w_down)


def _moe_combine_kernel(pos_ref, y_ref, x1_ref, rt_ref, mp_ref, ms_ref, g_ref, b_ref, x2_ref, *rest,
                        n_ptiles, alpha, emit_next):
    if emit_next:
        u_ref, buf_ref, sem = rest
    else:
        buf_ref, sem = rest
    i = pl.program_id(0)
    tm = x1_ref.shape[0]

    def row_copy(r, k, src):
        return pltpu.make_async_copy(y_ref.at[pl.ds(src, 1)], buf_ref.at[k, pl.ds(r, 1)], sem)

    def issue(r, c):
        base = 2 * (i * tm + r)
        row_copy(r, 0, pos_ref[base]).start()
        row_copy(r, 1, pos_ref[base + 1]).start()
        return c

    def drain(r, c):
        row_copy(r, 0, 0).wait()
        row_copy(r, 1, 0).wait()
        return c

    lax.fori_loop(0, tm, issue, 0, unroll=8)
    lax.fori_loop(0, tm, drain, 0, unroll=8)
    is_s = i >= n_ptiles
    gate2 = _mod_row(is_s, mp_ref, ms_ref, 0)
    ffn = rt_ref[:, 2:3] * buf_ref[0] + rt_ref[:, 3:4] * buf_ref[1]
    x2 = _layer_norm(alpha * x1_ref[...] + gate2 * ffn, g_ref[...], b_ref[...])
    x2_ref[...] = x2
    if emit_next:
        shift = _mod_row(is_s, mp_ref, ms_ref, 1)
        scale = _mod_row(is_s, mp_ref, ms_ref, 2)
        u_ref[...] = (x2 * (1.0 + scale) + shift).astype(u_ref.dtype)


def _moe_combine(rows, pos, y_rows, x1, route, modp, mods, ln_g, ln_b, alpha, emit_next):
    d = x1.shape[1]
    row = pl.BlockSpec((ROW_TILE, d), lambda i, *_: (i, 0))
    vec = pl.BlockSpec((1, d), lambda i, *_: (0, 0))
    out_specs = [row]
    out_shape = [jax.ShapeDtypeStruct((rows.n_rows, d), F32)]
    if emit_next:
        out_specs.append(row)
        out_shape.append(jax.ShapeDtypeStruct((rows.n_rows, d), BF16))
    grid_spec = pltpu.PrefetchScalarGridSpec(
        num_scalar_prefetch=1,
        grid=(rows.n_rows // ROW_TILE,),
        in_specs=[pl.BlockSpec(memory_space=pl.ANY), row, pl.BlockSpec((ROW_TILE, LANES), lambda i, *_: (i, 0))]
        + rows.mod_specs(d, 2) + [vec, vec],
        out_specs=out_specs,
        scratch_shapes=[pltpu.VMEM((2, ROW_TILE, d), F32), pltpu.SemaphoreType.DMA(())],
    )
    out = pl.pallas_call(
        functools.partial(_moe_combine_kernel, n_ptiles=rows.n_ptiles, alpha=alpha, emit_next=emit_next),
        grid_spec=grid_spec,
        out_shape=out_shape,
        compiler_params=_cparams("arbitrary"),
        name="moe_combine_ln",
    )(pos, y_rows, x1, route, modp, mods, ln_g.reshape(1, d), ln_b.reshape(1, d))
    return out if emit_next else (out[0], None)


def _moe_dispatch(route, n_real, n_rows):
    tm = EXPERT_TILE
    n_pairs = 2 * n_real
    n_tiles = -(-(n_pairs + N_EXPERTS * (tm - 1)) // tm)
    eid = route[:n_real, 0:2].astype(I32).reshape(n_pairs, 1)
    onehot = (eid == jnp.arange(N_EXPERTS, dtype=I32)[None, :]).astype(I32)
    seen = jnp.cumsum(onehot, axis=0)
    counts = seen[-1]
    padded = -(-counts // tm) * tm
    ends_pad = jnp.cumsum(padded)
    starts_pad = ends_pad - padded
    dest = jnp.sum(onehot * (starts_pad[None, :] + seen - 1), axis=1).astype(I32)
    pos = jnp.concatenate([dest, jnp.zeros((2 * (n_rows - n_real),), I32)])
    pads = jnp.stack([starts_pad + counts, ends_pad], axis=1).reshape(-1)
    pads = jnp.concatenate([pads, ends_pad[-1:], jnp.full((1,), n_tiles * tm)]).astype(I32)
    n_used = (ends_pad[-1] // tm).astype(I32)
    tile_start = jnp.arange(n_tiles, dtype=I32) * tm
    tile_e = jnp.minimum(jnp.sum((tile_start[:, None] >= ends_pad[None, :]).astype(I32), axis=1), N_EXPERTS - 1)
    last_e = jnp.sum(jnp.where(jnp.arange(n_tiles) == jnp.maximum(n_used - 1, 0), tile_e, 0))
    tile_e = jnp.where(jnp.arange(n_tiles) < n_used, tile_e, last_e).astype(I32)
    return pos, pads, tile_e, n_used.reshape(1), n_tiles * tm


def _cover_matrix(n_chunk, n_cmp, n_sel, n_sel_pad):
    ci = jnp.arange(n_chunk)[:, None]
    sj = jnp.arange(n_sel_pad)[None, :]
    hit = ((ci * CMP_STRIDE < sj * SEL_BLOCK + SEL_BLOCK)
           & (ci * CMP_STRIDE + CMP_LEN - 1 >= sj * SEL_BLOCK)
           & (ci < n_cmp) & (sj < n_sel))
    return hit.astype(BF16)


def kernel(x_prompt, x_sample, c_prompt, c_sample, cache_kv, cache_win, state_pool, page_table, w_ada, b_ada,
           w_in, cmp_w1, cmp_pe, cmp_w2, pool_w, pool_scale, w_up_pool, w_up_nsa, w_out, ln1_g, ln1_b, w_rg,
           b_rg, w_re, b_re, w_gate, w_up, w_down, ln2_g, ln2_b):
    batch, seq, d = x_prompt.shape
    nb, ts, _ = x_sample.shape
    n_layers = w_ada.shape[0]
    page = cache_kv.shape[2]
    n_pages = page_table.shape[1]
    past = n_pages * page
    win_buf = cache_win.shape[2]
    pool_width = pool_w.shape[1] * pool_w.shape[2]
    q_width = N_HEADS * HEAD_DIM
    kv_width = N_KV_PLANES * GRP_WIDTH
    gn_width = 3 * N_HEADS
    alpha = (2 * n_layers) ** 0.25
    cmp_hid = cmp_w1.shape[-1]
    chunk_w = CMP_STRIDE * HEAD_DIM

    rows = _Rows(batch, seq, nb * ts)
    n_p, n_s, n_rows = rows.n_prompt, rows.n_sample, rows.n_rows
    n_real = n_p + n_s

    x = jnp.concatenate([x_prompt.reshape(n_p, d), x_sample.reshape(n_s, d),
                         jnp.zeros((rows.sample_pad - n_s, d), F32)], axis=0)

    n_seq = batch + nb
    c_all = jnp.concatenate([c_prompt, c_sample, jnp.zeros((-n_seq % SUBLANES, d), F32)], axis=0)
    mod = _adaln(c_all, w_ada, b_ada)[:, :n_seq].reshape(n_layers, n_seq, 6, d)
    nxt = jnp.concatenate([mod[1:], jnp.zeros_like(mod[:1])], axis=0)
    mod9 = jnp.stack([mod[:, :, 0], mod[:, :, 1], jnp.zeros_like(mod[:, :, 0]),
                      mod[:, :, 2], mod[:, :, 3], mod[:, :, 4],
                      mod[:, :, 5], nxt[:, :, 0], nxt[:, :, 1]], axis=2).reshape(n_layers, n_seq, 3, 3, d)
    modp_all = jnp.pad(mod9[:, :batch], ((0, 0), (0, 0), (0, 0), (0, SUBLANES - 3), (0, 0)))
    mods_all = jnp.repeat(mod9[:, batch:], ts, axis=1).transpose(0, 2, 3, 1, 4)
    mods_all = jnp.pad(mods_all, ((0, 0), (0, 0), (0, 0), (0, rows.sample_pad - n_s), (0, 0)))

    w_in_t = w_in.transpose(0, 2, 1)
    o1 = pool_width
    o2 = o1 + q_width
    o3 = o2 + kv_width
    o4 = o3 + gn_width
    w_gm_t = w_in_t[:, o4:]
    w_gn_t = jnp.pad(w_in_t[:, o3:o4], ((0, 0), (0, -gn_width % LANES), (0, 0)))
    w_out_bf = w_out.astype(BF16)
    n_rt = N_GROUPS + N_EXPERTS
    w_router_t = jnp.concatenate([w_rg.transpose(0, 2, 1), w_re.transpose(0, 2, 1),
                                  jnp.zeros((n_layers, LANES - n_rt, d), F32)], axis=1)
    b_router = jnp.concatenate([b_rg, b_re, jnp.zeros((n_layers, LANES - n_rt), F32)], axis=1)
    cache_t = cache_kv.transpose(0, 1, 3, 4, 5, 2)
    win_t = cache_win.transpose(0, 1, 3, 4, 5, 2)
    state_pm = state_pool.transpose(0, 2, 1, 3)
    w2_t = cmp_w2.transpose(0, 1, 3, 2)

    slopes = jnp.exp2(-8.0 * jnp.arange(1, N_HEADS + 1, dtype=F32) / N_HEADS)
    nch_p = seq // CMP_STRIDE
    ncmp_p = (seq - CMP_LEN) // CMP_STRIDE + 1
    nsel_p = -(-seq // SEL_BLOCK)
    cover_p_t = _cover_matrix(nch_p, ncmp_p, nsel_p, -(-nsel_p // LANES) * LANES).T

    tk_s = past + ts
    ncmp_s = (tk_s - CMP_LEN) // CMP_STRIDE + 1
    nch_s = past // CMP_STRIDE
    assert (ncmp_s + CMP_SEGS - 1) * CMP_STRIDE <= past and nch_s >= ncmp_s
    assert page % CMP_STRIDE == 0 and page % SEL_BLOCK == 0 and page % LANES == 0 and ts <= LANES
    nsel_s = -(-tk_s // SEL_BLOCK)
    nsel_s_pad = -(-nsel_s // LANES) * LANES
    cover_s = _cover_matrix(nch_s, ncmp_s, nsel_s, nsel_s_pad)
    nk_s = past + LANES
    expand_s = (jnp.arange(nsel_s_pad)[:, None] == jnp.arange(nk_s)[None, :] // SEL_BLOCK).astype(BF16)
    gt = KV_HEADS * ts
    slopes_gh = slopes.reshape(KV_HEADS, Q_PER_KV)
    slope_s = jnp.broadcast_to(slopes_gh.T[:, :, None], (Q_PER_KV, KV_HEADS, ts)).reshape(Q_PER_KV * gt, 1)
    qpos_s = jnp.broadcast_to(past + jnp.arange(ts, dtype=I32), (Q_PER_KV, KV_HEADS, ts)).reshape(Q_PER_KV * gt, 1)
    eye_g = jnp.eye(KV_HEADS, dtype=BF16)

    u = _modulate(rows, x, modp_all[0], mods_all[0])
    outs = {k: [] for k in ("kv_p", "win_p", "pool_p", "kv_s", "win_s", "pool_s")}
    for l in range(n_layers):
        up = _proj_rows(u, w_in_t, l, 0, pool_width, MM_TILE_N, F32)
        q = _proj_rows(u, w_in_t, l, o1, q_width, MM_TILE_N, BF16, scale=HEAD_DIM ** -0.5)
        kvc = _proj_rows(u, w_in_t, l, o2, 2 * GRP_WIDTH, MM_TILE_N, F32)
        kv_rest_s = _proj_rows(u, w_in_t, l, o2 + 2 * GRP_WIDTH, 4 * GRP_WIDTH, MM_TILE_N, F32,
                               m0=n_p, m_rows=rows.sample_pad)
        gn = _proj_rows(u, w_gn_t, l, 0, LANES, LANES, F32, act="sigmoid")
        kv_t = _proj_t(u, w_in_t, l, o2, N_CACHED_PLANES, batch, seq)
        kw_t = _proj_t(u, w_in_t, l, o2 + N_CACHED_PLANES * GRP_WIDTH, 2, batch, seq)

        kv_s = jnp.concatenate([kvc[n_p:n_real], kv_rest_s[:n_s]], axis=1)
        kv_s = kv_s.reshape(nb, ts, N_KV_PLANES, KV_HEADS, HEAD_DIM)
        up_p = up[:n_p].reshape(batch, seq, pool_width)
        up_s = up[n_p:n_real].reshape(nb, ts, pool_width)
        outs["kv_p"].append(kv_t.reshape(batch, N_CACHED_PLANES, KV_HEADS, HEAD_DIM, seq))
        n_keep = min(WINDOW, seq)
        outs["win_p"].append(kw_t[:, :, :, seq - n_keep:].reshape(batch, 2, KV_HEADS, HEAD_DIM, n_keep))
        hist_p = jnp.concatenate([jnp.zeros((batch, POOL_HIST, pool_width), F32), up_p[:, -POOL_HIST:]], axis=1)
        outs["pool_p"].append(hist_p[:, -POOL_HIST:].transpose(1, 0, 2))
        new_t = kv_s.transpose(0, 2, 3, 4, 1)
        win_new = jnp.concatenate([win_t[l], new_t[:, N_CACHED_PLANES:]], axis=-1)[..., -win_buf:]
        outs["kv_s"].append(kv_s[:, :, :N_CACHED_PLANES])
        outs["win_s"].append(win_new)
        pool_ext = jnp.concatenate([state_pm[l], up_s.transpose(1, 0, 2)], axis=0)
        outs["pool_s"].append(pool_ext[-POOL_HIST:])

        yp_p = _pool_prompt(up, batch, seq, pool_w[l], pool_scale[l])
        yp_s = _pool_sample(pool_ext, ts, past, pool_w[l], pool_scale[l])
        y_pool = jnp.concatenate([yp_p, yp_s.transpose(1, 0, 2).reshape(n_s, pool_width),
                                  jnp.zeros((rows.sample_pad - n_s, pool_width), BF16)], axis=0)

        w1 = cmp_w1[l].reshape(2, CMP_SEGS, chunk_w, cmp_hid)
        pe = cmp_pe[l].reshape(2, CMP_SEGS, 1, chunk_w)
        chunks_p = kvc[:n_p].reshape(batch, nch_p, CMP_STRIDE, 2, KV_HEADS, HEAD_DIM)
        chunks_p = chunks_p.transpose(3, 0, 4, 1, 2, 5).reshape(2, batch, KV_HEADS, nch_p, chunk_w)
        cmp_rows_p, cmp_t_p = _compress_prompt(chunks_p, w1, pe, w2_t[l])
        cmp_s = _compress_sample(page_table, cache_t, l, w1, pe, w2_t[l])

        gates_p = gn[:n_p, :gn_width].reshape(batch, seq, KV_HEADS, 3 * Q_PER_KV).transpose(0, 2, 3, 1)
        ya_p = _attn_prompt(slopes, q, batch, seq, cmp_rows_p, cmp_t_p, kv_t, kw_t, gates_p, cover_p_t)

        q_s = q[n_p:n_real].reshape(nb, ts, KV_HEADS, Q_PER_KV, HEAD_DIM).transpose(0, 3, 2, 1, 4)
        qbd = (q_s[:, :, :, :, None, :] * eye_g[None, None, :, None, :, None]).reshape(nb, Q_PER_KV * gt, GRP_WIDTH)
        new_blk = jnp.pad(new_t[:, 2:].reshape(nb, 4, GRP_WIDTH, ts), ((0, 0), (0, 0), (0, 0), (0, LANES - ts)))
        gates_s = gn[n_p:n_real, :gn_width].reshape(nb, ts, KV_HEADS, Q_PER_KV, 3).transpose(0, 3, 2, 1, 4)
        gates_s = gates_s.reshape(nb, Q_PER_KV * gt, 3)
        oa_s = _attn_sample(page_table, qbd, cmp_s, cache_t, new_blk, win_t, gates_s,
                            slope_s, qpos_s, cover_s, expand_s, layer=l, past=past, ts=ts, n_cmp=ncmp_s,
                            n_sel=nsel_s, gt=gt)
        ya_s = oa_s.reshape(nb, Q_PER_KV, KV_HEADS, ts, HEAD_DIM).transpose(0, 3, 2, 1, 4).reshape(n_s, q_width)
        y_attn = jnp.concatenate([ya_p, ya_s.astype(BF16),
                                  jnp.zeros((rows.sample_pad - n_s, q_width), BF16)], axis=0)

        merged = _merge(u, y_pool, y_attn, w_gm_t, w_up_pool, w_up_nsa, l)
        x1, u2, route = _outproj(rows, merged, x, w_out_bf[l], modp_all[l], mods_all[l], ln1_g[l], ln1_b[l],
                                 w_router_t[l], b_router[l].reshape(1, LANES), alpha)

        pos, pads, tile_e, n_used, r_pad = _moe_dispatch(route, n_real, n_rows)
        xs = _moe_scatter(pos, pads, u2, n_real, r_pad)
        y_rows = _moe_experts(tile_e, n_used, xs, w_gate, w_up, w_down, l)
        x, u = _moe_combine(rows, pos, y_rows, x1, route, modp_all[l], mods_all[l], ln2_g[l], ln2_b[l], alpha,
                            emit_next=l + 1 < n_layers)

    y_prompt = x[:n_p].reshape(batch, seq, d)
    y_sample = x[n_p:n_real].reshape(nb, ts, d)
    new_kv_p = jnp.stack(outs["kv_p"]).transpose(0, 1, 5, 2, 3, 4)
    new_win_p = jnp.stack(outs["win_p"]).transpose(0, 1, 5, 2, 3, 4)
    new_pool_p = jnp.stack(outs["pool_p"]).transpose(0, 2, 1, 3)
    new_win_s = jnp.stack(outs["win_s"]).transpose(0, 1, 5, 2, 3, 4)
    new_pool_s = jnp.stack(outs["pool_s"]).transpose(0, 2, 1, 3)
    return (y_prompt, y_sample, new_kv_p, new_win_p, new_pool_p, jnp.stack(outs["kv_s"]), new_win_s, new_pool_s)
```

```python
import functools

import jax
import jax.numpy as jnp
from jax import lax
from jax.experimental import pallas as pl
from jax.experimental.pallas import tpu as pltpu

F32 = jnp.float32
BF16 = jnp.bfloat16
I32 = jnp.int32

POOL_WINDOWS = (2, 4, 8, 16)
POOL_HIST = max(POOL_WINDOWS) - 1
N_HEADS = 16
KV_HEADS = 4
HEAD_DIM = 64
Q_PER_KV = N_HEADS // KV_HEADS
GRP_WIDTH = KV_HEADS * HEAD_DIM
CMP_LEN = 32
CMP_STRIDE = 16
CMP_SEGS = CMP_LEN // CMP_STRIDE
SEL_BLOCK = 64
SEL_TOPK = 16
WINDOW = 512
Q_BLOCK = 256
N_KV_PLANES = 6
N_CACHED_PLANES = 4
FORCE_BONUS = 1.0e4
N_GROUPS = 4
EXPERTS_PER_GROUP = 4
N_EXPERTS = N_GROUPS * EXPERTS_PER_GROUP
LN_EPS = 1e-5
NEG_INF = -1e30
NEG_CLAMP = -1e29

LANES = 128
SUBLANES = 8
VMEM_LIMIT = 52 * 1024 * 1024

ROW_TILE = 256
MM_TILE_M = 512
MM_TILE_N = 512
SEL_KV_TILE = 512
EXPERT_TILE = 256
CMP_K_GROUP = 4


def _cparams(*sem):
    return pltpu.CompilerParams(dimension_semantics=sem, vmem_limit_bytes=VMEM_LIMIT)


def _dot(a, b):
    return jnp.dot(a, b, preferred_element_type=F32)


def _dot_nt(a, b):
    return lax.dot_general(a, b, (((1,), (1,)), ((), ())), preferred_element_type=F32)


def _split_bf16(a):
    hi = a.astype(BF16)
    return hi, (a - hi.astype(F32)).astype(BF16)


def _masked_softmax(s, valid):
    s = jnp.where(valid, s, NEG_INF)
    m = jnp.max(s, axis=-1, keepdims=True)
    e = jnp.where(valid, jnp.exp(s - m), 0.0)
    l = jnp.sum(e, axis=-1, keepdims=True)
    return e / jnp.where(l > 0.0, l, 1.0)


def _layer_norm(x, g, b):
    mu = jnp.mean(x, axis=-1, keepdims=True)
    xc = x - mu
    var = jnp.mean(xc * xc, axis=-1, keepdims=True)
    return xc * lax.rsqrt(var + LN_EPS) * g + b


def _block_scores(imp, blk, qpos, n_sel):
    cur = qpos // SEL_BLOCK
    forced = (blk == 0) | (blk == cur) | (blk == cur - 1)
    visible = blk * SEL_BLOCK <= qpos
    score = jnp.where(visible, imp + jnp.where(forced, FORCE_BONUS, 0.0), -1.0)
    return jnp.where(blk < n_sel, score, -2.0)


def _adaln_kernel(c_ref, w_ref, b_ref, o_ref):
    c = c_ref[...]
    a = (c * jax.nn.sigmoid(c)).astype(BF16)
    o_ref[0] = _dot(a, w_ref[0].astype(BF16)) + b_ref[0]


def _adaln(c_all, w_ada, b_ada):
    n_layers, d, n6 = w_ada.shape
    mp = c_all.shape[0]
    tn = 1024
    assert n6 % tn == 0
    return pl.pallas_call(
        _adaln_kernel,
        grid=(n_layers, n6 // tn),
        in_specs=[pl.BlockSpec((mp, d), lambda l, n: (0, 0)),
                  pl.BlockSpec((1, d, tn), lambda l, n: (l, 0, n)),
                  pl.BlockSpec((1, 1, tn), lambda l, n: (l, 0, n))],
        out_specs=pl.BlockSpec((1, mp, tn), lambda l, n: (l, 0, n)),
        out_shape=jax.ShapeDtypeStruct((n_layers, mp, n6), F32),
        compiler_params=_cparams("arbitrary", "arbitrary"),
        name="adaln",
    )(c_all, w_ada, b_ada.reshape(n_layers, 1, n6))


class _Rows:
    def __init__(self, batch, seq, n_sample):
        self.batch = batch
        self.seq = seq
        self.n_prompt = batch * seq
        self.n_sample = n_sample
        self.sample_pad = -(-n_sample // MM_TILE_M) * MM_TILE_M
        self.n_rows = self.n_prompt + self.sample_pad
        assert seq % MM_TILE_M == 0 and MM_TILE_M % ROW_TILE == 0
        self.n_ptiles = self.n_prompt // ROW_TILE

    def mod_specs(self, d, group):
        tiles_per_seq = self.seq // ROW_TILE
        last = self.batch - 1
        n_pt = self.n_ptiles
        return [pl.BlockSpec((1, 1, SUBLANES, d),
                             lambda i, *_: (jnp.minimum(i // tiles_per_seq, last), group, 0, 0)),
                pl.BlockSpec((1, 3, ROW_TILE, d),
                             lambda i, *_: (group, 0, jnp.maximum(i - n_pt, 0), 0))]


def _mod_row(is_sample, mp_ref, ms_ref, k):
    return jnp.where(is_sample, ms_ref[0, k], mp_ref[0, 0, k:k + 1, :])


def _modulate_kernel(x_ref, mp_ref, ms_ref, u_ref, *, n_ptiles):
    is_s = pl.program_id(0) >= n_ptiles
    shift = _mod_row(is_s, mp_ref, ms_ref, 0)
    scale = _mod_row(is_s, mp_ref, ms_ref, 1)
    u_ref[...] = (x_ref[...] * (1.0 + scale) + shift).astype(BF16)


def _modulate(rows, x, modp, mods):
    d = x.shape[1]
    return pl.pallas_call(
        functools.partial(_modulate_kernel, n_ptiles=rows.n_ptiles),
        grid=(rows.n_rows // ROW_TILE,),
        in_specs=[pl.BlockSpec((ROW_TILE, d), lambda i: (i, 0))] + rows.mod_specs(d, 0),
        out_specs=pl.BlockSpec((ROW_TILE, d), lambda i: (i, 0)),
        out_shape=jax.ShapeDtypeStruct((rows.n_rows, d), BF16),
        compiler_params=_cparams("arbitrary"),
        name="modulate",
    )(x, modp, mods)


def _proj_rows_kernel(x_ref, w_ref, o_ref, wb_ref, *, act, scale):
    @pl.when(pl.program_id(1) == 0)
    def _():
        wb_ref[...] = w_ref[0].astype(BF16)

    y = _dot_nt(x_ref[...], wb_ref[...])
    if act == "sigmoid":
        y = jax.nn.sigmoid(y)
    if scale != 1.0:
        y = y * scale
    o_ref[...] = y.astype(o_ref.dtype)


def _proj_rows(x, w_t, layer, row0, ncols, tn, out_dtype, act=None, scale=1.0, m0=0, m_rows=None):
    k = x.shape[1]
    m_rows = x.shape[0] - m0 if m_rows is None else m_rows
    assert m0 % MM_TILE_M == 0 and m_rows % MM_TILE_M == 0 and row0 % tn == 0 and ncols % tn == 0
    mb, rb = m0 // MM_TILE_M, row0 // tn
    return pl.pallas_call(
        functools.partial(_proj_rows_kernel, act=act, scale=scale),
        grid=(ncols // tn, m_rows // MM_TILE_M),
        in_specs=[pl.BlockSpec((MM_TILE_M, k), lambda n, i: (mb + i, 0)),
                  pl.BlockSpec((1, tn, k), lambda n, i: (layer, rb + n, 0))],
        out_specs=pl.BlockSpec((MM_TILE_M, tn), lambda n, i: (i, n)),
        out_shape=jax.ShapeDtypeStruct((m_rows, ncols), out_dtype),
        scratch_shapes=[pltpu.VMEM((tn, k), BF16)],
        compiler_params=_cparams("arbitrary", "arbitrary"),
        name="proj_rows",
    )(x, w_t)


def _proj_t_kernel(w_ref, x_ref, o_ref, wb_ref):
    @pl.when((pl.program_id(1) == 0) & (pl.program_id(2) == 0))
    def _():
        wb_ref[...] = w_ref[0].astype(BF16)

    o_ref[0, 0] = _dot_nt(wb_ref[...], x_ref[...])


def _proj_t(x, w_t, layer, row0, n_planes, batch, seq):
    k = x.shape[1]
    tt = 2 * MM_TILE_M if seq % (2 * MM_TILE_M) == 0 else MM_TILE_M
    assert row0 % GRP_WIDTH == 0 and seq % tt == 0
    rb = row0 // GRP_WIDTH
    nt = seq // tt
    return pl.pallas_call(
        _proj_t_kernel,
        grid=(n_planes, batch, nt),
        in_specs=[pl.BlockSpec((1, GRP_WIDTH, k), lambda p, b, i: (layer, rb + p, 0)),
                  pl.BlockSpec((tt, k), lambda p, b, i: (b * nt + i, 0))],
        out_specs=pl.BlockSpec((1, 1, GRP_WIDTH, tt), lambda p, b, i: (b, p, 0, i)),
        out_shape=jax.ShapeDtypeStruct((batch, n_planes, GRP_WIDTH, seq), F32),
        scratch_shapes=[pltpu.VMEM((GRP_WIDTH, k), BF16)],
        compiler_params=_cparams("arbitrary", "arbitrary", "arbitrary"),
        name="proj_t",
    )(w_t, x)


def _pool_prompt_kernel(cur_ref, prev_ref, pw_ref, sc_ref, o_ref, ext_ref, *, tt, pg):
    i = pl.program_id(1)
    halo = 2 * SUBLANES
    ext_ref[0:halo, :] = jnp.where(i == 0, 0.0, prev_ref[...])
    ext_ref[halo:, :] = cur_ref[...]
    pos = i * tt + lax.broadcasted_iota(I32, (tt, 1), 0)
    for g, w in enumerate(POOL_WINDOWS):
        cols = slice(g * pg, (g + 1) * pg)
        cur = ext_ref[halo:halo + tt, cols]
        acc = cur
        for j in range(1, w):
            acc = acc + ext_ref[halo - j:halo - j + tt, cols]
        count = jnp.minimum(pos + 1, w).astype(F32)
        dlt = acc / count - cur
        y = _dot(dlt.astype(BF16), pw_ref[g].astype(BF16)) * sc_ref[:, cols]
        o_ref[:, cols] = y.astype(o_ref.dtype)


def _pool_prompt(up, batch, seq, pool_w, pool_scale):
    pw = up.shape[1]
    pg = pw // len(POOL_WINDOWS)
    tt = MM_TILE_M
    halo = 2 * SUBLANES
    assert POOL_HIST <= halo and seq % tt == 0
    nt = seq // tt
    return pl.pallas_call(
        functools.partial(_pool_prompt_kernel, tt=tt, pg=pg),
        grid=(batch, nt),
        in_specs=[pl.BlockSpec((tt, pw), lambda b, i: (b * nt + i, 0)),
                  pl.BlockSpec((halo, pw), lambda b, i: (jnp.maximum((b * nt + i) * (tt // halo) - 1, 0), 0)),
                  pl.BlockSpec((len(POOL_WINDOWS), pg, pg), lambda b, i: (0, 0, 0)),
                  pl.BlockSpec((1, pw), lambda b, i: (0, 0))],
        out_specs=pl.BlockSpec((tt, pw), lambda b, i: (b * nt + i, 0)),
        out_shape=jax.ShapeDtypeStruct((batch * seq, pw), BF16),
        scratch_shapes=[pltpu.VMEM((halo + tt, pw), F32)],
        compiler_params=_cparams("arbitrary", "arbitrary"),
        name="pool_prompt",
    )(up, up, pool_w, pool_scale.reshape(1, pw))


def _pool_sample_kernel(ext_ref, pw_ref, sc_ref, o_ref, *, ts, pg, pos0):
    for t in range(ts):
        for g, w in enumerate(POOL_WINDOWS):
            cols = slice(g * pg, (g + 1) * pg)
            cur = ext_ref[POOL_HIST + t, :, cols]
            acc = cur
            for j in range(1, w):
                acc = acc + ext_ref[POOL_HIST + t - j, :, cols]
            count = float(min(pos0 + t + 1, w))
            dlt = acc / count - cur
            y = _dot(dlt.astype(BF16), pw_ref[g].astype(BF16)) * sc_ref[:, cols]
            o_ref[t, :, cols] = y.astype(o_ref.dtype)


def _pool_sample(ext, ts, pos0, pool_w, pool_scale):
    n_ext, nb, pw = ext.shape
    pg = pw // len(POOL_WINDOWS)
    return pl.pallas_call(
        functools.partial(_pool_sample_kernel, ts=ts, pg=pg, pos0=pos0),
        grid=(1,),
        in_specs=[pl.BlockSpec((n_ext, nb, pw), lambda i: (0, 0, 0)),
                  pl.BlockSpec((len(POOL_WINDOWS), pg, pg), lambda i: (0, 0, 0)),
                  pl.BlockSpec((1, pw), lambda i: (0, 0))],
        out_specs=pl.BlockSpec((ts, nb, pw), lambda i: (0, 0, 0)),
        out_shape=jax.ShapeDtypeStruct((ts, nb, pw), BF16),
        compiler_params=_cparams("arbitrary"),
        name="pool_sample",
    )(ext, pool_w, pool_scale.reshape(1, pw))


def _compress_tail(ha, hb, w2t_ref):
    n_chunk = ha.shape[0]
    h = ha + pltpu.roll(hb, n_chunk - 1, 0)
    act = jax.nn.gelu(h).astype(BF16)
    w2t = w2t_ref[0].astype(BF16)
    return _dot_nt(act, w2t), _dot_nt(w2t, act)


def _compress_prompt_kernel(c_ref, w1_ref, pe_ref, w2t_ref, o_ref, ot_ref):
    c = c_ref[0, 0, 0]
    ha = _dot((c + pe_ref[0, 0]).astype(BF16), w1_ref[0, 0].astype(BF16))
    hb = _dot((c + pe_ref[0, 1]).astype(BF16), w1_ref[0, 1].astype(BF16))
    o_ref[0, 0, 0], ot_ref[0, 0, 0] = _compress_tail(ha, hb, w2t_ref)


def _compress_prompt(chunks, w1, pe, w2t):
    _, b, g, n_chunk, kdim = chunks.shape
    hid = w1.shape[-1]
    return pl.pallas_call(
        _compress_prompt_kernel,
        grid=(2, b, g),
        in_specs=[pl.BlockSpec((1, 1, 1, n_chunk, kdim), lambda p, bi, gi: (p, bi, gi, 0, 0)),
                  pl.BlockSpec((1, CMP_SEGS, kdim, hid), lambda p, bi, gi: (p, 0, 0, 0)),
                  pl.BlockSpec((1, CMP_SEGS, 1, kdim), lambda p, bi, gi: (p, 0, 0, 0)),
                  pl.BlockSpec((1, HEAD_DIM, hid), lambda p, bi, gi: (p, 0, 0))],
        out_specs=[pl.BlockSpec((1, 1, 1, n_chunk, HEAD_DIM), lambda p, bi, gi: (p, bi, gi, 0, 0)),
                   pl.BlockSpec((1, 1, 1, HEAD_DIM, n_chunk), lambda p, bi, gi: (p, bi, gi, 0, 0))],
        out_shape=[jax.ShapeDtypeStruct((2, b, g, n_chunk, HEAD_DIM), F32),
                   jax.ShapeDtypeStruct((2, b, g, HEAD_DIM, n_chunk), F32)],
        compiler_params=_cparams("arbitrary", "arbitrary", "arbitrary"),
        name="compress_prompt",
    )(chunks, w1, pe, w2t)


def _compress_sample_kernel(pt_ref, cache_ref, w1_ref, pe_ref, w2t_ref, perm_ref, o_ref, buf_ref, rows_ref,
                            w1b_ref, peb_ref, sem, *, layer, n_pages, page, nb):
    p = pl.program_id(0)
    b = pl.program_id(1)
    step = p * nb + b
    slot = step % 2
    n_chunk = n_pages * page // CMP_STRIDE
    kg = CMP_K_GROUP
    n_q = CMP_STRIDE // kg

    def page_copy(pp, bb, sl, j):
        pid = pt_ref[bb * n_pages + j]
        return pltpu.make_async_copy(cache_ref.at[layer, pid, pp], buf_ref.at[sl, j], sem.at[sl])

    def fetch(pp, bb, sl):
        def issue(j, c):
            page_copy(pp, bb, sl, j).start()
            return c

        lax.fori_loop(0, n_pages, issue, 0)

    @pl.when(step == 0)
    def _():
        fetch(p, b, slot)

    @pl.when(step + 1 < 2 * nb)
    def _():
        fetch((step + 1) // nb, (step + 1) % nb, 1 - slot)

    @pl.when(b == 0)
    def _():
        w1b_ref[...] = w1_ref[0].astype(BF16)
        bias = jnp.zeros(peb_ref.shape, F32)
        for m in range(CMP_SEGS):
            for q in range(n_q):
                pe_rows = jnp.broadcast_to(pe_ref[0, m, q], (SUBLANES, kg * HEAD_DIM)).astype(BF16)
                bias = bias + _dot(pe_rows, w1b_ref[m, q])
        peb_ref[...] = bias

    def drain(j, c):
        page_copy(p, b, slot, j).wait()
        return c

    lax.fori_loop(0, n_pages, drain, 0)
    cpp = page // CMP_STRIDE

    def to_rows(j, c):
        x_t = buf_ref[slot, j].reshape(GRP_WIDTH, page).astype(BF16)
        x = _dot_nt(perm_ref[...], x_t)
        c0 = pl.multiple_of(j * cpp, cpp)
        for s in range(CMP_STRIDE):
            rows_ref[s, pl.ds(c0, cpp), :] = x[s * cpp:(s + 1) * cpp, :]
        return c

    lax.fori_loop(0, n_pages, to_rows, 0, unroll=16)
    out = []
    for g in range(KV_HEADS):
        lanes = slice(g * HEAD_DIM, (g + 1) * HEAD_DIM)
        acc = [None] * CMP_SEGS
        for q in range(n_q):
            piece = jnp.concatenate([rows_ref[q * kg + k][:, lanes] for k in range(kg)], axis=1).astype(BF16)
            for m in range(CMP_SEGS):
                t = _dot(piece, w1b_ref[m, q])
                acc[m] = t if acc[m] is None else acc[m] + t
        out.append(_compress_tail(acc[0] + peb_ref[0:1, :], acc[1], w2t_ref)[0])
    o_ref[0, 0] = jnp.concatenate(out, axis=1)


def _compress_sample(page_table, cache_t, layer, w1, pe, w2t):
    nb, n_pages = page_table.shape
    page = cache_t.shape[-1]
    n_chunk = n_pages * page // CMP_STRIDE
    hid = w1.shape[-1]
    kg = CMP_K_GROUP
    w1g = w1.reshape(2, CMP_SEGS, CMP_STRIDE // kg, kg * HEAD_DIM, hid)
    peg = pe.reshape(2, CMP_SEGS, CMP_STRIDE // kg, 1, kg * HEAD_DIM)
    cpp = page // CMP_STRIDE
    row = jnp.arange(page)
    perm = ((CMP_STRIDE * (row % cpp) + row // cpp)[:, None] == jnp.arange(page)[None, :]).astype(BF16)
    assert page == LANES and KV_HEADS % 2 == 0
    grid_spec = pltpu.PrefetchScalarGridSpec(
        num_scalar_prefetch=1,
        grid=(2, nb),
        in_specs=[pl.BlockSpec(memory_space=pl.ANY),
                  pl.BlockSpec((1,) + w1g.shape[1:], lambda p, bi, pt: (p, 0, 0, 0, 0)),
                  pl.BlockSpec((1,) + peg.shape[1:], lambda p, bi, pt: (p, 0, 0, 0, 0)),
                  pl.BlockSpec((1, HEAD_DIM, hid), lambda p, bi, pt: (p, 0, 0)),
                  pl.BlockSpec((page, page), lambda p, bi, pt: (0, 0))],
        out_specs=pl.BlockSpec((1, 1, n_chunk, GRP_WIDTH), lambda p, bi, pt: (p, bi, 0, 0)),
        scratch_shapes=[pltpu.VMEM((2, n_pages, KV_HEADS, HEAD_DIM, page), F32),
                        pltpu.VMEM((CMP_STRIDE, n_chunk, GRP_WIDTH), F32),
                        pltpu.VMEM(w1g.shape[1:], BF16),
                        pltpu.VMEM((SUBLANES, hid), F32),
                        pltpu.SemaphoreType.DMA((2,))],
    )
    return pl.pallas_call(
        functools.partial(_compress_sample_kernel, layer=layer, n_pages=n_pages, page=page, nb=nb),
        grid_spec=grid_spec,
        out_shape=jax.ShapeDtypeStruct((2, nb, n_chunk, GRP_WIDTH), F32),
        compiler_params=_cparams("arbitrary", "arbitrary"),
        name="compress_sample",
    )(page_table.reshape(-1), cache_t, w1g, peg, w2t, perm)


def _attn_prompt_kernel(slope_ref, q_ref, kc_ref, vct_ref, ks_ref, vs_ref, kw_ref, vw_ref, gate_ref, covert_ref,
                        o_ref, m_ref, l_ref, acc_ref, st_ref, bias_ref, s_ref, z_ref, e_ref, flag_ref,
                        *, seq, n_cmp, n_sel, n_top, wlen):
    qb = Q_BLOCK
    tk = SEL_KV_TILE
    g = pl.program_id(1)
    p0 = pl.program_id(2) * qb
    qpos = p0 + lax.broadcasted_iota(I32, (1, qb), 1)
    slopes = [slope_ref[g * Q_PER_KV + h] for h in range(Q_PER_KV)]
    q_t = q_ref[...].astype(F32).T.astype(BF16)
    q_all = jnp.concatenate([q_t[h * HEAD_DIM:(h + 1) * HEAD_DIM] for h in range(Q_PER_KV)], axis=1)
    head = lambda a, h: a[:, h * qb:(h + 1) * qb]

    def col_softmax(z):
        m = jnp.maximum(jnp.max(z, axis=0, keepdims=True), NEG_CLAMP)
        e = jnp.exp(z - m)
        l = jnp.sum(e, axis=0, keepdims=True)
        return e, jnp.where(l > 0.0, l, 1.0)

    n_chunk = kc_ref.shape[3]
    crow = lax.broadcasted_iota(I32, (n_chunk, 1), 0)
    d_c = qpos - (crow * CMP_STRIDE + (CMP_LEN - 1))
    mask_c = jnp.where((d_c >= 0) & (crow < n_cmp), 0.0, NEG_INF)
    d_cf = d_c.astype(F32)
    vct = vct_ref[0, 0, 0].astype(BF16)
    s_c = _dot(kc_ref[0, 0, 0].astype(BF16), q_all)
    p_sum = jnp.zeros((n_chunk, qb), F32)
    o_c = []
    for h in range(Q_PER_KV):
        e, l = col_softmax(head(s_c, h) - slopes[h] * d_cf + mask_c)
        p = e / l
        p_sum = p_sum + p
        o_c.append(_dot(vct, p.astype(BF16)))

    p_hi, p_lo = _split_bf16(p_sum)
    cover_t = covert_ref[...]
    imp_t = _dot(cover_t, p_hi) + _dot(cover_t, p_lo)
    n_blk = -(-n_sel // SUBLANES) * SUBLANES
    blk = lax.broadcasted_iota(I32, (n_blk, 1), 0)
    score = _block_scores(imp_t[0:n_blk], blk, qpos, n_sel)
    st_ref[0:n_blk, :] = score

    def count_beaten(k, cnt):
        row = st_ref[pl.ds(k, 1), :]
        return cnt + jnp.where((row > score) | ((row == score) & (k < blk)), 1.0, 0.0)

    rank = lax.fori_loop(0, n_sel, count_beaten, jnp.zeros(score.shape, F32), unroll=4)
    chosen = rank < n_top
    st_ref[0:n_blk, :] = jnp.where(chosen, 0.0, NEG_INF)
    bpt = SEL_KV_TILE // SEL_BLOCK
    for t in range(n_blk // bpt):
        flag_ref[t] = jnp.max(jnp.where(chosen[t * bpt:(t + 1) * bpt], 1.0, 0.0)).astype(I32)

    krow = lax.broadcasted_iota(I32, (tk, 1), 0)
    @pl.when(pl.program_id(2) == 0)
    def _():
        rel = (krow - (qpos - p0)).astype(F32)
        for h in range(Q_PER_KV):
            bias_ref[h] = slopes[h] * rel
    m_ref[...] = jnp.full(m_ref.shape, NEG_INF, F32)
    l_ref[...] = jnp.zeros(l_ref.shape, F32)
    acc_ref[...] = jnp.zeros(acc_ref.shape, F32)
    n_tiles = (p0 + qb + tk - 1) // tk

    def sweep_tile(jt, causal):
        k0 = pl.multiple_of(jt * tk, tk)
        k_rows = ks_ref[0, 0, :, pl.ds(k0, tk)].T.astype(BF16)
        v_t = vs_ref[0, 0, :, pl.ds(k0, tk)].astype(BF16)
        s_ref[...] = _dot(k_rows, q_all)
        j0 = k0 // SEL_BLOCK
        off = (k0 - p0).astype(F32)
        nblk = tk // SEL_BLOCK
        fold = lambda a, op: op(a.reshape(SEL_BLOCK // SUBLANES, SUBLANES, qb), axis=0)
        for h in range(Q_PER_KV):
            mx = None
            for c in range(nblk):
                rows = slice(c * SEL_BLOCK, (c + 1) * SEL_BLOCK)
                madd = jnp.broadcast_to(st_ref[pl.ds(j0 + c, 1), :], (SEL_BLOCK, qb))
                if causal:
                    madd = jnp.where(k0 + krow[rows] <= qpos, madd, NEG_INF)
                z = s_ref[rows, h * qb:(h + 1) * qb] + bias_ref[h, rows, :] + madd
                z_ref[h, rows, :] = z
                cm = fold(z, jnp.max)
                mx = cm if mx is None else jnp.maximum(mx, cm)
            shift = slopes[h] * off
            m_old = m_ref[h]
            m_new = jnp.maximum(m_old, jnp.max(mx, axis=0, keepdims=True) + shift)
            m_use = jnp.maximum(m_new, NEG_CLAMP)
            sm = None
            for c in range(nblk):
                rows = slice(c * SEL_BLOCK, (c + 1) * SEL_BLOCK)
                e = jnp.exp(z_ref[h, rows, :] - (m_use - shift))
                e_ref[h, rows, :] = e.astype(BF16)
                cs = fold(e, jnp.sum)
                sm = cs if sm is None else sm + cs
            alpha = jnp.exp(m_old - m_use)
            l_ref[h] = alpha * l_ref[h] + jnp.sum(sm, axis=0, keepdims=True)
            acc_ref[h] = alpha * acc_ref[h] + _dot(v_t, e_ref[h])
            m_ref[h] = m_new

    def sweep_body(jt, carry):
        @pl.when(flag_ref[jt] > 0)
        def _():
            sweep_tile(jt, False)

        return carry

    lax.fori_loop(0, n_tiles - 1, sweep_body, 0)
    sweep_tile(n_tiles - 1, True)

    w0 = pl.multiple_of(jnp.minimum(jnp.maximum(p0 - WINDOW, 0), seq - wlen), qb)
    relw = (w0 + lax.broadcasted_iota(I32, (wlen, 1), 0)) - qpos
    mask_w = jnp.where((relw <= 0) & (relw > -WINDOW), 0.0, NEG_INF)
    relw_f = relw.astype(F32)
    kw_rows = kw_ref[0, 0, :, pl.ds(w0, wlen)].T.astype(BF16)
    vw_t = vw_ref[0, 0, :, pl.ds(w0, wlen)].astype(BF16)
    s_w = _dot(kw_rows, q_all)

    out = []
    for h in range(Q_PER_KV):
        e, l = col_softmax(head(s_w, h) + slopes[h] * relw_f + mask_w)
        o_w = _dot(vw_t, e.astype(BF16)) / l
        l_s = l_ref[h]
        o_s = acc_ref[h] / jnp.where(l_s > 0.0, l_s, 1.0)
        g0 = gate_ref[0, 0, 3 * h + 0:3 * h + 1, :]
        g1 = gate_ref[0, 0, 3 * h + 1:3 * h + 2, :]
        g2 = gate_ref[0, 0, 3 * h + 2:3 * h + 3, :]
        out.append(g0 * o_c[h] + g1 * o_s + g2 * o_w)
    o_ref[...] = jnp.concatenate(out, axis=0).T.astype(o_ref.dtype)


def _attn_prompt(slopes, q, batch, seq, cmp_rows, cmp_t, kv_t, win_t, gates_t, cover_t):
    n_chunk = cmp_rows.shape[3]
    n_cmp = (seq - CMP_LEN) // CMP_STRIDE + 1
    n_sel = -(-seq // SEL_BLOCK)
    n_top = min(SEL_TOPK, n_sel)
    wlen = min(WINDOW + Q_BLOCK, seq)
    nsp = cover_t.shape[0]
    nqb = seq // Q_BLOCK
    assert seq % SEL_KV_TILE == 0 and seq % Q_BLOCK == 0 and n_chunk >= n_cmp and wlen % LANES == 0

    def plane(pidx):
        return pl.BlockSpec((1, 1, HEAD_DIM, seq), lambda bi, gi, i: (bi, pidx, gi, 0))

    return pl.pallas_call(
        functools.partial(_attn_prompt_kernel, seq=seq, n_cmp=n_cmp, n_sel=n_sel, n_top=n_top, wlen=wlen),
        grid=(batch, KV_HEADS, nqb),
        in_specs=[pl.BlockSpec(memory_space=pltpu.SMEM),
                  pl.BlockSpec((Q_BLOCK, GRP_WIDTH), lambda bi, gi, i: (bi * nqb + i, gi)),
                  pl.BlockSpec((1, 1, 1, n_chunk, HEAD_DIM), lambda bi, gi, i: (0, bi, gi, 0, 0)),
                  pl.BlockSpec((1, 1, 1, HEAD_DIM, n_chunk), lambda bi, gi, i: (1, bi, gi, 0, 0)),
                  plane(2), plane(3), plane(0), plane(1),
                  pl.BlockSpec((1, 1, 3 * Q_PER_KV, Q_BLOCK), lambda bi, gi, i: (bi, gi, 0, i)),
                  pl.BlockSpec(cover_t.shape, lambda bi, gi, i: (0, 0))],
        out_specs=pl.BlockSpec((Q_BLOCK, GRP_WIDTH), lambda bi, gi, i: (bi * nqb + i, gi)),
        out_shape=jax.ShapeDtypeStruct((batch * seq, N_HEADS * HEAD_DIM), BF16),
        scratch_shapes=[pltpu.VMEM((Q_PER_KV, 1, Q_BLOCK), F32), pltpu.VMEM((Q_PER_KV, 1, Q_BLOCK), F32),
                        pltpu.VMEM((Q_PER_KV, HEAD_DIM, Q_BLOCK), F32), pltpu.VMEM((nsp, Q_BLOCK), F32),
                        pltpu.VMEM((Q_PER_KV, SEL_KV_TILE, Q_BLOCK), F32),
                        pltpu.VMEM((SEL_KV_TILE, Q_PER_KV * Q_BLOCK), F32),
                        pltpu.VMEM((Q_PER_KV, SEL_KV_TILE, Q_BLOCK), F32),
                        pltpu.VMEM((Q_PER_KV, SEL_KV_TILE, Q_BLOCK), BF16),
                        pltpu.SMEM((seq // SEL_KV_TILE,), I32)],
        compiler_params=_cparams("arbitrary", "arbitrary", "arbitrary"),
        name="attn_prompt",
    )(slopes, q, cmp_rows, cmp_t, kv_t, kv_t, win_t, win_t, gates_t, cover_t)


def _attn_sample_kernel(pt_ref, qbd_ref, kc_ref, vc_ref, cache_ref, new_ref, win_ref, gate_ref, slope_ref,
                        qpos_ref, qlane_ref, cover_ref, expand_ref, o_ref, kt_ref, vt_ref, st_ref, sem,
                        *, layer, n_pages, page, past, ts, n_cmp, n_sel, n_top, win_buf, gt):
    b = pl.program_id(0)
    nk = kt_ref.shape[2]

    def page_copies(j):
        pid = pt_ref[b * n_pages + j]
        dst = pl.ds(pl.multiple_of(j * page, page), page)
        return (pltpu.make_async_copy(cache_ref.at[layer, pid, 2], kt_ref.at[:, :, dst], sem),
                pltpu.make_async_copy(cache_ref.at[layer, pid, 3], vt_ref.at[:, :, dst], sem))

    def issue(j, c):
        for cp in page_copies(j):
            cp.start()
        return c

    def drain(j, c):
        for cp in page_copies(j):
            cp.wait()
        return c

    lax.fori_loop(0, n_pages, issue, 0)
    kt_ref[:, :, past:] = new_ref[0, 0].reshape(KV_HEADS, HEAD_DIM, nk - past)
    vt_ref[:, :, past:] = new_ref[0, 1].reshape(KV_HEADS, HEAD_DIM, nk - past)

    q = qbd_ref[0]
    nrow = q.shape[0]
    slope = slope_ref[...]
    qpos = qpos_ref[...]
    lane_grp = lax.broadcasted_iota(I32, (nrow, GRP_WIDTH), 1) // HEAD_DIM
    row_grp = (lax.broadcasted_iota(I32, (nrow, GRP_WIDTH), 0) % gt) // (gt // KV_HEADS)
    diag = lane_grp == row_grp

    def own_group(full):
        kept = jnp.where(diag, full, 0.0)
        out = kept[:, 0:HEAD_DIM]
        for g in range(1, KV_HEADS):
            out = out + kept[:, g * HEAD_DIM:(g + 1) * HEAD_DIM]
        return out

    n_chunk = kc_ref.shape[2]
    ccol = lax.broadcasted_iota(I32, (1, n_chunk), 1)
    d_c = qpos - (ccol * CMP_STRIDE + (CMP_LEN - 1))
    s_c = _dot_nt(q, kc_ref[0, 0].astype(BF16)) - slope * d_c.astype(F32)
    p_c = _masked_softmax(s_c, (d_c >= 0) & (ccol < n_cmp))
    o_c = own_group(_dot(p_c.astype(BF16), vc_ref[0, 0].astype(BF16)))

    p_sum = p_c[0:gt]
    for h in range(1, Q_PER_KV):
        p_sum = p_sum + p_c[h * gt:(h + 1) * gt]
    p_pad = jnp.concatenate([p_sum, jnp.zeros((LANES - gt, n_chunk), F32)], axis=0)
    p_hi, p_lo = _split_bf16(p_pad)
    cover_t = cover_ref[...]
    imp_t = _dot_nt(cover_t, p_hi) + _dot_nt(cover_t, p_lo)
    n_blk = -(-n_sel // SUBLANES) * SUBLANES
    blk = lax.broadcasted_iota(I32, (n_blk, 1), 0)
    score = _block_scores(imp_t[0:n_blk], blk, qlane_ref[...], n_sel)
    st_ref[0:n_blk, :] = score

    def count_beaten(k, cnt):
        row = st_ref[pl.ds(k, 1), :]
        return cnt + jnp.where((row > score) | ((row == score) & (k < blk)), 1.0, 0.0)

    rank = lax.fori_loop(0, n_sel, count_beaten, jnp.zeros(score.shape, F32), unroll=4)
    st_ref[0:n_blk, :] = jnp.where(rank < n_top, 1.0, 0.0)
    st_ref[n_blk:, :] = jnp.zeros((st_ref.shape[0] - n_blk, LANES), F32)
    selmask = st_ref[...].T[0:gt].astype(BF16)

    n_newl = new_ref.shape[3]
    wcol = lax.broadcasted_iota(I32, (1, win_buf + n_newl), 1)
    kpos_w = jnp.where(wcol < win_buf, past - win_buf + wcol, past + wcol - win_buf)
    d_w = qpos - kpos_w
    valid_w = (d_w >= 0) & (d_w < WINDOW) & (wcol < win_buf + ts)
    kw_t = win_ref[0, 0, 0].reshape(GRP_WIDTH, win_buf).astype(BF16)
    vw_t = win_ref[0, 0, 1].reshape(GRP_WIDTH, win_buf).astype(BF16)
    s_w = jnp.concatenate([_dot(q, kw_t), _dot(q, new_ref[0, 2].astype(BF16))], axis=1)
    p_w = _masked_softmax(s_w - slope * d_w.astype(F32), valid_w).astype(BF16)
    o_w = own_group(_dot_nt(p_w[:, :win_buf], vw_t) + _dot_nt(p_w[:, win_buf:], new_ref[0, 3].astype(BF16)))

    lax.fori_loop(0, n_pages, drain, 0)
    selk = _dot(selmask, expand_ref[...])
    selk = jnp.concatenate([selk] * Q_PER_KV, axis=0)
    d_s = qpos - lax.broadcasted_iota(I32, (1, nk), 1)
    k_t = kt_ref[...].reshape(GRP_WIDTH, nk).astype(BF16)
    s_s = _dot(q, k_t) - slope * d_s.astype(F32)
    p_s = _masked_softmax(s_s, (selk > 0.5) & (d_s >= 0))
    v_t = vt_ref[...].reshape(GRP_WIDTH, nk).astype(BF16)
    o_s = own_group(_dot_nt(p_s.astype(BF16), v_t))

    gate = gate_ref[0]
    o_ref[0] = gate[:, 0:1] * o_c + gate[:, 1:2] * o_s + gate[:, 2:3] * o_w


def _attn_sample(page_table, qbd, cmp_kv, cache_t, new_t, win_t, gates, slope_rows, qpos_rows, qpos_lanes,
                 cover_t, expand, *, layer, past, ts, n_cmp, n_sel, gt):
    nb, n_pages = page_table.shape
    page = cache_t.shape[-1]
    win_buf = win_t.shape[-1]
    nrow = qbd.shape[1]
    nk = expand.shape[1]
    n_top = min(SEL_TOPK, n_sel)
    assert nk == past + new_t.shape[3] and win_buf % LANES == 0

    def per_seq(shape):
        return pl.BlockSpec((1,) + tuple(shape[1:]), lambda bi, pt: (bi,) + (0,) * (len(shape) - 1))

    def whole(shape):
        return pl.BlockSpec(tuple(shape), lambda bi, pt: (0,) * len(shape))

    grid_spec = pltpu.PrefetchScalarGridSpec(
        num_scalar_prefetch=1,
        grid=(nb,),
        in_specs=[per_seq(qbd.shape),
                  pl.BlockSpec((1, 1) + tuple(cmp_kv.shape[2:]), lambda bi, pt: (0, bi, 0, 0)),
                  pl.BlockSpec((1, 1) + tuple(cmp_kv.shape[2:]), lambda bi, pt: (1, bi, 0, 0)),
                  pl.BlockSpec(memory_space=pl.ANY), per_seq(new_t.shape),
                  pl.BlockSpec((1, 1) + tuple(win_t.shape[2:]), lambda bi, pt: (layer, bi, 0, 0, 0, 0)),
                  per_seq(gates.shape), whole(slope_rows.shape), whole(qpos_rows.shape),
                  whole(qpos_lanes.shape), whole(cover_t.shape), whole(expand.shape)],
        out_specs=pl.BlockSpec((1, nrow, HEAD_DIM), lambda bi, pt: (bi, 0, 0)),
        scratch_shapes=[pltpu.VMEM((KV_HEADS, HEAD_DIM, nk), F32), pltpu.VMEM((KV_HEADS, HEAD_DIM, nk), F32),
                        pltpu.VMEM((cover_t.shape[0], LANES), F32), pltpu.SemaphoreType.DMA(())],
    )
    return pl.pallas_call(
        functools.partial(_attn_sample_kernel, layer=layer, n_pages=n_pages, page=page, past=past, ts=ts,
                          n_cmp=n_cmp, n_sel=n_sel, n_top=n_top, win_buf=win_buf, gt=gt),
        grid_spec=grid_spec,
        out_shape=jax.ShapeDtypeStruct((nb, nrow, HEAD_DIM), F32),
        compiler_params=_cparams("arbitrary"),
        name="attn_sample",
    )(page_table.reshape(-1), qbd, cmp_kv, cmp_kv, cache_t, new_t, win_t, gates, slope_rows, qpos_rows,
      qpos_lanes, cover_t, expand)


def _merge_kernel(u_ref, yp_ref, ya_ref, wgp_ref, wga_ref, wp_ref, wa_ref, o_ref, wgpb_ref, wgab_ref, wpb_ref,
                  wab_ref):
    @pl.when(pl.program_id(1) == 0)
    def _():
        wgpb_ref[...] = wgp_ref[0].astype(BF16)
        wgab_ref[...] = wga_ref[0].astype(BF16)
        wpb_ref[...] = wp_ref[0].astype(BF16)
        wab_ref[...] = wa_ref[0].astype(BF16)

    u = u_ref[...]
    g_pool = jax.nn.sigmoid(_dot_nt(u, wgpb_ref[...]))
    g_attn = jax.nn.sigmoid(_dot_nt(u, wgab_ref[...]))
    merged = g_pool * _dot(yp_ref[...], wpb_ref[...]) + g_attn * _dot(ya_ref[...], wab_ref[...])
    o_ref[...] = merged.astype(o_ref.dtype)


def _merge(u, y_pool, y_attn, w_gm_t, w_up_pool, w_up_nsa, layer):
    m, d = u.shape
    kp = y_pool.shape[1]
    ka = y_attn.shape[1]
    tn = MM_TILE_N
    nn = d // tn
    row = lambda w: pl.BlockSpec((MM_TILE_M, w), lambda n, i: (i, 0))
    return pl.pallas_call(
        _merge_kernel,
        grid=(nn, m // MM_TILE_M),
        in_specs=[row(d), row(kp), row(ka),
                  pl.BlockSpec((1, tn, d), lambda n, i: (layer, n, 0)),
                  pl.BlockSpec((1, tn, d), lambda n, i: (layer, nn + n, 0)),
                  pl.BlockSpec((1, kp, tn), lambda n, i: (layer, 0, n)),
                  pl.BlockSpec((1, ka, tn), lambda n, i: (layer, 0, n))],
        out_specs=pl.BlockSpec((MM_TILE_M, tn), lambda n, i: (i, n)),
        out_shape=jax.ShapeDtypeStruct((m, d), BF16),
        scratch_shapes=[pltpu.VMEM((tn, d), BF16), pltpu.VMEM((tn, d), BF16),
                        pltpu.VMEM((kp, tn), BF16), pltpu.VMEM((ka, tn), BF16)],
        compiler_params=_cparams("arbitrary", "arbitrary"),
        name="merge",
    )(u, y_pool, y_attn, w_gm_t, w_gm_t, w_up_pool, w_up_nsa)


def _route(logits):
    lane = lax.broadcasted_iota(I32, logits.shape, 1)
    is_grp = lane < N_GROUPS
    p_grp = _masked_softmax(logits, is_grp)
    p_top = jnp.max(p_grp, axis=-1, keepdims=True)
    grp = jnp.min(jnp.where(is_grp & (p_grp == p_top), lane, LANES), axis=-1, keepdims=True)
    first = N_GROUPS + grp * EXPERTS_PER_GROUP
    in_grp = (lane >= first) & (lane < first + EXPERTS_PER_GROUP)
    v = _masked_softmax(logits, in_grp)
    v0 = jnp.max(jnp.where(in_grp, v, -1.0), axis=-1, keepdims=True)
    i0 = jnp.min(jnp.where(in_grp & (v == v0), lane, LANES), axis=-1, keepdims=True)
    rest = in_grp & (lane != i0)
    v1 = jnp.max(jnp.where(rest, v, -1.0), axis=-1, keepdims=True)
    i1 = jnp.min(jnp.where(rest & (v == v1), lane, LANES), axis=-1, keepdims=True)
    tot = v0 + v1
    out = jnp.where(lane == 0, (i0 - N_GROUPS).astype(F32), 0.0)
    out = jnp.where(lane == 1, (i1 - N_GROUPS).astype(F32), out)
    out = jnp.where(lane == 2, p_top * (v0 / tot), out)
    out = jnp.where(lane == 3, p_top * (v1 / tot), out)
    return out


def _outproj_kernel(mg_ref, x_ref, w_ref, mp_ref, ms_ref, g_ref, b_ref, wr_ref, br_ref,
                    x1_ref, u_ref, rt_ref, *, n_ptiles, alpha):
    is_s = pl.program_id(0) >= n_ptiles
    gate1 = _mod_row(is_s, mp_ref, ms_ref, 0)
    shift2 = _mod_row(is_s, mp_ref, ms_ref, 1)
    scale2 = _mod_row(is_s, mp_ref, ms_ref, 2)
    mix = _dot(mg_ref[...], w_ref[...])
    x1 = _layer_norm(alpha * x_ref[...] + gate1 * mix, g_ref[...], b_ref[...])
    x1_ref[...] = x1
    u = x1 * (1.0 + scale2) + shift2
    u_ref[...] = u
    wr_hi, wr_lo = _split_bf16(wr_ref[...])
    u_hi, u_lo = _split_bf16(u)
    logits = _dot_nt(u_hi, wr_hi) + _dot_nt(u_lo, wr_hi) + _dot_nt(u_hi, wr_lo) + br_ref[...]
    rt_ref[...] = _route(logits)


def _outproj(rows, merged, x, w_out_bf, modp, mods, ln_g, ln_b, w_router_t, b_router, alpha):
    d = x.shape[1]
    row = pl.BlockSpec((ROW_TILE, d), lambda i: (i, 0))
    vec = pl.BlockSpec((1, d), lambda i: (0, 0))
    return pl.pallas_call(
        functools.partial(_outproj_kernel, n_ptiles=rows.n_ptiles, alpha=alpha),
        grid=(rows.n_rows // ROW_TILE,),
        in_specs=[row, row, pl.BlockSpec((d, d), lambda i: (0, 0))] + rows.mod_specs(d, 1)
        + [vec, vec, pl.BlockSpec((LANES, d), lambda i: (0, 0)), pl.BlockSpec((1, LANES), lambda i: (0, 0))],
        out_specs=[row, row, pl.BlockSpec((ROW_TILE, LANES), lambda i: (i, 0))],
        out_shape=[jax.ShapeDtypeStruct((rows.n_rows, d), F32),
                   jax.ShapeDtypeStruct((rows.n_rows, d), F32),
                   jax.ShapeDtypeStruct((rows.n_rows, LANES), F32)],
        compiler_params=_cparams("arbitrary"),
        name="outproj_ln_route",
    )(merged, x, w_out_bf, modp, mods, ln_g.reshape(1, d), ln_b.reshape(1, d), w_router_t, b_router)


def _moe_scatter_kernel(dest_ref, pad_ref, u_ref, xs_ref, zero_ref, sem, *, n_real):
    i = pl.program_id(0)
    tm = u_ref.shape[0]

    def pad_copy(r):
        return pltpu.make_async_copy(zero_ref, xs_ref.at[pl.ds(r, 1)], sem)

    @pl.when(i == 0)
    def _():
        zero_ref[...] = jnp.zeros(zero_ref.shape, zero_ref.dtype)
        for e in range(N_EXPERTS + 1):
            lo, hi = pad_ref[2 * e], pad_ref[2 * e + 1]

            def issue_pad(r, c):
                pad_copy(r).start()
                return c

            def drain_pad(r, c):
                pad_copy(r).wait()
                return c

            lax.fori_loop(lo, hi, issue_pad, 0)
            lax.fori_loop(lo, hi, drain_pad, 0)

    def row_copy(r, dst):
        return pltpu.make_async_copy(u_ref.at[pl.ds(r, 1)], xs_ref.at[pl.ds(dst, 1)], sem)

    def issue(r, c):
        base = 2 * (i * tm + r)
        row_copy(r, dest_ref[base]).start()
        row_copy(r, dest_ref[base + 1]).start()
        return c

    def drain(r, c):
        row_copy(r, 0).wait()
        row_copy(r, 0).wait()
        return c

    @pl.when((i + 1) * tm <= n_real)
    def _():
        lax.fori_loop(0, tm, issue, 0, unroll=8)
        lax.fori_loop(0, tm, drain, 0, unroll=8)

    @pl.when((i + 1) * tm > n_real)
    def _():
        lax.fori_loop(0, n_real - i * tm, issue, 0)
        lax.fori_loop(0, n_real - i * tm, drain, 0)


def _moe_scatter(dest, pads, u, n_real, r_pad):
    d = u.shape[1]
    tm = EXPERT_TILE
    grid_spec = pltpu.PrefetchScalarGridSpec(
        num_scalar_prefetch=2,
        grid=(-(-n_real // tm),),
        in_specs=[pl.BlockSpec((tm, d), lambda i, *_: (i, 0))],
        out_specs=pl.BlockSpec(memory_space=pl.ANY),
        scratch_shapes=[pltpu.VMEM((1, d), F32), pltpu.SemaphoreType.DMA(())],
    )
    return pl.pallas_call(
        functools.partial(_moe_scatter_kernel, n_real=n_real),
        grid_spec=grid_spec,
        out_shape=jax.ShapeDtypeStruct((r_pad, d), F32),
        compiler_params=_cparams("arbitrary"),
        name="moe_scatter",
    )(dest, pads, u)


def _moe_expert_kernel(te_ref, used_ref, x_ref, wg_ref, wu_ref, wd_ref, o_ref, wgb_ref, wub_ref, wdb_ref):
    t = pl.program_id(0)
    fresh = jnp.logical_or(t == 0, te_ref[t] != te_ref[jnp.maximum(t - 1, 0)])

    @pl.when(fresh)
    def _():
        wgb_ref[...] = wg_ref[0, 0].astype(BF16)
        wub_ref[...] = wu_ref[0, 0].astype(BF16)
        wdb_ref[...] = wd_ref[0, 0].astype(BF16)

    @pl.when(t < used_ref[0])
    def _():
        x = x_ref[...].astype(BF16)
        h = jax.nn.silu(_dot(x, wgb_ref[...])) * _dot(x, wub_ref[...])
        o_ref[...] = _dot(h.astype(BF16), wdb_ref[...])

    @pl.when(t >= used_ref[0])
    def _():
        o_ref[...] = jnp.zeros(o_ref.shape, o_ref.dtype)


def _moe_experts(tile_expert, n_used, xs, w_gate, w_up, w_down, layer):
    r_pad, d = xs.shape
    f = w_gate.shape[3]
    tm = EXPERT_TILE
    n_tiles = r_pad // tm

    def x_map(t, te, nu):
        return (jnp.minimum(t, jnp.maximum(nu[0] - 1, 0)), 0)

    grid_spec = pltpu.PrefetchScalarGridSpec(
        num_scalar_prefetch=2,
        grid=(n_tiles,),
        in_specs=[pl.BlockSpec((tm, d), x_map),
                  pl.BlockSpec((1, 1, d, f), lambda t, te, nu: (layer, te[t], 0, 0)),
                  pl.BlockSpec((1, 1, d, f), lambda t, te, nu: (layer, te[t], 0, 0)),
                  pl.BlockSpec((1, 1, f, d), lambda t, te, nu: (layer, te[t], 0, 0))],
        out_specs=pl.BlockSpec((tm, d), lambda t, te, nu: (t, 0)),
        scratch_shapes=[pltpu.VMEM((d, f), BF16), pltpu.VMEM((d, f), BF16), pltpu.VMEM((f, d), BF16)],
    )
    return pl.pallas_call(
        _moe_expert_kernel,
        grid_spec=grid_spec,
        out_shape=jax.ShapeDtypeStruct((r_pad, d), F32),
        compiler_params=_cparams("arbitrary"),
        name="moe_experts",
    )(tile_expert, n_used, xs, w_gate, w_up, w_down)


def _moe_combine_kernel(pos_ref, y_ref, x1_ref, rt_ref, mp_ref, ms_ref, g_ref, b_ref, x2_ref, *rest,
                        n_ptiles, alpha, emit_next):
    if emit_next:
        u_ref, buf_ref, sem = rest
    else:
        buf_ref, sem = rest
    i = pl.program_id(0)
    tm = x1_ref.shape[0]
    slot = i % 2

    def row_copy(sl, r, k, src):
        return pltpu.make_async_copy(y_ref.at[pl.ds(src, 1)], buf_ref.at[sl, k, pl.ds(r, 1)], sem.at[sl])

    def fetch(tile, sl):
        def issue(r, c):
            base = 2 * (tile * tm + r)
            row_copy(sl, r, 0, pos_ref[base]).start()
            row_copy(sl, r, 1, pos_ref[base + 1]).start()
            return c

        lax.fori_loop(0, tm, issue, 0, unroll=8)

    @pl.when(i == 0)
    def _():
        fetch(i, slot)

    @pl.when(i + 1 < pl.num_programs(0))
    def _():
        fetch(i + 1, 1 - slot)

    def drain(r, c):
        row_copy(slot, r, 0, 0).wait()
        row_copy(slot, r, 1, 0).wait()
        return c

    lax.fori_loop(0, tm, drain, 0, unroll=8)
    is_s = i >= n_ptiles
    gate2 = _mod_row(is_s, mp_ref, ms_ref, 0)
    ffn = rt_ref[:, 2:3] * buf_ref[slot, 0] + rt_ref[:, 3:4] * buf_ref[slot, 1]
    x2 = _layer_norm(alpha * x1_ref[...] + gate2 * ffn, g_ref[...], b_ref[...])
    x2_ref[...] = x2
    if emit_next:
        shift = _mod_row(is_s, mp_ref, ms_ref, 1)
        scale = _mod_row(is_s, mp_ref, ms_ref, 2)
        u_ref[...] = (x2 * (1.0 + scale) + shift).astype(u_ref.dtype)


def _moe_combine(rows, pos, y_rows, x1, route, modp, mods, ln_g, ln_b, alpha, emit_next):
    d = x1.shape[1]
    row = pl.BlockSpec((ROW_TILE, d), lambda i, *_: (i, 0))
    vec = pl.BlockSpec((1, d), lambda i, *_: (0, 0))
    out_specs = [row]
    out_shape = [jax.ShapeDtypeStruct((rows.n_rows, d), F32)]
    if emit_next:
        out_specs.append(row)
        out_shape.append(jax.ShapeDtypeStruct((rows.n_rows, d), BF16))
    grid_spec = pltpu.PrefetchScalarGridSpec(
        num_scalar_prefetch=1,
        grid=(rows.n_rows // ROW_TILE,),
        in_specs=[pl.BlockSpec(memory_space=pl.ANY), row, pl.BlockSpec((ROW_TILE, LANES), lambda i, *_: (i, 0))]
        + rows.mod_specs(d, 2) + [vec, vec],
        out_specs=out_specs,
        scratch_shapes=[pltpu.VMEM((2, 2, ROW_TILE, d), F32), pltpu.SemaphoreType.DMA((2,))],
    )
    out = pl.pallas_call(
        functools.partial(_moe_combine_kernel, n_ptiles=rows.n_ptiles, alpha=alpha, emit_next=emit_next),
        grid_spec=grid_spec,
        out_shape=out_shape,
        compiler_params=_cparams("arbitrary"),
        name="moe_combine_ln",
    )(pos, y_rows, x1, route, modp, mods, ln_g.reshape(1, d), ln_b.reshape(1, d))
    return out if emit_next else (out[0], None)


def _moe_dispatch(route, n_real, n_rows):
    tm = EXPERT_TILE
    n_pairs = 2 * n_real
    n_tiles = -(-(n_pairs + N_EXPERTS * (tm - 1)) // tm)
    eid = route[:n_real, 0:2].astype(I32).reshape(n_pairs, 1)
    onehot = (eid == jnp.arange(N_EXPERTS, dtype=I32)[None, :]).astype(I32)
    seen = jnp.cumsum(onehot, axis=0)
    counts = seen[-1]
    padded = -(-counts // tm) * tm
    ends_pad = jnp.cumsum(padded)
    starts_pad = ends_pad - padded
    dest = jnp.sum(onehot * (starts_pad[None, :] + seen - 1), axis=1).astype(I32)
    pos = jnp.concatenate([dest, jnp.zeros((2 * (n_rows - n_real),), I32)])
    pads = jnp.stack([starts_pad + counts, ends_pad], axis=1).reshape(-1)
    pads = jnp.concatenate([pads, ends_pad[-1:], jnp.full((1,), n_tiles * tm)]).astype(I32)
    n_used = (ends_pad[-1] // tm).astype(I32)
    tile_start = jnp.arange(n_tiles, dtype=I32) * tm
    tile_e = jnp.minimum(jnp.sum((tile_start[:, None] >= ends_pad[None, :]).astype(I32), axis=1), N_EXPERTS - 1)
    last_e = jnp.sum(jnp.where(jnp.arange(n_tiles) == jnp.maximum(n_used - 1, 0), tile_e, 0))
    tile_e = jnp.where(jnp.arange(n_tiles) < n_used, tile_e, last_e).astype(I32)
    return pos, pads, tile_e, n_used.reshape(1), n_tiles * tm


def _cover_matrix(n_chunk, n_cmp, n_sel, n_sel_pad):
    ci = jnp.arange(n_chunk)[:, None]
    sj = jnp.arange(n_sel_pad)[None, :]
    hit = ((ci * CMP_STRIDE < sj * SEL_BLOCK + SEL_BLOCK)
           & (ci * CMP_STRIDE + CMP_LEN - 1 >= sj * SEL_BLOCK)
           & (ci < n_cmp) & (sj < n_sel))
    return hit.astype(BF16)


def kernel(x_prompt, x_sample, c_prompt, c_sample, cache_kv, cache_win, state_pool, page_table, w_ada, b_ada,
           w_in, cmp_w1, cmp_pe, cmp_w2, pool_w, pool_scale, w_up_pool, w_up_nsa, w_out, ln1_g, ln1_b, w_rg,
           b_rg, w_re, b_re, w_gate, w_up, w_down, ln2_g, ln2_b):
    batch, seq, d = x_prompt.shape
    nb, ts, _ = x_sample.shape
    n_layers = w_ada.shape[0]
    page = cache_kv.shape[2]
    n_pages = page_table.shape[1]
    past = n_pages * page
    win_buf = cache_win.shape[2]
    pool_width = pool_w.shape[1] * pool_w.shape[2]
    q_width = N_HEADS * HEAD_DIM
    kv_width = N_KV_PLANES * GRP_WIDTH
    gn_width = 3 * N_HEADS
    alpha = (2 * n_layers) ** 0.25
    cmp_hid = cmp_w1.shape[-1]
    chunk_w = CMP_STRIDE * HEAD_DIM

    rows = _Rows(batch, seq, nb * ts)
    n_p, n_s, n_rows = rows.n_prompt, rows.n_sample, rows.n_rows
    n_real = n_p + n_s

    x = jnp.concatenate([x_prompt.reshape(n_p, d), x_sample.reshape(n_s, d),
                         jnp.zeros((rows.sample_pad - n_s, d), F32)], axis=0)

    n_seq = batch + nb
    c_all = jnp.concatenate([c_prompt, c_sample, jnp.zeros((-n_seq % SUBLANES, d), F32)], axis=0)
    mod = _adaln(c_all, w_ada, b_ada)[:, :n_seq].reshape(n_layers, n_seq, 6, d)
    nxt = jnp.concatenate([mod[1:], jnp.zeros_like(mod[:1])], axis=0)
    mod9 = jnp.stack([mod[:, :, 0], mod[:, :, 1], jnp.zeros_like(mod[:, :, 0]),
                      mod[:, :, 2], mod[:, :, 3], mod[:, :, 4],
                      mod[:, :, 5], nxt[:, :, 0], nxt[:, :, 1]], axis=2).reshape(n_layers, n_seq, 3, 3, d)
    modp_all = jnp.pad(mod9[:, :batch], ((0, 0), (0, 0), (0, 0), (0, SUBLANES - 3), (0, 0)))
    mods_all = jnp.repeat(mod9[:, batch:], ts, axis=1).transpose(0, 2, 3, 1, 4)
    mods_all = jnp.pad(mods_all, ((0, 0), (0, 0), (0, 0), (0, rows.sample_pad - n_s), (0, 0)))

    w_in_t = w_in.transpose(0, 2, 1)
    o1 = pool_width
    o2 = o1 + q_width
    o3 = o2 + kv_width
    o4 = o3 + gn_width
    w_gm_t = w_in_t[:, o4:]
    w_gn_t = jnp.pad(w_in_t[:, o3:o4], ((0, 0), (0, -gn_width % LANES), (0, 0)))
    w_out_bf = w_out.astype(BF16)
    n_rt = N_GROUPS + N_EXPERTS
    w_router_t = jnp.concatenate([w_rg.transpose(0, 2, 1), w_re.transpose(0, 2, 1),
                                  jnp.zeros((n_layers, LANES - n_rt, d), F32)], axis=1)
    b_router = jnp.concatenate([b_rg, b_re, jnp.zeros((n_layers, LANES - n_rt), F32)], axis=1)
    cache_t = cache_kv.transpose(0, 1, 3, 4, 5, 2)
    win_t = cache_win.transpose(0, 1, 3, 4, 5, 2)
    state_pm = state_pool.transpose(0, 2, 1, 3)
    w2_t = cmp_w2.transpose(0, 1, 3, 2)

    slopes = jnp.exp2(-8.0 * jnp.arange(1, N_HEADS + 1, dtype=F32) / N_HEADS)
    nch_p = seq // CMP_STRIDE
    ncmp_p = (seq - CMP_LEN) // CMP_STRIDE + 1
    nsel_p = -(-seq // SEL_BLOCK)
    cover_p_t = _cover_matrix(nch_p, ncmp_p, nsel_p, -(-nsel_p // LANES) * LANES).T

    tk_s = past + ts
    ncmp_s = (tk_s - CMP_LEN) // CMP_STRIDE + 1
    nch_s = past // CMP_STRIDE
    assert (ncmp_s + CMP_SEGS - 1) * CMP_STRIDE <= past and nch_s >= ncmp_s
    assert page % CMP_STRIDE == 0 and page % SEL_BLOCK == 0 and page % LANES == 0 and ts <= LANES
    nsel_s = -(-tk_s // SEL_BLOCK)
    nsel_s_pad = -(-nsel_s // LANES) * LANES
    cover_s_t = _cover_matrix(nch_s, ncmp_s, nsel_s, nsel_s_pad).T
    nk_s = past + LANES
    expand_s = (jnp.arange(nsel_s_pad)[:, None] == jnp.arange(nk_s)[None, :] // SEL_BLOCK).astype(BF16)
    gt = KV_HEADS * ts
    slopes_gh = slopes.reshape(KV_HEADS, Q_PER_KV)
    slope_s = jnp.broadcast_to(slopes_gh.T[:, :, None], (Q_PER_KV, KV_HEADS, ts)).reshape(Q_PER_KV * gt, 1)
    qpos_s = jnp.broadcast_to(past + jnp.arange(ts, dtype=I32), (Q_PER_KV, KV_HEADS, ts)).reshape(Q_PER_KV * gt, 1)
    assert gt <= LANES
    qlane_s = jnp.pad(qpos_s[:gt, 0], (0, LANES - gt)).reshape(1, LANES)
    eye_g = jnp.eye(KV_HEADS, dtype=BF16)

    u = _modulate(rows, x, modp_all[0], mods_all[0])
    outs = {k: [] for k in ("kv_p", "win_p", "pool_p", "kv_s", "win_s", "pool_s")}
    for l in range(n_layers):
        up = _proj_rows(u, w_in_t, l, 0, pool_width, MM_TILE_N, F32)
        q = _proj_rows(u, w_in_t, l, o1, q_width, MM_TILE_N, BF16, scale=HEAD_DIM ** -0.5)
        kvc = _proj_rows(u, w_in_t, l, o2, 2 * GRP_WIDTH, MM_TILE_N, F32)
        kv_rest_s = _proj_rows(u, w_in_t, l, o2 + 2 * GRP_WIDTH, 4 * GRP_WIDTH, MM_TILE_N, F32,
                               m0=n_p, m_rows=rows.sample_pad)
        gn = _proj_rows(u, w_gn_t, l, 0, LANES, LANES, F32, act="sigmoid")
        kv_t = _proj_t(u, w_in_t, l, o2, N_CACHED_PLANES, batch, seq)
        kw_t = _proj_t(u, w_in_t, l, o2 + N_CACHED_PLANES * GRP_WIDTH, 2, batch, seq)

        kv_s = jnp.concatenate([kvc[n_p:n_real], kv_rest_s[:n_s]], axis=1)
        kv_s = kv_s.reshape(nb, ts, N_KV_PLANES, KV_HEADS, HEAD_DIM)
        up_p = up[:n_p].reshape(batch, seq, pool_width)
        up_s = up[n_p:n_real].reshape(nb, ts, pool_width)
        outs["kv_p"].append(kv_t.reshape(batch, N_CACHED_PLANES, KV_HEADS, HEAD_DIM, seq))
        n_keep = min(WINDOW, seq)
        outs["win_p"].append(kw_t[:, :, :, seq - n_keep:].reshape(batch, 2, KV_HEADS, HEAD_DIM, n_keep))
        hist_p = jnp.concatenate([jnp.zeros((batch, POOL_HIST, pool_width), F32), up_p[:, -POOL_HIST:]], axis=1)
        outs["pool_p"].append(hist_p[:, -POOL_HIST:].transpose(1, 0, 2))
        new_t = kv_s.transpose(0, 2, 3, 4, 1)
        win_new = jnp.concatenate([win_t[l], new_t[:, N_CACHED_PLANES:]], axis=-1)[..., -win_buf:]
        outs["kv_s"].append(kv_s[:, :, :N_CACHED_PLANES])
        outs["win_s"].append(win_new)
        pool_ext = jnp.concatenate([state_pm[l], up_s.transpose(1, 0, 2)], axis=0)
        outs["pool_s"].append(pool_ext[-POOL_HIST:])

        yp_p = _pool_prompt(up, batch, seq, pool_w[l], pool_scale[l])
        yp_s = _pool_sample(pool_ext, ts, past, pool_w[l], pool_scale[l])
        y_pool = jnp.concatenate([yp_p, yp_s.transpose(1, 0, 2).reshape(n_s, pool_width),
                                  jnp.zeros((rows.sample_pad - n_s, pool_width), BF16)], axis=0)

        w1 = cmp_w1[l].reshape(2, CMP_SEGS, chunk_w, cmp_hid)
        pe = cmp_pe[l].reshape(2, CMP_SEGS, 1, chunk_w)
        chunks_p = kvc[:n_p].reshape(batch, nch_p, CMP_STRIDE, 2, KV_HEADS, HEAD_DIM)
        chunks_p = chunks_p.transpose(3, 0, 4, 1, 2, 5).reshape(2, batch, KV_HEADS, nch_p, chunk_w)
        cmp_rows_p, cmp_t_p = _compress_prompt(chunks_p, w1, pe, w2_t[l])
        cmp_s = _compress_sample(page_table, cache_t, l, w1, pe, w2_t[l])

        gates_p = gn[:n_p, :gn_width].reshape(batch, seq, KV_HEADS, 3 * Q_PER_KV).transpose(0, 2, 3, 1)
        ya_p = _attn_prompt(slopes, q, batch, seq, cmp_rows_p, cmp_t_p, kv_t, kw_t, gates_p, cover_p_t)

        q_s = q[n_p:n_real].reshape(nb, ts, KV_HEADS, Q_PER_KV, HEAD_DIM).transpose(0, 3, 2, 1, 4)
        qbd = (q_s[:, :, :, :, None, :] * eye_g[None, None, :, None, :, None]).reshape(nb, Q_PER_KV * gt, GRP_WIDTH)
        new_blk = jnp.pad(new_t[:, 2:].reshape(nb, 4, GRP_WIDTH, ts), ((0, 0), (0, 0), (0, 0), (0, LANES - ts)))
        gates_s = gn[n_p:n_real, :gn_width].reshape(nb, ts, KV_HEADS, Q_PER_KV, 3).transpose(0, 3, 2, 1, 4)
        gates_s = gates_s.reshape(nb, Q_PER_KV * gt, 3)
        oa_s = _attn_sample(page_table, qbd, cmp_s, cache_t, new_blk, win_t, gates_s,
                            slope_s, qpos_s, qlane_s, cover_s_t, expand_s, layer=l, past=past, ts=ts, n_cmp=ncmp_s,
                            n_sel=nsel_s, gt=gt)
        ya_s = oa_s.reshape(nb, Q_PER_KV, KV_HEADS, ts, HEAD_DIM).transpose(0, 3, 2, 1, 4).reshape(n_s, q_width)
        y_attn = jnp.concatenate([ya_p, ya_s.astype(BF16),
                                  jnp.zeros((rows.sample_pad - n_s, q_width), BF16)], axis=0)

        merged = _merge(u, y_pool, y_attn, w_gm_t, w_up_pool, w_up_nsa, l)
        x1, u2, route = _outproj(rows, merged, x, w_out_bf[l], modp_all[l], mods_all[l], ln1_g[l], ln1_b[l],
                                 w_router_t[l], b_router[l].reshape(1, LANES), alpha)

        pos, pads, tile_e, n_used, r_pad = _moe_dispatch(route, n_real, n_rows)
        xs = _moe_scatter(pos, pads, u2, n_real, r_pad)
        y_rows = _moe_experts(tile_e, n_used, xs, w_gate, w_up, w_down, l)
        x, u = _moe_combine(rows, pos, y_rows, x1, route, modp_all[l], mods_all[l], ln2_g[l], ln2_b[l], alpha,
                            emit_next=l + 1 < n_layers)

    y_prompt = x[:n_p].reshape(batch, seq, d)
    y_sample = x[n_p:n_real].reshape(nb, ts, d)
    new_kv_p = jnp.stack(outs["kv_p"]).transpose(0, 1, 5, 2, 3, 4)
    new_win_p = jnp.stack(outs["win_p"]).transpose(0, 1, 5, 2, 3, 4)
    new_pool_p = jnp.stack(outs["pool_p"]).transpose(0, 2, 1, 3)
    new_win_s = jnp.stack(outs["win_s"]).transpose(0, 1, 5, 2, 3, 4)
    new_pool_s = jnp.stack(outs["pool_s"]).transpose(0, 2, 1, 3)
    return (y_prompt, y_sample, new_kv_p, new_win_p, new_pool_p, jnp.stack(outs["kv_s"]), new_win_s, new_pool_s)
```

```python
import functools

import jax
import jax.numpy as jnp
from jax import lax
from jax.experimental import pallas as pl
from jax.experimental.pallas import tpu as pltpu

F32 = jnp.float32
BF16 = jnp.bfloat16
I32 = jnp.int32

POOL_WINDOWS = (2, 4, 8, 16)
POOL_HIST = max(POOL_WINDOWS) - 1
N_HEADS = 16
KV_HEADS = 4
HEAD_DIM = 64
Q_PER_KV = N_HEADS // KV_HEADS
GRP_WIDTH = KV_HEADS * HEAD_DIM
CMP_LEN = 32
CMP_STRIDE = 16
CMP_SEGS = CMP_LEN // CMP_STRIDE
SEL_BLOCK = 64
SEL_TOPK = 16
WINDOW = 512
Q_BLOCK = 256
N_KV_PLANES = 6
N_CACHED_PLANES = 4
FORCE_BONUS = 1.0e4
N_GROUPS = 4
EXPERTS_PER_GROUP = 4
N_EXPERTS = N_GROUPS * EXPERTS_PER_GROUP
LN_EPS = 1e-5
NEG_INF = -1e30
NEG_CLAMP = -1e29

LANES = 128
SUBLANES = 8
VMEM_LIMIT = 52 * 1024 * 1024

ROW_TILE = 256
MM_TILE_M = 512
MM_TILE_N = 512
SEL_KV_TILE = 512
EXPERT_TILE = 256
CMP_K_GROUP = 4


def _cparams(*sem):
    return pltpu.CompilerParams(dimension_semantics=sem, vmem_limit_bytes=VMEM_LIMIT)


def _dot(a, b):
    return jnp.dot(a, b, preferred_element_type=F32)


def _dot_nt(a, b):
    return lax.dot_general(a, b, (((1,), (1,)), ((), ())), preferred_element_type=F32)


def _split_bf16(a):
    hi = a.astype(BF16)
    return hi, (a - hi.astype(F32)).astype(BF16)


def _masked_softmax(s, valid):
    s = jnp.where(valid, s, NEG_INF)
    m = jnp.max(s, axis=-1, keepdims=True)
    e = jnp.where(valid, jnp.exp(s - m), 0.0)
    l = jnp.sum(e, axis=-1, keepdims=True)
    return e / jnp.where(l > 0.0, l, 1.0)


def _layer_norm(x, g, b):
    mu = jnp.mean(x, axis=-1, keepdims=True)
    xc = x - mu
    var = jnp.mean(xc * xc, axis=-1, keepdims=True)
    return xc * lax.rsqrt(var + LN_EPS) * g + b


def _block_scores(imp, blk, qpos, n_sel):
    cur = qpos // SEL_BLOCK
    forced = (blk == 0) | (blk == cur) | (blk == cur - 1)
    visible = blk * SEL_BLOCK <= qpos
    score = jnp.where(visible, imp + jnp.where(forced, FORCE_BONUS, 0.0), -1.0)
    return jnp.where(blk < n_sel, score, -2.0)


def _adaln_kernel(c_ref, w_ref, b_ref, o_ref):
    c = c_ref[...]
    a = (c * jax.nn.sigmoid(c)).astype(BF16)
    o_ref[0] = _dot(a, w_ref[0].astype(BF16)) + b_ref[0]


def _adaln(c_all, w_ada, b_ada):
    n_layers, d, n6 = w_ada.shape
    mp = c_all.shape[0]
    tn = 1024
    assert n6 % tn == 0
    return pl.pallas_call(
        _adaln_kernel,
        grid=(n_layers, n6 // tn),
        in_specs=[pl.BlockSpec((mp, d), lambda l, n: (0, 0)),
                  pl.BlockSpec((1, d, tn), lambda l, n: (l, 0, n)),
                  pl.BlockSpec((1, 1, tn), lambda l, n: (l, 0, n))],
        out_specs=pl.BlockSpec((1, mp, tn), lambda l, n: (l, 0, n)),
        out_shape=jax.ShapeDtypeStruct((n_layers, mp, n6), F32),
        compiler_params=_cparams("arbitrary", "arbitrary"),
        name="adaln",
    )(c_all, w_ada, b_ada.reshape(n_layers, 1, n6))


class _Rows:
    def __init__(self, batch, seq, n_sample):
        self.batch = batch
        self.seq = seq
        self.n_prompt = batch * seq
        self.n_sample = n_sample
        self.sample_pad = -(-n_sample // MM_TILE_M) * MM_TILE_M
        self.n_rows = self.n_prompt + self.sample_pad
        assert seq % MM_TILE_M == 0 and MM_TILE_M % ROW_TILE == 0
        self.n_ptiles = self.n_prompt // ROW_TILE

    def mod_specs(self, d, group):
        tiles_per_seq = self.seq // ROW_TILE
        last = self.batch - 1
        n_pt = self.n_ptiles
        return [pl.BlockSpec((1, 1, SUBLANES, d),
                             lambda i, *_: (jnp.minimum(i // tiles_per_seq, last), group, 0, 0)),
                pl.BlockSpec((1, 3, ROW_TILE, d),
                             lambda i, *_: (group, 0, jnp.maximum(i - n_pt, 0), 0))]


def _mod_row(is_sample, mp_ref, ms_ref, k):
    return jnp.where(is_sample, ms_ref[0, k], mp_ref[0, 0, k:k + 1, :])


def _modulate_kernel(x_ref, mp_ref, ms_ref, u_ref, *, n_ptiles):
    is_s = pl.program_id(0) >= n_ptiles
    shift = _mod_row(is_s, mp_ref, ms_ref, 0)
    scale = _mod_row(is_s, mp_ref, ms_ref, 1)
    u_ref[...] = (x_ref[...] * (1.0 + scale) + shift).astype(BF16)


def _modulate(rows, x, modp, mods):
    d = x.shape[1]
    return pl.pallas_call(
        functools.partial(_modulate_kernel, n_ptiles=rows.n_ptiles),
        grid=(rows.n_rows // ROW_TILE,),
        in_specs=[pl.BlockSpec((ROW_TILE, d), lambda i: (i, 0))] + rows.mod_specs(d, 0),
        out_specs=pl.BlockSpec((ROW_TILE, d), lambda i: (i, 0)),
        out_shape=jax.ShapeDtypeStruct((rows.n_rows, d), BF16),
        compiler_params=_cparams("arbitrary"),
        name="modulate",
    )(x, modp, mods)


def _proj_rows_kernel(x_ref, w_ref, o_ref, wb_ref, *, act, scale):
    @pl.when(pl.program_id(1) == 0)
    def _():
        wb_ref[...] = w_ref[0].astype(BF16)

    y = _dot_nt(x_ref[...], wb_ref[...])
    if act == "sigmoid":
        y = jax.nn.sigmoid(y)
    if scale != 1.0:
        y = y * scale
    o_ref[...] = y.astype(o_ref.dtype)


def _proj_rows(x, w_t, layer, row0, ncols, tn, out_dtype, act=None, scale=1.0, m0=0, m_rows=None):
    k = x.shape[1]
    m_rows = x.shape[0] - m0 if m_rows is None else m_rows
    assert m0 % MM_TILE_M == 0 and m_rows % MM_TILE_M == 0 and row0 % tn == 0 and ncols % tn == 0
    mb, rb = m0 // MM_TILE_M, row0 // tn
    return pl.pallas_call(
        functools.partial(_proj_rows_kernel, act=act, scale=scale),
        grid=(ncols // tn, m_rows // MM_TILE_M),
        in_specs=[pl.BlockSpec((MM_TILE_M, k), lambda n, i: (mb + i, 0)),
                  pl.BlockSpec((1, tn, k), lambda n, i: (layer, rb + n, 0))],
        out_specs=pl.BlockSpec((MM_TILE_M, tn), lambda n, i: (i, n)),
        out_shape=jax.ShapeDtypeStruct((m_rows, ncols), out_dtype),
        scratch_shapes=[pltpu.VMEM((tn, k), BF16)],
        compiler_params=_cparams("arbitrary", "arbitrary"),
        name="proj_rows",
    )(x, w_t)


def _proj_t_kernel(w_ref, x_ref, o_ref, wb_ref):
    @pl.when((pl.program_id(1) == 0) & (pl.program_id(2) == 0))
    def _():
        wb_ref[...] = w_ref[0].astype(BF16)

    o_ref[0, 0] = _dot_nt(wb_ref[...], x_ref[...])


def _proj_t(x, w_t, layer, row0, n_planes, batch, seq):
    k = x.shape[1]
    tt = 2 * MM_TILE_M if seq % (2 * MM_TILE_M) == 0 else MM_TILE_M
    assert row0 % GRP_WIDTH == 0 and seq % tt == 0
    rb = row0 // GRP_WIDTH
    nt = seq // tt
    return pl.pallas_call(
        _proj_t_kernel,
        grid=(n_planes, batch, nt),
        in_specs=[pl.BlockSpec((1, GRP_WIDTH, k), lambda p, b, i: (layer, rb + p, 0)),
                  pl.BlockSpec((tt, k), lambda p, b, i: (b * nt + i, 0))],
        out_specs=pl.BlockSpec((1, 1, GRP_WIDTH, tt), lambda p, b, i: (b, p, 0, i)),
        out_shape=jax.ShapeDtypeStruct((batch, n_planes, GRP_WIDTH, seq), F32),
        scratch_shapes=[pltpu.VMEM((GRP_WIDTH, k), BF16)],
        compiler_params=_cparams("arbitrary", "arbitrary", "arbitrary"),
        name="proj_t",
    )(w_t, x)


def _pool_prompt_kernel(cur_ref, prev_ref, pw_ref, sc_ref, o_ref, ext_ref, *, tt, pg):
    i = pl.program_id(1)
    halo = 2 * SUBLANES
    ext_ref[0:halo, :] = jnp.where(i == 0, 0.0, prev_ref[...])
    ext_ref[halo:, :] = cur_ref[...]
    pos = i * tt + lax.broadcasted_iota(I32, (tt, 1), 0)
    for g, w in enumerate(POOL_WINDOWS):
        cols = slice(g * pg, (g + 1) * pg)
        cur = ext_ref[halo:halo + tt, cols]
        acc = cur
        for j in range(1, w):
            acc = acc + ext_ref[halo - j:halo - j + tt, cols]
        count = jnp.minimum(pos + 1, w).astype(F32)
        dlt = acc / count - cur
        y = _dot(dlt.astype(BF16), pw_ref[g].astype(BF16)) * sc_ref[:, cols]
        o_ref[:, cols] = y.astype(o_ref.dtype)


def _pool_prompt(up, batch, seq, pool_w, pool_scale):
    pw = up.shape[1]
    pg = pw // len(POOL_WINDOWS)
    tt = MM_TILE_M
    halo = 2 * SUBLANES
    assert POOL_HIST <= halo and seq % tt == 0
    nt = seq // tt
    return pl.pallas_call(
        functools.partial(_pool_prompt_kernel, tt=tt, pg=pg),
        grid=(batch, nt),
        in_specs=[pl.BlockSpec((tt, pw), lambda b, i: (b * nt + i, 0)),
                  pl.BlockSpec((halo, pw), lambda b, i: (jnp.maximum((b * nt + i) * (tt // halo) - 1, 0), 0)),
                  pl.BlockSpec((len(POOL_WINDOWS), pg, pg), lambda b, i: (0, 0, 0)),
                  pl.BlockSpec((1, pw), lambda b, i: (0, 0))],
        out_specs=pl.BlockSpec((tt, pw), lambda b, i: (b * nt + i, 0)),
        out_shape=jax.ShapeDtypeStruct((batch * seq, pw), BF16),
        scratch_shapes=[pltpu.VMEM((halo + tt, pw), F32)],
        compiler_params=_cparams("arbitrary", "arbitrary"),
        name="pool_prompt",
    )(up, up, pool_w, pool_scale.reshape(1, pw))


def _pool_sample_kernel(ext_ref, pw_ref, sc_ref, o_ref, *, ts, pg, pos0):
    for t in range(ts):
        for g, w in enumerate(POOL_WINDOWS):
            cols = slice(g * pg, (g + 1) * pg)
            cur = ext_ref[POOL_HIST + t, :, cols]
            acc = cur
            for j in range(1, w):
                acc = acc + ext_ref[POOL_HIST + t - j, :, cols]
            count = float(min(pos0 + t + 1, w))
            dlt = acc / count - cur
            y = _dot(dlt.astype(BF16), pw_ref[g].astype(BF16)) * sc_ref[:, cols]
            o_ref[t, :, cols] = y.astype(o_ref.dtype)


def _pool_sample(ext, ts, pos0, pool_w, pool_scale):
    n_ext, nb, pw = ext.shape
    pg = pw // len(POOL_WINDOWS)
    return pl.pallas_call(
        functools.partial(_pool_sample_kernel, ts=ts, pg=pg, pos0=pos0),
        grid=(1,),
        in_specs=[pl.BlockSpec((n_ext, nb, pw), lambda i: (0, 0, 0)),
                  pl.BlockSpec((len(POOL_WINDOWS), pg, pg), lambda i: (0, 0, 0)),
                  pl.BlockSpec((1, pw), lambda i: (0, 0))],
        out_specs=pl.BlockSpec((ts, nb, pw), lambda i: (0, 0, 0)),
        out_shape=jax.ShapeDtypeStruct((ts, nb, pw), BF16),
        compiler_params=_cparams("arbitrary"),
        name="pool_sample",
    )(ext, pool_w, pool_scale.reshape(1, pw))


def _compress_tail(ha, hb, w2t_ref):
    n_chunk = ha.shape[0]
    h = ha + pltpu.roll(hb, n_chunk - 1, 0)
    act = jax.nn.gelu(h).astype(BF16)
    w2t = w2t_ref[0].astype(BF16)
    return _dot_nt(act, w2t), _dot_nt(w2t, act)


def _compress_prompt_kernel(c_ref, w1_ref, pe_ref, w2t_ref, o_ref, ot_ref):
    c = c_ref[0, 0, 0]
    ha = _dot((c + pe_ref[0, 0]).astype(BF16), w1_ref[0, 0].astype(BF16))
    hb = _dot((c + pe_ref[0, 1]).astype(BF16), w1_ref[0, 1].astype(BF16))
    o_ref[0, 0, 0], ot_ref[0, 0, 0] = _compress_tail(ha, hb, w2t_ref)


def _compress_prompt(chunks, w1, pe, w2t):
    _, b, g, n_chunk, kdim = chunks.shape
    hid = w1.shape[-1]
    return pl.pallas_call(
        _compress_prompt_kernel,
        grid=(2, b, g),
        in_specs=[pl.BlockSpec((1, 1, 1, n_chunk, kdim), lambda p, bi, gi: (p, bi, gi, 0, 0)),
                  pl.BlockSpec((1, CMP_SEGS, kdim, hid), lambda p, bi, gi: (p, 0, 0, 0)),
                  pl.BlockSpec((1, CMP_SEGS, 1, kdim), lambda p, bi, gi: (p, 0, 0, 0)),
                  pl.BlockSpec((1, HEAD_DIM, hid), lambda p, bi, gi: (p, 0, 0))],
        out_specs=[pl.BlockSpec((1, 1, 1, n_chunk, HEAD_DIM), lambda p, bi, gi: (p, bi, gi, 0, 0)),
                   pl.BlockSpec((1, 1, 1, HEAD_DIM, n_chunk), lambda p, bi, gi: (p, bi, gi, 0, 0))],
        out_shape=[jax.ShapeDtypeStruct((2, b, g, n_chunk, HEAD_DIM), F32),
                   jax.ShapeDtypeStruct((2, b, g, HEAD_DIM, n_chunk), F32)],
        compiler_params=_cparams("arbitrary", "arbitrary", "arbitrary"),
        name="compress_prompt",
    )(chunks, w1, pe, w2t)


def _compress_sample_kernel(pt_ref, cache_ref, w1_ref, pe_ref, w2t_ref, perm_ref, o_ref, buf_ref, rows_ref,
                            w1b_ref, peb_ref, sem, *, layer, n_pages, page, nb):
    p = pl.program_id(0)
    b = pl.program_id(1)
    step = p * nb + b
    slot = step % 2
    n_chunk = n_pages * page // CMP_STRIDE
    kg = CMP_K_GROUP
    n_q = CMP_STRIDE // kg

    def page_copy(pp, bb, sl, j):
        pid = pt_ref[bb * n_pages + j]
        return pltpu.make_async_copy(cache_ref.at[layer, pid, pp], buf_ref.at[sl, j], sem.at[sl])

    def fetch(pp, bb, sl):
        def issue(j, c):
            page_copy(pp, bb, sl, j).start()
            return c

        lax.fori_loop(0, n_pages, issue, 0)

    @pl.when(step == 0)
    def _():
        fetch(p, b, slot)

    @pl.when(step + 1 < 2 * nb)
    def _():
        fetch((step + 1) // nb, (step + 1) % nb, 1 - slot)

    @pl.when(b == 0)
    def _():
        w1b_ref[...] = w1_ref[0].astype(BF16)
        bias = jnp.zeros(peb_ref.shape, F32)
        for m in range(CMP_SEGS):
            for q in range(n_q):
                pe_rows = jnp.broadcast_to(pe_ref[0, m, q], (SUBLANES, kg * HEAD_DIM)).astype(BF16)
                bias = bias + _dot(pe_rows, w1b_ref[m, q])
        peb_ref[...] = bias

    def drain(j, c):
        page_copy(p, b, slot, j).wait()
        return c

    lax.fori_loop(0, n_pages, drain, 0)
    cpp = page // CMP_STRIDE

    def to_rows(j, c):
        x_t = buf_ref[slot, j].reshape(GRP_WIDTH, page).astype(BF16)
        x = _dot_nt(perm_ref[...], x_t)
        c0 = pl.multiple_of(j * cpp, cpp)
        for s in range(CMP_STRIDE):
            rows_ref[s, pl.ds(c0, cpp), :] = x[s * cpp:(s + 1) * cpp, :]
        return c

    lax.fori_loop(0, n_pages, to_rows, 0, unroll=16)
    out = []
    for g in range(KV_HEADS):
        lanes = slice(g * HEAD_DIM, (g + 1) * HEAD_DIM)
        acc = [None] * CMP_SEGS
        for q in range(n_q):
            piece = jnp.concatenate([rows_ref[q * kg + k][:, lanes] for k in range(kg)], axis=1).astype(BF16)
            for m in range(CMP_SEGS):
                t = _dot(piece, w1b_ref[m, q])
                acc[m] = t if acc[m] is None else acc[m] + t
        out.append(_compress_tail(acc[0] + peb_ref[0:1, :], acc[1], w2t_ref)[0])
    o_ref[0, 0] = jnp.concatenate(out, axis=1)


def _compress_sample(page_table, cache_t, layer, w1, pe, w2t):
    nb, n_pages = page_table.shape
    page = cache_t.shape[-1]
    n_chunk = n_pages * page // CMP_STRIDE
    hid = w1.shape[-1]
    kg = CMP_K_GROUP
    w1g = w1.reshape(2, CMP_SEGS, CMP_STRIDE // kg, kg * HEAD_DIM, hid)
    peg = pe.reshape(2, CMP_SEGS, CMP_STRIDE // kg, 1, kg * HEAD_DIM)
    cpp = page // CMP_STRIDE
    row = jnp.arange(page)
    perm = ((CMP_STRIDE * (row % cpp) + row // cpp)[:, None] == jnp.arange(page)[None, :]).astype(BF16)
    assert page == LANES and KV_HEADS % 2 == 0
    grid_spec = pltpu.PrefetchScalarGridSpec(
        num_scalar_prefetch=1,
        grid=(2, nb),
        in_specs=[pl.BlockSpec(memory_space=pl.ANY),
                  pl.BlockSpec((1,) + w1g.shape[1:], lambda p, bi, pt: (p, 0, 0, 0, 0)),
                  pl.BlockSpec((1,) + peg.shape[1:], lambda p, bi, pt: (p, 0, 0, 0, 0)),
                  pl.BlockSpec((1, HEAD_DIM, hid), lambda p, bi, pt: (p, 0, 0)),
                  pl.BlockSpec((page, page), lambda p, bi, pt: (0, 0))],
        out_specs=pl.BlockSpec((1, 1, n_chunk, GRP_WIDTH), lambda p, bi, pt: (p, bi, 0, 0)),
        scratch_shapes=[pltpu.VMEM((2, n_pages, KV_HEADS, HEAD_DIM, page), F32),
                        pltpu.VMEM((CMP_STRIDE, n_chunk, GRP_WIDTH), F32),
                        pltpu.VMEM(w1g.shape[1:], BF16),
                        pltpu.VMEM((SUBLANES, hid), F32),
                        pltpu.SemaphoreType.DMA((2,))],
    )
    return pl.pallas_call(
        functools.partial(_compress_sample_kernel, layer=layer, n_pages=n_pages, page=page, nb=nb),
        grid_spec=grid_spec,
        out_shape=jax.ShapeDtypeStruct((2, nb, n_chunk, GRP_WIDTH), F32),
        compiler_params=_cparams("arbitrary", "arbitrary"),
        name="compress_sample",
    )(page_table.reshape(-1), cache_t, w1g, peg, w2t, perm)


def _attn_prompt_kernel(slope_ref, q_ref, kc_ref, vct_ref, ks_ref, vs_ref, kw_ref, vw_ref, gate_ref, covert_ref,
                        o_ref, m_ref, l_ref, acc_ref, st_ref, bias_ref, s_ref, z_ref, e_ref, flag_ref,
                        *, seq, n_cmp, n_sel, n_top, wlen):
    qb = Q_BLOCK
    tk = SEL_KV_TILE
    g = pl.program_id(1)
    p0 = pl.program_id(2) * qb
    qpos = p0 + lax.broadcasted_iota(I32, (1, qb), 1)
    slopes = [slope_ref[g * Q_PER_KV + h] for h in range(Q_PER_KV)]
    q_t = q_ref[...].astype(F32).T.astype(BF16)
    q_all = jnp.concatenate([q_t[h * HEAD_DIM:(h + 1) * HEAD_DIM] for h in range(Q_PER_KV)], axis=1)
    head = lambda a, h: a[:, h * qb:(h + 1) * qb]

    def col_softmax(z):
        m = jnp.maximum(jnp.max(z, axis=0, keepdims=True), NEG_CLAMP)
        e = jnp.exp(z - m)
        l = jnp.sum(e, axis=0, keepdims=True)
        return e, jnp.where(l > 0.0, l, 1.0)

    n_chunk = kc_ref.shape[3]
    crow = lax.broadcasted_iota(I32, (n_chunk, 1), 0)
    d_c = qpos - (crow * CMP_STRIDE + (CMP_LEN - 1))
    mask_c = jnp.where((d_c >= 0) & (crow < n_cmp), 0.0, NEG_INF)
    d_cf = d_c.astype(F32)
    vct = vct_ref[0, 0, 0].astype(BF16)
    s_c = _dot(kc_ref[0, 0, 0].astype(BF16), q_all)
    p_sum = jnp.zeros((n_chunk, qb), F32)
    o_c = []
    for h in range(Q_PER_KV):
        e, l = col_softmax(head(s_c, h) - slopes[h] * d_cf + mask_c)
        p = e / l
        p_sum = p_sum + p
        o_c.append(_dot(vct, p.astype(BF16)))

    p_hi, p_lo = _split_bf16(p_sum)
    cover_t = covert_ref[...]
    imp_t = _dot(cover_t, p_hi) + _dot(cover_t, p_lo)
    n_blk = -(-n_sel // SUBLANES) * SUBLANES
    blk = lax.broadcasted_iota(I32, (n_blk, 1), 0)
    score = _block_scores(imp_t[0:n_blk], blk, qpos, n_sel)
    st_ref[0:n_blk, :] = score

    def count_beaten(k, cnt):
        row = st_ref[pl.ds(k, 1), :]
        return cnt + jnp.where((row > score) | ((row == score) & (k < blk)), 1.0, 0.0)

    rank = lax.fori_loop(0, n_sel, count_beaten, jnp.zeros(score.shape, F32), unroll=4)
    chosen = rank < n_top
    st_ref[0:n_blk, :] = jnp.where(chosen, 0.0, NEG_INF)
    bpt = SEL_KV_TILE // SEL_BLOCK
    for t in range(n_blk // bpt):
        flag_ref[t] = jnp.max(jnp.where(chosen[t * bpt:(t + 1) * bpt], 1.0, 0.0)).astype(I32)

    krow = lax.broadcasted_iota(I32, (tk, 1), 0)
    @pl.when(pl.program_id(2) == 0)
    def _():
        rel = (krow - (qpos - p0)).astype(F32)
        for h in range(Q_PER_KV):
            bias_ref[h] = slopes[h] * rel
    m_ref[...] = jnp.full(m_ref.shape, NEG_INF, F32)
    l_ref[...] = jnp.zeros(l_ref.shape, F32)
    acc_ref[...] = jnp.zeros(acc_ref.shape, F32)
    n_tiles = (p0 + qb + tk - 1) // tk

    def sweep_tile(jt, causal):
        k0 = pl.multiple_of(jt * tk, tk)
        k_rows = ks_ref[0, 0, :, pl.ds(k0, tk)].T.astype(BF16)
        v_t = vs_ref[0, 0, :, pl.ds(k0, tk)].astype(BF16)
        s_ref[...] = _dot(k_rows, q_all)
        j0 = k0 // SEL_BLOCK
        off = (k0 - p0).astype(F32)
        nblk = tk // SEL_BLOCK
        fold = lambda a, op: op(a.reshape(SEL_BLOCK // SUBLANES, SUBLANES, qb), axis=0)
        for h in range(Q_PER_KV):
            mx = None
            for c in range(nblk):
                rows = slice(c * SEL_BLOCK, (c + 1) * SEL_BLOCK)
                madd = jnp.broadcast_to(st_ref[pl.ds(j0 + c, 1), :], (SEL_BLOCK, qb))
                if causal:
                    madd = jnp.where(k0 + krow[rows] <= qpos, madd, NEG_INF)
                z = s_ref[rows, h * qb:(h + 1) * qb] + bias_ref[h, rows, :] + madd
                z_ref[h, rows, :] = z
                cm = fold(z, jnp.max)
                mx = cm if mx is None else jnp.maximum(mx, cm)
            shift = slopes[h] * off
            m_old = m_ref[h]
            m_new = jnp.maximum(m_old, jnp.max(mx, axis=0, keepdims=True) + shift)
            m_use = jnp.maximum(m_new, NEG_CLAMP)
            sm = None
            for c in range(nblk):
                rows = slice(c * SEL_BLOCK, (c + 1) * SEL_BLOCK)
                e = jnp.exp(z_ref[h, rows, :] - (m_use - shift))
                e_ref[h, rows, :] = e.astype(BF16)
                cs = fold(e, jnp.sum)
                sm = cs if sm is None else sm + cs
            alpha = jnp.exp(m_old - m_use)
            l_ref[h] = alpha * l_ref[h] + jnp.sum(sm, axis=0, keepdims=True)
            acc_ref[h] = alpha * acc_ref[h] + _dot(v_t, e_ref[h])
            m_ref[h] = m_new

    def sweep_body(jt, carry):
        @pl.when(flag_ref[jt] > 0)
        def _():
            sweep_tile(jt, False)

        return carry

    lax.fori_loop(0, n_tiles - 1, sweep_body, 0)
    sweep_tile(n_tiles - 1, True)

    w0 = pl.multiple_of(jnp.minimum(jnp.maximum(p0 - WINDOW, 0), seq - wlen), qb)
    relw = (w0 + lax.broadcasted_iota(I32, (wlen, 1), 0)) - qpos
    mask_w = jnp.where((relw <= 0) & (relw > -WINDOW), 0.0, NEG_INF)
    relw_f = relw.astype(F32)
    kw_rows = kw_ref[0, 0, :, pl.ds(w0, wlen)].T.astype(BF16)
    vw_t = vw_ref[0, 0, :, pl.ds(w0, wlen)].astype(BF16)
    s_w = _dot(kw_rows, q_all)

    out = []
    for h in range(Q_PER_KV):
        e, l = col_softmax(head(s_w, h) + slopes[h] * relw_f + mask_w)
        o_w = _dot(vw_t, e.astype(BF16)) / l
        l_s = l_ref[h]
        o_s = acc_ref[h] / jnp.where(l_s > 0.0, l_s, 1.0)
        g0 = gate_ref[0, 0, 3 * h + 0:3 * h + 1, :]
        g1 = gate_ref[0, 0, 3 * h + 1:3 * h + 2, :]
        g2 = gate_ref[0, 0, 3 * h + 2:3 * h + 3, :]
        out.append(g0 * o_c[h] + g1 * o_s + g2 * o_w)
    o_ref[...] = jnp.concatenate(out, axis=0).T.astype(o_ref.dtype)


def _attn_prompt(slopes, q, batch, seq, cmp_rows, cmp_t, kv_t, win_t, gates_t, cover_t):
    n_chunk = cmp_rows.shape[3]
    n_cmp = (seq - CMP_LEN) // CMP_STRIDE + 1
    n_sel = -(-seq // SEL_BLOCK)
    n_top = min(SEL_TOPK, n_sel)
    wlen = min(WINDOW + Q_BLOCK, seq)
    nsp = cover_t.shape[0]
    nqb = seq // Q_BLOCK
    assert seq % SEL_KV_TILE == 0 and seq % Q_BLOCK == 0 and n_chunk >= n_cmp and wlen % LANES == 0

    def plane(pidx):
        return pl.BlockSpec((1, 1, HEAD_DIM, seq), lambda bi, gi, i: (bi, pidx, gi, 0))

    return pl.pallas_call(
        functools.partial(_attn_prompt_kernel, seq=seq, n_cmp=n_cmp, n_sel=n_sel, n_top=n_top, wlen=wlen),
        grid=(batch, KV_HEADS, nqb),
        in_specs=[pl.BlockSpec(memory_space=pltpu.SMEM),
                  pl.BlockSpec((Q_BLOCK, GRP_WIDTH), lambda bi, gi, i: (bi * nqb + i, gi)),
                  pl.BlockSpec((1, 1, 1, n_chunk, HEAD_DIM), lambda bi, gi, i: (0, bi, gi, 0, 0)),
                  pl.BlockSpec((1, 1, 1, HEAD_DIM, n_chunk), lambda bi, gi, i: (1, bi, gi, 0, 0)),
                  plane(2), plane(3), plane(0), plane(1),
                  pl.BlockSpec((1, 1, 3 * Q_PER_KV, Q_BLOCK), lambda bi, gi, i: (bi, gi, 0, i)),
                  pl.BlockSpec(cover_t.shape, lambda bi, gi, i: (0, 0))],
        out_specs=pl.BlockSpec((Q_BLOCK, GRP_WIDTH), lambda bi, gi, i: (bi * nqb + i, gi)),
        out_shape=jax.ShapeDtypeStruct((batch * seq, N_HEADS * HEAD_DIM), BF16),
        scratch_shapes=[pltpu.VMEM((Q_PER_KV, 1, Q_BLOCK), F32), pltpu.VMEM((Q_PER_KV, 1, Q_BLOCK), F32),
                        pltpu.VMEM((Q_PER_KV, HEAD_DIM, Q_BLOCK), F32), pltpu.VMEM((nsp, Q_BLOCK), F32),
                        pltpu.VMEM((Q_PER_KV, SEL_KV_TILE, Q_BLOCK), F32),
                        pltpu.VMEM((SEL_KV_TILE, Q_PER_KV * Q_BLOCK), F32),
                        pltpu.VMEM((Q_PER_KV, SEL_KV_TILE, Q_BLOCK), F32),
                        pltpu.VMEM((Q_PER_KV, SEL_KV_TILE, Q_BLOCK), BF16),
                        pltpu.SMEM((seq // SEL_KV_TILE,), I32)],
        compiler_params=_cparams("arbitrary", "arbitrary", "arbitrary"),
        name="attn_prompt",
    )(slopes, q, cmp_rows, cmp_t, kv_t, kv_t, win_t, win_t, gates_t, cover_t)


def _attn_sample_kernel(pt_ref, qbd_ref, kc_ref, vc_ref, cache_ref, new_ref, win_ref, gate_ref, slope_ref,
                        qpos_ref, qlane_ref, cover_ref, expand_ref, o_ref, kt_ref, vt_ref, st_ref, sem,
                        *, layer, n_pages, page, past, ts, n_cmp, n_sel, n_top, win_buf, gt):
    b = pl.program_id(0)
    nk = vt_ref.shape[2]
    slot = b % 2
    v_sem = 2

    def k_copy(bb, sl, j):
        pid = pt_ref[bb * n_pages + j]
        dst = pl.ds(pl.multiple_of(j * page, page), page)
        return pltpu.make_async_copy(cache_ref.at[layer, pid, 2], kt_ref.at[sl, :, :, dst], sem.at[sl])

    def v_copy(j):
        pid = pt_ref[b * n_pages + j]
        dst = pl.ds(pl.multiple_of(j * page, page), page)
        return pltpu.make_async_copy(cache_ref.at[layer, pid, 3], vt_ref.at[:, :, dst], sem.at[v_sem])

    def fetch_keys(bb, sl):
        def issue_k(j, c):
            k_copy(bb, sl, j).start()
            return c

        lax.fori_loop(0, n_pages, issue_k, 0)

    def issue_v(j, c):
        v_copy(j).start()
        return c

    def drain(j, c):
        k_copy(b, slot, j).wait()
        v_copy(j).wait()
        return c

    @pl.when(b == 0)
    def _():
        fetch_keys(b, slot)

    lax.fori_loop(0, n_pages, issue_v, 0)

    @pl.when(b + 1 < pl.num_programs(0))
    def _():
        fetch_keys(b + 1, 1 - slot)

    kt_ref[slot, :, :, past:] = new_ref[0, 0].reshape(KV_HEADS, HEAD_DIM, nk - past)
    vt_ref[:, :, past:] = new_ref[0, 1].reshape(KV_HEADS, HEAD_DIM, nk - past)

    q = qbd_ref[0]
    nrow = q.shape[0]
    slope = slope_ref[...]
    qpos = qpos_ref[...]
    lane_grp = lax.broadcasted_iota(I32, (nrow, GRP_WIDTH), 1) // HEAD_DIM
    row_grp = (lax.broadcasted_iota(I32, (nrow, GRP_WIDTH), 0) % gt) // (gt // KV_HEADS)
    diag = lane_grp == row_grp

    def own_group(full):
        kept = jnp.where(diag, full, 0.0)
        out = kept[:, 0:HEAD_DIM]
        for g in range(1, KV_HEADS):
            out = out + kept[:, g * HEAD_DIM:(g + 1) * HEAD_DIM]
        return out

    n_chunk = kc_ref.shape[2]
    ccol = lax.broadcasted_iota(I32, (1, n_chunk), 1)
    d_c = qpos - (ccol * CMP_STRIDE + (CMP_LEN - 1))
    s_c = _dot_nt(q, kc_ref[0, 0].astype(BF16)) - slope * d_c.astype(F32)
    p_c = _masked_softmax(s_c, (d_c >= 0) & (ccol < n_cmp))
    o_c = own_group(_dot(p_c.astype(BF16), vc_ref[0, 0].astype(BF16)))

    p_sum = p_c[0:gt]
    for h in range(1, Q_PER_KV):
        p_sum = p_sum + p_c[h * gt:(h + 1) * gt]
    p_pad = jnp.concatenate([p_sum, jnp.zeros((LANES - gt, n_chunk), F32)], axis=0)
    p_hi, p_lo = _split_bf16(p_pad)
    cover_t = cover_ref[...]
    imp_t = _dot_nt(cover_t, p_hi) + _dot_nt(cover_t, p_lo)
    n_blk = -(-n_sel // SUBLANES) * SUBLANES
    blk = lax.broadcasted_iota(I32, (n_blk, 1), 0)
    score = _block_scores(imp_t[0:n_blk], blk, qlane_ref[...], n_sel)
    st_ref[0:n_blk, :] = score

    def count_beaten(k, cnt):
        row = st_ref[pl.ds(k, 1), :]
        return cnt + jnp.where((row > score) | ((row == score) & (k < blk)), 1.0, 0.0)

    rank = lax.fori_loop(0, n_sel, count_beaten, jnp.zeros(score.shape, F32), unroll=4)
    st_ref[0:n_blk, :] = jnp.where(rank < n_top, 1.0, 0.0)
    st_ref[n_blk:, :] = jnp.zeros((st_ref.shape[0] - n_blk, LANES), F32)
    selmask = st_ref[...].T[0:gt].astype(BF16)

    n_newl = new_ref.shape[3]
    wcol = lax.broadcasted_iota(I32, (1, win_buf + n_newl), 1)
    kpos_w = jnp.where(wcol < win_buf, past - win_buf + wcol, past + wcol - win_buf)
    d_w = qpos - kpos_w
    valid_w = (d_w >= 0) & (d_w < WINDOW) & (wcol < win_buf + ts)
    kw_t = win_ref[0, 0, 0].reshape(GRP_WIDTH, win_buf).astype(BF16)
    vw_t = win_ref[0, 0, 1].reshape(GRP_WIDTH, win_buf).astype(BF16)
    s_w = jnp.concatenate([_dot(q, kw_t), _dot(q, new_ref[0, 2].astype(BF16))], axis=1)
    p_w = _masked_softmax(s_w - slope * d_w.astype(F32), valid_w).astype(BF16)
    o_w = own_group(_dot_nt(p_w[:, :win_buf], vw_t) + _dot_nt(p_w[:, win_buf:], new_ref[0, 3].astype(BF16)))

    lax.fori_loop(0, n_pages, drain, 0)
    selk = _dot(selmask, expand_ref[...])
    selk = jnp.concatenate([selk] * Q_PER_KV, axis=0)
    d_s = qpos - lax.broadcasted_iota(I32, (1, nk), 1)
    k_t = kt_ref[slot].reshape(GRP_WIDTH, nk).astype(BF16)
    s_s = _dot(q, k_t) - slope * d_s.astype(F32)
    p_s = _masked_softmax(s_s, (selk > 0.5) & (d_s >= 0))
    v_t = vt_ref[...].reshape(GRP_WIDTH, nk).astype(BF16)
    o_s = own_group(_dot_nt(p_s.astype(BF16), v_t))

    gate = gate_ref[0]
    o_ref[0] = gate[:, 0:1] * o_c + gate[:, 1:2] * o_s + gate[:, 2:3] * o_w


def _attn_sample(page_table, qbd, cmp_kv, cache_t, new_t, win_t, gates, slope_rows, qpos_rows, qpos_lanes,
                 cover_t, expand, *, layer, past, ts, n_cmp, n_sel, gt):
    nb, n_pages = page_table.shape
    page = cache_t.shape[-1]
    win_buf = win_t.shape[-1]
    nrow = qbd.shape[1]
    nk = expand.shape[1]
    n_top = min(SEL_TOPK, n_sel)
    assert nk == past + new_t.shape[3] and win_buf % LANES == 0

    def per_seq(shape):
        return pl.BlockSpec((1,) + tuple(shape[1:]), lambda bi, pt: (bi,) + (0,) * (len(shape) - 1))

    def whole(shape):
        return pl.BlockSpec(tuple(shape), lambda bi, pt: (0,) * len(shape))

    grid_spec = pltpu.PrefetchScalarGridSpec(
        num_scalar_prefetch=1,
        grid=(nb,),
        in_specs=[per_seq(qbd.shape),
                  pl.BlockSpec((1, 1) + tuple(cmp_kv.shape[2:]), lambda bi, pt: (0, bi, 0, 0)),
                  pl.BlockSpec((1, 1) + tuple(cmp_kv.shape[2:]), lambda bi, pt: (1, bi, 0, 0)),
                  pl.BlockSpec(memory_space=pl.ANY), per_seq(new_t.shape),
                  pl.BlockSpec((1, 1) + tuple(win_t.shape[2:]), lambda bi, pt: (layer, bi, 0, 0, 0, 0)),
                  per_seq(gates.shape), whole(slope_rows.shape), whole(qpos_rows.shape),
                  whole(qpos_lanes.shape), whole(cover_t.shape), whole(expand.shape)],
        out_specs=pl.BlockSpec((1, nrow, HEAD_DIM), lambda bi, pt: (bi, 0, 0)),
        scratch_shapes=[pltpu.VMEM((2, KV_HEADS, HEAD_DIM, nk), F32), pltpu.VMEM((KV_HEADS, HEAD_DIM, nk), F32),
                        pltpu.VMEM((cover_t.shape[0], LANES), F32), pltpu.SemaphoreType.DMA((3,))],
    )
    return pl.pallas_call(
        functools.partial(_attn_sample_kernel, layer=layer, n_pages=n_pages, page=page, past=past, ts=ts,
                          n_cmp=n_cmp, n_sel=n_sel, n_top=n_top, win_buf=win_buf, gt=gt),
        grid_spec=grid_spec,
        out_shape=jax.ShapeDtypeStruct((nb, nrow, HEAD_DIM), F32),
        compiler_params=_cparams("arbitrary"),
        name="attn_sample",
    )(page_table.reshape(-1), qbd, cmp_kv, cmp_kv, cache_t, new_t, win_t, gates, slope_rows, qpos_rows,
      qpos_lanes, cover_t, expand)


def _merge_kernel(u_ref, yp_ref, ya_ref, wgp_ref, wga_ref, wp_ref, wa_ref, o_ref, wgpb_ref, wgab_ref, wpb_ref,
                  wab_ref):
    @pl.when(pl.program_id(1) == 0)
    def _():
        wgpb_ref[...] = wgp_ref[0].astype(BF16)
        wgab_ref[...] = wga_ref[0].astype(BF16)
        wpb_ref[...] = wp_ref[0].astype(BF16)
        wab_ref[...] = wa_ref[0].astype(BF16)

    u = u_ref[...]
    g_pool = jax.nn.sigmoid(_dot_nt(u, wgpb_ref[...]))
    g_attn = jax.nn.sigmoid(_dot_nt(u, wgab_ref[...]))
    merged = g_pool * _dot(yp_ref[...], wpb_ref[...]) + g_attn * _dot(ya_ref[...], wab_ref[...])
    o_ref[...] = merged.astype(o_ref.dtype)


def _merge(u, y_pool, y_attn, w_gm_t, w_up_pool, w_up_nsa, layer):
    m, d = u.shape
    kp = y_pool.shape[1]
    ka = y_attn.shape[1]
    tn = MM_TILE_N
    nn = d // tn
    row = lambda w: pl.BlockSpec((MM_TILE_M, w), lambda n, i: (i, 0))
    return pl.pallas_call(
        _merge_kernel,
        grid=(nn, m // MM_TILE_M),
        in_specs=[row(d), row(kp), row(ka),
                  pl.BlockSpec((1, tn, d), lambda n, i: (layer, n, 0)),
                  pl.BlockSpec((1, tn, d), lambda n, i: (layer, nn + n, 0)),
                  pl.BlockSpec((1, kp, tn), lambda n, i: (layer, 0, n)),
                  pl.BlockSpec((1, ka, tn), lambda n, i: (layer, 0, n))],
        out_specs=pl.BlockSpec((MM_TILE_M, tn), lambda n, i: (i, n)),
        out_shape=jax.ShapeDtypeStruct((m, d), BF16),
        scratch_shapes=[pltpu.VMEM((tn, d), BF16), pltpu.VMEM((tn, d), BF16),
                        pltpu.VMEM((kp, tn), BF16), pltpu.VMEM((ka, tn), BF16)],
        compiler_params=_cparams("arbitrary", "arbitrary"),
        name="merge",
    )(u, y_pool, y_attn, w_gm_t, w_gm_t, w_up_pool, w_up_nsa)


def _route(logits):
    lane = lax.broadcasted_iota(I32, logits.shape, 1)
    is_grp = lane < N_GROUPS
    p_grp = _masked_softmax(logits, is_grp)
    p_top = jnp.max(p_grp, axis=-1, keepdims=True)
    grp = jnp.min(jnp.where(is_grp & (p_grp == p_top), lane, LANES), axis=-1, keepdims=True)
    first = N_GROUPS + grp * EXPERTS_PER_GROUP
    in_grp = (lane >= first) & (lane < first + EXPERTS_PER_GROUP)
    v = _masked_softmax(logits, in_grp)
    v0 = jnp.max(jnp.where(in_grp, v, -1.0), axis=-1, keepdims=True)
    i0 = jnp.min(jnp.where(in_grp & (v == v0), lane, LANES), axis=-1, keepdims=True)
    rest = in_grp & (lane != i0)
    v1 = jnp.max(jnp.where(rest, v, -1.0), axis=-1, keepdims=True)
    i1 = jnp.min(jnp.where(rest & (v == v1), lane, LANES), axis=-1, keepdims=True)
    tot = v0 + v1
    out = jnp.where(lane == 0, (i0 - N_GROUPS).astype(F32), 0.0)
    out = jnp.where(lane == 1, (i1 - N_GROUPS).astype(F32), out)
    out = jnp.where(lane == 2, p_top * (v0 / tot), out)
    out = jnp.where(lane == 3, p_top * (v1 / tot), out)
    return out


def _outproj_kernel(mg_ref, x_ref, w_ref, mp_ref, ms_ref, g_ref, b_ref, wr_ref, br_ref,
                    x1_ref, u_ref, rt_ref, *, n_ptiles, alpha):
    is_s = pl.program_id(0) >= n_ptiles
    gate1 = _mod_row(is_s, mp_ref, ms_ref, 0)
    shift2 = _mod_row(is_s, mp_ref, ms_ref, 1)
    scale2 = _mod_row(is_s, mp_ref, ms_ref, 2)
    mix = _dot(mg_ref[...], w_ref[...])
    x1 = _layer_norm(alpha * x_ref[...] + gate1 * mix, g_ref[...], b_ref[...])
    x1_ref[...] = x1
    u = x1 * (1.0 + scale2) + shift2
    u_ref[...] = u
    wr_hi, wr_lo = _split_bf16(wr_ref[...])
    u_hi, u_lo = _split_bf16(u)
    logits = _dot_nt(u_hi, wr_hi) + _dot_nt(u_lo, wr_hi) + _dot_nt(u_hi, wr_lo) + br_ref[...]
    rt_ref[...] = _route(logits)


def _outproj(rows, merged, x, w_out_bf, modp, mods, ln_g, ln_b, w_router_t, b_router, alpha):
    d = x.shape[1]
    row = pl.BlockSpec((ROW_TILE, d), lambda i: (i, 0))
    vec = pl.BlockSpec((1, d), lambda i: (0, 0))
    return pl.pallas_call(
        functools.partial(_outproj_kernel, n_ptiles=rows.n_ptiles, alpha=alpha),
        grid=(rows.n_rows // ROW_TILE,),
        in_specs=[row, row, pl.BlockSpec((d, d), lambda i: (0, 0))] + rows.mod_specs(d, 1)
        + [vec, vec, pl.BlockSpec((LANES, d), lambda i: (0, 0)), pl.BlockSpec((1, LANES), lambda i: (0, 0))],
        out_specs=[row, row, pl.BlockSpec((ROW_TILE, LANES), lambda i: (i, 0))],
        out_shape=[jax.ShapeDtypeStruct((rows.n_rows, d), F32),
                   jax.ShapeDtypeStruct((rows.n_rows, d), F32),
                   jax.ShapeDtypeStruct((rows.n_rows, LANES), F32)],
        compiler_params=_cparams("arbitrary"),
        name="outproj_ln_route",
    )(merged, x, w_out_bf, modp, mods, ln_g.reshape(1, d), ln_b.reshape(1, d), w_router_t, b_router)


def _moe_scatter_kernel(dest_ref, pad_ref, u_ref, xs_ref, zero_ref, sem, *, n_real):
    i = pl.program_id(0)
    tm = u_ref.shape[0]

    def pad_copy(r):
        return pltpu.make_async_copy(zero_ref, xs_ref.at[pl.ds(r, 1)], sem)

    @pl.when(i == 0)
    def _():
        zero_ref[...] = jnp.zeros(zero_ref.shape, zero_ref.dtype)
        for e in range(N_EXPERTS + 1):
            lo, hi = pad_ref[2 * e], pad_ref[2 * e + 1]

            def issue_pad(r, c):
                pad_copy(r).start()
                return c

            def drain_pad(r, c):
                pad_copy(r).wait()
                return c

            lax.fori_loop(lo, hi, issue_pad, 0)
            lax.fori_loop(lo, hi, drain_pad, 0)

    def row_copy(r, dst):
        return pltpu.make_async_copy(u_ref.at[pl.ds(r, 1)], xs_ref.at[pl.ds(dst, 1)], sem)

    def issue(r, c):
        base = 2 * (i * tm + r)
        row_copy(r, dest_ref[base]).start()
        row_copy(r, dest_ref[base + 1]).start()
        return c

    def drain(r, c):
        row_copy(r, 0).wait()
        row_copy(r, 0).wait()
        return c

    @pl.when((i + 1) * tm <= n_real)
    def _():
        lax.fori_loop(0, tm, issue, 0, unroll=8)
        lax.fori_loop(0, tm, drain, 0, unroll=8)

    @pl.when((i + 1) * tm > n_real)
    def _():
        lax.fori_loop(0, n_real - i * tm, issue, 0)
        lax.fori_loop(0, n_real - i * tm, drain, 0)


def _moe_scatter(dest, pads, u, n_real, r_pad):
    d = u.shape[1]
    tm = EXPERT_TILE
    grid_spec = pltpu.PrefetchScalarGridSpec(
        num_scalar_prefetch=2,
        grid=(-(-n_real // tm),),
        in_specs=[pl.BlockSpec((tm, d), lambda i, *_: (i, 0))],
        out_specs=pl.BlockSpec(memory_space=pl.ANY),
        scratch_shapes=[pltpu.VMEM((1, d), F32), pltpu.SemaphoreType.DMA(())],
    )
    return pl.pallas_call(
        functools.partial(_moe_scatter_kernel, n_real=n_real),
        grid_spec=grid_spec,
        out_shape=jax.ShapeDtypeStruct((r_pad, d), F32),
        compiler_params=_cparams("arbitrary"),
        name="moe_scatter",
    )(dest, pads, u)


def _moe_expert_kernel(te_ref, used_ref, x_ref, wg_ref, wu_ref, wd_ref, o_ref, wgb_ref, wub_ref, wdb_ref):
    t = pl.program_id(0)
    fresh = jnp.logical_or(t == 0, te_ref[t] != te_ref[jnp.maximum(t - 1, 0)])

    @pl.when(fresh)
    def _():
        wgb_ref[...] = wg_ref[0, 0].astype(BF16)
        wub_ref[...] = wu_ref[0, 0].astype(BF16)
        wdb_ref[...] = wd_ref[0, 0].astype(BF16)

    @pl.when(t < used_ref[0])
    def _():
        x = x_ref[...].astype(BF16)
        h = jax.nn.silu(_dot(x, wgb_ref[...])) * _dot(x, wub_ref[...])
        o_ref[...] = _dot(h.astype(BF16), wdb_ref[...])

    @pl.when(t >= used_ref[0])
    def _():
        o_ref[...] = jnp.zeros(o_ref.shape, o_ref.dtype)


def _moe_experts(tile_expert, n_used, xs, w_gate, w_up, w_down, layer):
    r_pad, d = xs.shape
    f = w_gate.shape[3]
    tm = EXPERT_TILE
    n_tiles = r_pad // tm

    def x_map(t, te, nu):
        return (jnp.minimum(t, jnp.maximum(nu[0] - 1, 0)), 0)

    grid_spec = pltpu.PrefetchScalarGridSpec(
        num_scalar_prefetch=2,
        grid=(n_tiles,),
        in_specs=[pl.BlockSpec((tm, d), x_map),
                  pl.BlockSpec((1, 1, d, f), lambda t, te, nu: (layer, te[t], 0, 0)),
                  pl.BlockSpec((1, 1, d, f), lambda t, te, nu: (layer, te[t], 0, 0)),
                  pl.BlockSpec((1, 1, f, d), lambda t, te, nu: (layer, te[t], 0, 0))],
        out_specs=pl.BlockSpec((tm, d), lambda t, te, nu: (t, 0)),
        scratch_shapes=[pltpu.VMEM((d, f), BF16), pltpu.VMEM((d, f), BF16), pltpu.VMEM((f, d), BF16)],
    )
    return pl.pallas_call(
        _moe_expert_kernel,
        grid_spec=grid_spec,
        out_shape=jax.ShapeDtypeStruct((r_pad, d), F32),
        compiler_params=_cparams("arbitrary"),
        name="moe_experts",
    )(tile_expert, n_used, xs, w_gate, w_up, w_down)


def _moe_combine_kernel(pos_ref, y_ref, x1_ref, rt_ref, mp_ref, ms_ref, g_ref, b_ref, x2_ref, *rest,
                        n_ptiles, alpha, emit_next):
    if emit_next:
        u_ref, buf_ref, sem = rest
    else:
        buf_ref, sem = rest
    i = pl.program_id(0)
    tm = x1_ref.shape[0]
    slot = i % 2

    def row_copy(sl, r, k, src):
        return pltpu.make_async_copy(y_ref.at[pl.ds(src, 1)], buf_ref.at[sl, k, pl.ds(r, 1)], sem.at[sl])

    def fetch(tile, sl):
        def issue(r, c):
            base = 2 * (tile * tm + r)
            row_copy(sl, r, 0, pos_ref[base]).start()
            row_copy(sl, r, 1, pos_ref[base + 1]).start()
            return c

        lax.fori_loop(0, tm, issue, 0, unroll=8)

    @pl.when(i == 0)
    def _():
        fetch(i, slot)

    @pl.when(i + 1 < pl.num_programs(0))
    def _():
        fetch(i + 1, 1 - slot)

    def drain(r, c):
        row_copy(slot, r, 0, 0).wait()
        row_copy(slot, r, 1, 0).wait()
        return c

    lax.fori_loop(0, tm, drain, 0, unroll=8)
    is_s = i >= n_ptiles
    gate2 = _mod_row(is_s, mp_ref, ms_ref, 0)
    ffn = rt_ref[:, 2:3] * buf_ref[slot, 0] + rt_ref[:, 3:4] * buf_ref[slot, 1]
    x2 = _layer_norm(alpha * x1_ref[...] + gate2 * ffn, g_ref[...], b_ref[...])
    x2_ref[...] = x2
    if emit_next:
        shift = _mod_row(is_s, mp_ref, ms_ref, 1)
        scale = _mod_row(is_s, mp_ref, ms_ref, 2)
        u_ref[...] = (x2 * (1.0 + scale) + shift).astype(u_ref.dtype)


def _moe_combine(rows, pos, y_rows, x1, route, modp, mods, ln_g, ln_b, alpha, emit_next):
    d = x1.shape[1]
    row = pl.BlockSpec((ROW_TILE, d), lambda i, *_: (i, 0))
    vec = pl.BlockSpec((1, d), lambda i, *_: (0, 0))
    out_specs = [row]
    out_shape = [jax.ShapeDtypeStruct((rows.n_rows, d), F32)]
    if emit_next:
        out_specs.append(row)
        out_shape.append(jax.ShapeDtypeStruct((rows.n_rows, d), BF16))
    grid_spec = pltpu.PrefetchScalarGridSpec(
        num_scalar_prefetch=1,
        grid=(rows.n_rows // ROW_TILE,),
        in_specs=[pl.BlockSpec(memory_space=pl.ANY), row, pl.BlockSpec((ROW_TILE, LANES), lambda i, *_: (i, 0))]
        + rows.mod_specs(d, 2) + [vec, vec],
        out_specs=out_specs,
        scratch_shapes=[pltpu.VMEM((2, 2, ROW_TILE, d), F32), pltpu.SemaphoreType.DMA((2,))],
    )
    out = pl.pallas_call(
        functools.partial(_moe_combine_kernel, n_ptiles=rows.n_ptiles, alpha=alpha, emit_next=emit_next),
        grid_spec=grid_spec,
        out_shape=out_shape,
        compiler_params=_cparams("arbitrary"),
        name="moe_combine_ln",
    )(pos, y_rows, x1, route, modp, mods, ln_g.reshape(1, d), ln_b.reshape(1, d))
    return out if emit_next else (out[0], None)


def _moe_dispatch(route, n_real, n_rows):
    tm = EXPERT_TILE
    n_pairs = 2 * n_real
    n_tiles = -(-(n_pairs + N_EXPERTS * (tm - 1)) // tm)
    eid = route[:n_real, 0:2].astype(I32).reshape(n_pairs, 1)
    onehot = (eid == jnp.arange(N_EXPERTS, dtype=I32)[None, :]).astype(I32)
    seen = jnp.cumsum(onehot, axis=0)
    counts = seen[-1]
    padded = -(-counts // tm) * tm
    ends_pad = jnp.cumsum(padded)
    starts_pad = ends_pad - padded
    dest = jnp.sum(onehot * (starts_pad[None, :] + seen - 1), axis=1).astype(I32)
    pos = jnp.concatenate([dest, jnp.zeros((2 * (n_rows - n_real),), I32)])
    pads = jnp.stack([starts_pad + counts, ends_pad], axis=1).reshape(-1)
    pads = jnp.concatenate([pads, ends_pad[-1:], jnp.full((1,), n_tiles * tm)]).astype(I32)
    n_used = (ends_pad[-1] // tm).astype(I32)
    tile_start = jnp.arange(n_tiles, dtype=I32) * tm
    tile_e = jnp.minimum(jnp.sum((tile_start[:, None] >= ends_pad[None, :]).astype(I32), axis=1), N_EXPERTS - 1)
    last_e = jnp.sum(jnp.where(jnp.arange(n_tiles) == jnp.maximum(n_used - 1, 0), tile_e, 0))
    tile_e = jnp.where(jnp.arange(n_tiles) < n_used, tile_e, last_e).astype(I32)
    return pos, pads, tile_e, n_used.reshape(1), n_tiles * tm


def _cover_matrix(n_chunk, n_cmp, n_sel, n_sel_pad):
    ci = jnp.arange(n_chunk)[:, None]
    sj = jnp.arange(n_sel_pad)[None, :]
    hit = ((ci * CMP_STRIDE < sj * SEL_BLOCK + SEL_BLOCK)
           & (ci * CMP_STRIDE + CMP_LEN - 1 >= sj * SEL_BLOCK)
           & (ci < n_cmp) & (sj < n_sel))
    return hit.astype(BF16)


def kernel(x_prompt, x_sample, c_prompt, c_sample, cache_kv, cache_win, state_pool, page_table, w_ada, b_ada,
           w_in, cmp_w1, cmp_pe, cmp_w2, pool_w, pool_scale, w_up_pool, w_up_nsa, w_out, ln1_g, ln1_b, w_rg,
           b_rg, w_re, b_re, w_gate, w_up, w_down, ln2_g, ln2_b):
    batch, seq, d = x_prompt.shape
    nb, ts, _ = x_sample.shape
    n_layers = w_ada.shape[0]
    page = cache_kv.shape[2]
    n_pages = page_table.shape[1]
    past = n_pages * page
    win_buf = cache_win.shape[2]
    pool_width = pool_w.shape[1] * pool_w.shape[2]
    q_width = N_HEADS * HEAD_DIM
    kv_width = N_KV_PLANES * GRP_WIDTH
    gn_width = 3 * N_HEADS
    alpha = (2 * n_layers) ** 0.25
    cmp_hid = cmp_w1.shape[-1]
    chunk_w = CMP_STRIDE * HEAD_DIM

    rows = _Rows(batch, seq, nb * ts)
    n_p, n_s, n_rows = rows.n_prompt, rows.n_sample, rows.n_rows
    n_real = n_p + n_s

    x = jnp.concatenate([x_prompt.reshape(n_p, d), x_sample.reshape(n_s, d),
                         jnp.zeros((rows.sample_pad - n_s, d), F32)], axis=0)

    n_seq = batch + nb
    c_all = jnp.concatenate([c_prompt, c_sample, jnp.zeros((-n_seq % SUBLANES, d), F32)], axis=0)
    mod = _adaln(c_all, w_ada, b_ada)[:, :n_seq].reshape(n_layers, n_seq, 6, d)
    nxt = jnp.concatenate([mod[1:], jnp.zeros_like(mod[:1])], axis=0)
    mod9 = jnp.stack([mod[:, :, 0], mod[:, :, 1], jnp.zeros_like(mod[:, :, 0]),
                      mod[:, :, 2], mod[:, :, 3], mod[:, :, 4],
                      mod[:, :, 5], nxt[:, :, 0], nxt[:, :, 1]], axis=2).reshape(n_layers, n_seq, 3, 3, d)
    modp_all = jnp.pad(mod9[:, :batch], ((0, 0), (0, 0), (0, 0), (0, SUBLANES - 3), (0, 0)))
    mods_all = jnp.repeat(mod9[:, batch:], ts, axis=1).transpose(0, 2, 3, 1, 4)
    mods_all = jnp.pad(mods_all, ((0, 0), (0, 0), (0, 0), (0, rows.sample_pad - n_s), (0, 0)))

    w_in_t = w_in.transpose(0, 2, 1)
    o1 = pool_width
    o2 = o1 + q_width
    o3 = o2 + kv_width
    o4 = o3 + gn_width
    w_gm_t = w_in_t[:, o4:]
    w_gn_t = jnp.pad(w_in_t[:, o3:o4], ((0, 0), (0, -gn_width % LANES), (0, 0)))
    w_out_bf = w_out.astype(BF16)
    n_rt = N_GROUPS + N_EXPERTS
    w_router_t = jnp.concatenate([w_rg.transpose(0, 2, 1), w_re.transpose(0, 2, 1),
                                  jnp.zeros((n_layers, LANES - n_rt, d), F32)], axis=1)
    b_router = jnp.concatenate([b_rg, b_re, jnp.zeros((n_layers, LANES - n_rt), F32)], axis=1)
    cache_t = cache_kv.transpose(0, 1, 3, 4, 5, 2)
    win_t = cache_win.transpose(0, 1, 3, 4, 5, 2)
    state_pm = state_pool.transpose(0, 2, 1, 3)
    w2_t = cmp_w2.transpose(0, 1, 3, 2)

    slopes = jnp.exp2(-8.0 * jnp.arange(1, N_HEADS + 1, dtype=F32) / N_HEADS)
    nch_p = seq // CMP_STRIDE
    ncmp_p = (seq - CMP_LEN) // CMP_STRIDE + 1
    nsel_p = -(-seq // SEL_BLOCK)
    cover_p_t = _cover_matrix(nch_p, ncmp_p, nsel_p, -(-nsel_p // LANES) * LANES).T

    tk_s = past + ts
    ncmp_s = (tk_s - CMP_LEN) // CMP_STRIDE + 1
    nch_s = past // CMP_STRIDE
    assert (ncmp_s + CMP_SEGS - 1) * CMP_STRIDE <= past and nch_s >= ncmp_s
    assert page % CMP_STRIDE == 0 and page % SEL_BLOCK == 0 and page % LANES == 0 and ts <= LANES
    nsel_s = -(-tk_s // SEL_BLOCK)
    nsel_s_pad = -(-nsel_s // LANES) * LANES
    cover_s_t = _cover_matrix(nch_s, ncmp_s, nsel_s, nsel_s_pad).T
    nk_s = past + LANES
    expand_s = (jnp.arange(nsel_s_pad)[:, None] == jnp.arange(nk_s)[None, :] // SEL_BLOCK).astype(BF16)
    gt = KV_HEADS * ts
    slopes_gh = slopes.reshape(KV_HEADS, Q_PER_KV)
    slope_s = jnp.broadcast_to(slopes_gh.T[:, :, None], (Q_PER_KV, KV_HEADS, ts)).reshape(Q_PER_KV * gt, 1)
    qpos_s = jnp.broadcast_to(past + jnp.arange(ts, dtype=I32), (Q_PER_KV, KV_HEADS, ts)).reshape(Q_PER_KV * gt, 1)
    assert gt <= LANES
    qlane_s = jnp.pad(qpos_s[:gt, 0], (0, LANES - gt)).reshape(1, LANES)
    eye_g = jnp.eye(KV_HEADS, dtype=BF16)

    u = _modulate(rows, x, modp_all[0], mods_all[0])
    outs = {k: [] for k in ("kv_p", "win_p", "pool_p", "kv_s", "win_s", "pool_s")}
    for l in range(n_layers):
        up = _proj_rows(u, w_in_t, l, 0, pool_width, MM_TILE_N, F32)
        q = _proj_rows(u, w_in_t, l, o1, q_width, MM_TILE_N, BF16, scale=HEAD_DIM ** -0.5)
        kvc = _proj_rows(u, w_in_t, l, o2, 2 * GRP_WIDTH, MM_TILE_N, F32)
        kv_rest_s = _proj_rows(u, w_in_t, l, o2 + 2 * GRP_WIDTH, 4 * GRP_WIDTH, MM_TILE_N, F32,
                               m0=n_p, m_rows=rows.sample_pad)
        gn = _proj_rows(u, w_gn_t, l, 0, LANES, LANES, F32, act="sigmoid")
        kv_t = _proj_t(u, w_in_t, l, o2, N_CACHED_PLANES, batch, seq)
        kw_t = _proj_t(u, w_in_t, l, o2 + N_CACHED_PLANES * GRP_WIDTH, 2, batch, seq)

        kv_s = jnp.concatenate([kvc[n_p:n_real], kv_rest_s[:n_s]], axis=1)
        kv_s = kv_s.reshape(nb, ts, N_KV_PLANES, KV_HEADS, HEAD_DIM)
        up_p = up[:n_p].reshape(batch, seq, pool_width)
        up_s = up[n_p:n_real].reshape(nb, ts, pool_width)
        outs["kv_p"].append(kv_t.reshape(batch, N_CACHED_PLANES, KV_HEADS, HEAD_DIM, seq))
        n_keep = min(WINDOW, seq)
        outs["win_p"].append(kw_t[:, :, :, seq - n_keep:].reshape(batch, 2, KV_HEADS, HEAD_DIM, n_keep))
        hist_p = jnp.concatenate([jnp.zeros((batch, POOL_HIST, pool_width), F32), up_p[:, -POOL_HIST:]], axis=1)
        outs["pool_p"].append(hist_p[:, -POOL_HIST:].transpose(1, 0, 2))
        new_t = kv_s.transpose(0, 2, 3, 4, 1)
        win_new = jnp.concatenate([win_t[l], new_t[:, N_CACHED_PLANES:]], axis=-1)[..., -win_buf:]
        outs["kv_s"].append(kv_s[:, :, :N_CACHED_PLANES])
        outs["win_s"].append(win_new)
        pool_ext = jnp.concatenate([state_pm[l], up_s.transpose(1, 0, 2)], axis=0)
        outs["pool_s"].append(pool_ext[-POOL_HIST:])

        yp_p = _pool_prompt(up, batch, seq, pool_w[l], pool_scale[l])
        yp_s = _pool_sample(pool_ext, ts, past, pool_w[l], pool_scale[l])
        y_pool = jnp.concatenate([yp_p, yp_s.transpose(1, 0, 2).reshape(n_s, pool_width),
                                  jnp.zeros((rows.sample_pad - n_s, pool_width), BF16)], axis=0)

        w1 = cmp_w1[l].reshape(2, CMP_SEGS, chunk_w, cmp_hid)
        pe = cmp_pe[l].reshape(2, CMP_SEGS, 1, chunk_w)
        chunks_p = kvc[:n_p].reshape(batch, nch_p, CMP_STRIDE, 2, KV_HEADS, HEAD_DIM)
        chunks_p = chunks_p.transpose(3, 0, 4, 1, 2, 5).reshape(2, batch, KV_HEADS, nch_p, chunk_w)
        cmp_rows_p, cmp_t_p = _compress_prompt(chunks_p, w1, pe, w2_t[l])
        cmp_s = _compress_sample(page_table, cache_t, l, w1, pe, w2_t[l])

        gates_p = gn[:n_p, :gn_width].reshape(batch, seq, KV_HEADS, 3 * Q_PER_KV).transpose(0, 2, 3, 1)
        ya_p = _attn_prompt(slopes, q, batch, seq, cmp_rows_p, cmp_t_p, kv_t, kw_t, gates_p, cover_p_t)

        q_s = q[n_p:n_real].reshape(nb, ts, KV_HEADS, Q_PER_KV, HEAD_DIM).transpose(0, 3, 2, 1, 4)
        qbd = (q_s[:, :, :, :, None, :] * eye_g[None, None, :, None, :, None]).reshape(nb, Q_PER_KV * gt, GRP_WIDTH)
        new_blk = jnp.pad(new_t[:, 2:].reshape(nb, 4, GRP_WIDTH, ts), ((0, 0), (0, 0), (0, 0), (0, LANES - ts)))
        gates_s = gn[n_p:n_real, :gn_width].reshape(nb, ts, KV_HEADS, Q_PER_KV, 3).transpose(0, 3, 2, 1, 4)
        gates_s = gates_s.reshape(nb, Q_PER_KV * gt, 3)
        oa_s = _attn_sample(page_table, qbd, cmp_s, cache_t, new_blk, win_t, gates_s,
                            slope_s, qpos_s, qlane_s, cover_s_t, expand_s, layer=l, past=past, ts=ts, n_cmp=ncmp_s,
                            n_sel=nsel_s, gt=gt)
        ya_s = oa_s.reshape(nb, Q_PER_KV, KV_HEADS, ts, HEAD_DIM).transpose(0, 3, 2, 1, 4).reshape(n_s, q_width)
        y_attn = jnp.concatenate([ya_p, ya_s.astype(BF16),
                                  jnp.zeros((rows.sample_pad - n_s, q_width), BF16)], axis=0)

        merged = _merge(u, y_pool, y_attn, w_gm_t, w_up_pool, w_up_nsa, l)
        x1, u2, route = _outproj(rows, merged, x, w_out_bf[l], modp_all[l], mods_all[l], ln1_g[l], ln1_b[l],
                                 w_router_t[l], b_router[l].reshape(1, LANES), alpha)

        pos, pads, tile_e, n_used, r_pad = _moe_dispatch(route, n_real, n_rows)
        xs = _moe_scatter(pos, pads, u2, n_real, r_pad)
        y_rows = _moe_experts(tile_e, n_used, xs, w_gate, w_up, w_down, l)
        x, u = _moe_combine(rows, pos, y_rows, x1, route, modp_all[l], mods_all[l], ln2_g[l], ln2_b[l], alpha,
                            emit_next=l + 1 < n_layers)

    y_prompt = x[:n_p].reshape(batch, seq, d)
    y_sample = x[n_p:n_real].reshape(nb, ts, d)
    new_kv_p = jnp.stack(outs["kv_p"]).transpose(0, 1, 5, 2, 3, 4)
    new_win_p = jnp.stack(outs["win_p"]).transpose(0, 1, 5, 2, 3, 4)
    new_pool_p = jnp.stack(outs["pool_p"]).transpose(0, 2, 1, 3)
    new_win_s = jnp.stack(outs["win_s"]).transpose(0, 1, 5, 2, 3, 4)
    new_pool_s = jnp.stack(outs["pool_s"]).transpose(0, 2, 1, 3)
    return (y_prompt, y_sample, new_kv_p, new_win_p, new_pool_p, jnp.stack(outs["kv_s"]), new_win_s, new_pool_s)
```
